```python
import math
import jax, jax.numpy as jnp
from jax import lax
import numpy as np

D_MODEL = 2048
BATCH = 1
SEQ = 16384
DEPTH = 1

HEAD_DIM = 64
N_Q_HEADS = 16
N_KV_HEADS = 2
Q_PER_KV = N_Q_HEADS // N_KV_HEADS
D_ATTN = N_Q_HEADS * HEAD_DIM
D_KV = N_KV_HEADS * HEAD_DIM
D_CONV = D_MODEL - D_ATTN
D_MIX = D_ATTN + D_CONV
D_IN = D_ATTN + 2 * D_KV + 2 * D_CONV
WINDOW = 128
BLOCK = 128
CONV_WIDTH = 31
N_BUCKETS = 32
MAX_DISTANCE = 128
N_EXPERTS = 32
TOP_K = 4
D_FF = D_MODEL
SWIGLU_LIMIT = 7.0
SWIGLU_ALPHA = 1.702
MOE_BLOCK = 128
EPS = 1e-6
NEG_INF = -1e30

kernel_name = "hymba_swa_conformer_moe_adaln"


def rms_norm(x, g):
    xf = x.astype(jnp.float32)
    y = xf * lax.rsqrt(jnp.mean(xf * xf, axis=-1, keepdims=True) + EPS)
    return (y * g.astype(jnp.float32)).astype(x.dtype)


def layer_norm(x, g, b):
    xf = x.astype(jnp.float32)
    mu = jnp.mean(xf, axis=-1, keepdims=True)
    var = jnp.mean(jnp.square(xf - mu), axis=-1, keepdims=True)
    y = (xf - mu) * lax.rsqrt(var + EPS)
    return (y * g.astype(jnp.float32) + b.astype(jnp.float32)).astype(x.dtype)


def t5_causal_bucket(dist):
    max_exact = N_BUCKETS // 2
    d = jnp.maximum(dist, 0)
    log_ratio = jnp.log(jnp.maximum(d, max_exact).astype(jnp.float32) / max_exact)
    large = max_exact + (log_ratio / math.log(MAX_DISTANCE / max_exact)
                         * (N_BUCKETS - max_exact)).astype(jnp.int32)
    large = jnp.minimum(large, N_BUCKETS - 1)
    return jnp.where(d < max_exact, d, large)


def sliding_window_attention(q, k, v, g_q, g_k, sinks, rel_bias):
    B, T = q.shape[0], q.shape[1]
    nb = T // BLOCK
    q = rms_norm(q, g_q)
    k = rms_norm(k, g_k)
    qb = q.reshape(B, nb, BLOCK, N_KV_HEADS, Q_PER_KV, HEAD_DIM)

    def with_prev(t):
        tb = t.reshape(B, nb, BLOCK, N_KV_HEADS, HEAD_DIM)
        prev = jnp.pad(tb[:, :-1], ((0, 0), (1, 0), (0, 0), (0, 0), (0, 0)))
        return jnp.concatenate([prev, tb], axis=2)

    kb = with_prev(k)
    vb = with_prev(v)
    scores = jnp.einsum('bnqhgd,bnkhd->bnhgqk', qb, kb).astype(jnp.float32) * (HEAD_DIM ** -0.5)

    q_local = jnp.arange(BLOCK, dtype=jnp.int32) + BLOCK
    k_local = jnp.arange(2 * BLOCK, dtype=jnp.int32)
    dist = q_local[:, None] - k_local[None, :]
    band = (dist >= 0) & (dist < WINDOW)
    has_prev = (jnp.arange(nb)[:, None, None] > 0) | (k_local >= BLOCK)[None, None, :]
    valid = band[None] & has_prev

    bias = rel_bias.astype(jnp.float32)[t5_causal_bucket(dist)]
    bias = jnp.transpose(bias, (2, 0, 1)).reshape(N_KV_HEADS, Q_PER_KV, BLOCK, 2 * BLOCK)
    scores = jnp.where(valid[None, :, None, None], scores + bias[None, None], NEG_INF)

    s = sinks.astype(jnp.float32).reshape(N_KV_HEADS, Q_PER_KV)[None, None, :, :, None, None]
    m = jnp.maximum(jnp.max(scores, axis=-1, keepdims=True), s)
    p = jnp.exp(scores - m)
    p = p / (jnp.sum(p, axis=-1, keepdims=True) + jnp.exp(s - m))
    out = jnp.einsum('bnhgqk,bnkhd->bnqhgd', p.astype(vb.dtype), vb)
    return out.reshape(B, T, N_Q_HEADS * HEAD_DIM)


def conformer_conv(u, w_dw, b_dw, ln_g, ln_b):
    a, gate = jnp.split(u, 2, axis=-1)
    h = a * jax.nn.sigmoid(gate)
    h = lax.conv_general_dilated(
        h, w_dw[:, None, :].astype(h.dtype), window_strides=(1,),
        padding=[(CONV_WIDTH - 1, 0)],
        dimension_numbers=('NWC', 'WIO', 'NWC'),
        feature_group_count=D_CONV) + b_dw
    h = layer_norm(h, ln_g, ln_b)
    return jax.nn.silu(h)


def moe_ffn(h, w_router, b_router, w_gate, b_gate, w_up, b_up, w_down, b_down):
    B, T, D = h.shape
    n_tok = B * T
    xt = h.reshape(n_tok, D)
    logits = (xt @ w_router + b_router).astype(jnp.float32)
    top_val, top_idx = lax.top_k(logits, TOP_K)
    top_w = jax.nn.softmax(top_val, axis=-1)

    n_assign = n_tok * TOP_K
    expert_flat = top_idx.reshape(-1).astype(jnp.int32)
    token_flat = jnp.repeat(jnp.arange(n_tok, dtype=jnp.int32), TOP_K)
    weight_flat = top_w.reshape(-1)
    order = jnp.argsort(expert_flat)
    exp_sorted = expert_flat[order]
    tok_sorted = token_flat[order]
    w_sorted = weight_flat[order]

    counts = jnp.bincount(expert_flat, length=N_EXPERTS)
    padded = ((counts + MOE_BLOCK - 1) // MOE_BLOCK) * MOE_BLOCK
    start = jnp.cumsum(counts) - counts
    pend = jnp.cumsum(padded)
    pstart = pend - padded
    dest = pstart[exp_sorted] + (jnp.arange(n_assign, dtype=jnp.int32) - start[exp_sorted])

    n_rows = ((n_assign + MOE_BLOCK - 1) // MOE_BLOCK) * MOE_BLOCK + N_EXPERTS * MOE_BLOCK
    n_blocks = n_rows // MOE_BLOCK
    row_tok = jnp.full((n_rows,), n_tok, jnp.int32).at[dest].set(tok_sorted)
    row_w = jnp.zeros((n_rows,), jnp.float32).at[dest].set(w_sorted)
    block_expert = jnp.minimum(
        jnp.searchsorted(pend, jnp.arange(n_blocks, dtype=jnp.int32) * MOE_BLOCK, side='right'),
        N_EXPERTS - 1).astype(jnp.int32)

    x_pad = jnp.concatenate([xt, jnp.zeros((1, D), xt.dtype)], axis=0)

    def expert_block(args):
        tok, e = args
        xb = x_pad[tok]
        g = xb @ w_gate[e] + b_gate[e]
        lin = xb @ w_up[e] + b_up[e]
        g = jnp.minimum(g, SWIGLU_LIMIT)
        lin = jnp.clip(lin, -SWIGLU_LIMIT, SWIGLU_LIMIT)
        act = g * jax.nn.sigmoid(SWIGLU_ALPHA * g) * (lin + 1.0)
        return act @ w_down[e] + b_down[e]

    out_rows = lax.map(expert_block, (row_tok.reshape(n_blocks, MOE_BLOCK), block_expert))
    out_rows = out_rows.reshape(n_rows, D) * row_w[:, None].astype(out_rows.dtype)
    y = jnp.zeros((n_tok + 1, D), out_rows.dtype).at[row_tok].add(out_rows)[:n_tok]
    return y.reshape(B, T, D)


def setup_inputs(seed: int = 0) -> dict:
    key = jax.random.key(seed)
    ks = jax.random.split(key, 32)
    f32 = jnp.float32
    L, D, E, F = DEPTH, D_MODEL, N_EXPERTS, D_FF

    def nrm(k, shape, scale):
        return jax.random.normal(k, shape, f32) * scale

    return {
        "x": nrm(ks[0], (BATCH, SEQ, D), 1.0),
        "c": nrm(ks[1], (BATCH, D), 1.0),
        "w_ada": nrm(ks[2], (L, D, 6 * D), 0.5 * D ** -0.5),
        "b_ada": nrm(ks[3], (L, 6 * D), 0.02),
        "g_norm1": 1.0 + nrm(ks[4], (L, D), 0.02),
        "w_in": nrm(ks[5], (L, D, D_IN), D ** -0.5),
        "b_in": nrm(ks[6], (L, D_IN), 0.02),
        "g_q": 1.0 + nrm(ks[7], (L, HEAD_DIM), 0.02),
        "g_k": 1.0 + nrm(ks[8], (L, HEAD_DIM), 0.02),
        "sinks": nrm(ks[9], (L, N_Q_HEADS), 1.0),
        "rel_bias": nrm(ks[10], (N_BUCKETS, N_Q_HEADS), 0.5),
        "w_dw": nrm(ks[11], (L, CONV_WIDTH, D_CONV), CONV_WIDTH ** -0.5),
        "b_dw": nrm(ks[12], (L, D_CONV), 0.02),
        "ln_g": 1.0 + nrm(ks[13], (L, D_CONV), 0.02),
        "ln_b": nrm(ks[14], (L, D_CONV), 0.02),
        "g_out_attn": 1.0 + nrm(ks[15], (L, D_ATTN), 0.02),
        "g_out_conv": 1.0 + nrm(ks[16], (L, D_CONV), 0.02),
        "w_out": nrm(ks[17], (L, D_MIX, D), D_MIX ** -0.5),
        "b_out": nrm(ks[18], (L, D), 0.02),
        "g_norm2": 1.0 + nrm(ks[19], (L, D), 0.02),
        "w_router": nrm(ks[20], (L, D, E), D ** -0.5),
        "b_router": nrm(ks[21], (L, E), 0.01),
        "w_gate": nrm(ks[22], (L, E, D, F), D ** -0.5),
        "b_gate": nrm(ks[23], (L, E, F), 0.02),
        "w_up": nrm(ks[24], (L, E, D, F), D ** -0.5),
        "b_up": nrm(ks[25], (L, E, F), 0.02),
        "w_down": nrm(ks[26], (L, E, F, D), F ** -0.5),
        "b_down": nrm(ks[27], (L, E, D), 0.02),
    }


def reference(x, c, w_ada, b_ada, g_norm1, w_in, b_in, g_q, g_k, sinks, rel_bias,
              w_dw, b_dw, ln_g, ln_b, g_out_attn, g_out_conv, w_out, b_out, g_norm2,
              w_router, b_router, w_gate, b_gate, w_up, b_up, w_down, b_down):
    B, T, _ = x.shape
    for l in range(DEPTH):
        mod = jax.nn.silu(c) @ w_ada[l] + b_ada[l]
        shift1, scale1, gate1, shift2, scale2, gate2 = jnp.split(mod[:, None, :], 6, axis=-1)

        h = rms_norm(x, g_norm1[l]) * (1.0 + scale1) + shift1
        u = h @ w_in[l] + b_in[l]
        q = u[..., :D_ATTN].reshape(B, T, N_Q_HEADS, HEAD_DIM)
        k = u[..., D_ATTN:D_ATTN + D_KV].reshape(B, T, N_KV_HEADS, HEAD_DIM)
        v = u[..., D_ATTN + D_KV:D_ATTN + 2 * D_KV].reshape(B, T, N_KV_HEADS, HEAD_DIM)
        u_conv = u[..., D_ATTN + 2 * D_KV:]

        y_attn = sliding_window_attention(q, k, v, g_q[l], g_k[l], sinks[l], rel_bias)
        y_conv = conformer_conv(u_conv, w_dw[l], b_dw[l], ln_g[l], ln_b[l])
        mixed = jnp.concatenate([rms_norm(y_attn, g_out_attn[l]),
                                 rms_norm(y_conv, g_out_conv[l])], axis=-1)
        x = x + gate1 * (mixed @ w_out[l] + b_out[l])

        h2 = rms_norm(x, g_norm2[l]) * (1.0 + scale2) + shift2
        y_moe = moe_ffn(h2, w_router[l], b_router[l], w_gate[l], b_gate[l],
                        w_up[l], b_up[l], w_down[l], b_down[l])
        x = x + gate2 * y_moe
    return x
```

```python
import functools
import math

import jax
import jax.numpy as jnp
from jax import lax
from jax.experimental import pallas as pl
from jax.experimental.pallas import tpu as pltpu

D_MODEL = 2048
HEAD_DIM = 64
N_Q_HEADS = 16
N_KV_HEADS = 2
D_ATTN = N_Q_HEADS * HEAD_DIM
D_KV = N_KV_HEADS * HEAD_DIM
D_CONV = D_MODEL - D_ATTN
D_IN = D_ATTN + 2 * D_KV + 2 * D_CONV
WINDOW = 128
BLOCK = 128
CONV_WIDTH = 31
N_BUCKETS = 32
MAX_DISTANCE = 128
N_EXPERTS = 32
TOP_K = 4
D_FF = D_MODEL
SWIGLU_LIMIT = 7.0
SWIGLU_ALPHA = 1.702
EPS = 1e-6
NEG_INF = -1e30

LANES = 128
VMEM_LIMIT = 56 * 1024 * 1024

HALO = 32
EXPERT_TM = 256
F32 = jnp.float32
BF16 = jnp.bfloat16


def _params(sem):
    return pltpu.CompilerParams(dimension_semantics=sem, vmem_limit_bytes=VMEM_LIMIT)


ADA_TN = 512
ADA_RC = 256


def _ada_kernel(c_ref, w_ref, b_ref, o_ref):
    tn = w_ref.shape[1]

    def body(i, acc):
        r = pl.multiple_of(i * ADA_RC, ADA_RC)
        c = c_ref[pl.ds(r, ADA_RC), :]
        sc = c * jax.nn.sigmoid(c)
        prod = w_ref[pl.ds(r, ADA_RC), :] * sc
        return acc + jnp.sum(prod.reshape(ADA_RC // 8, 8, tn), axis=0)

    acc = lax.fori_loop(0, D_MODEL // ADA_RC, body, jnp.zeros((8, tn), F32))
    o_ref[...] = jnp.sum(acc, axis=0, keepdims=True) + b_ref[...]


def _ada(c, w_ada, b_ada):
    n = w_ada.shape[1]
    return pl.pallas_call(
        _ada_kernel,
        grid=(n // ADA_TN,),
        in_specs=[
            pl.BlockSpec((D_MODEL, 1), lambda j: (0, 0)),
            pl.BlockSpec((D_MODEL, ADA_TN), lambda j: (0, j)),
            pl.BlockSpec((1, ADA_TN), lambda j: (0, j)),
        ],
        out_specs=pl.BlockSpec((1, ADA_TN), lambda j: (0, j)),
        out_shape=jax.ShapeDtypeStruct((1, n), F32),
        compiler_params=_params(("parallel",)),
        name="ada",
    )(c.reshape(D_MODEL, 1), w_ada, b_ada.reshape(1, n))


IN_TM = 512
IN_NC = 256


def _modulated_rms(x, g, scale, shift):
    ms = jnp.mean(x * x, axis=-1, keepdims=True)
    return (x * lax.rsqrt(ms + EPS) * g) * (1.0 + scale) + shift


def _in_kernel(x_ref, g_ref, sc_ref, sh_ref, w_ref, b_ref, q_ref, kv_ref, uc_ref):
    h = _modulated_rms(x_ref[...], g_ref[...], sc_ref[...], sh_ref[...]).astype(BF16)

    def proj(lo, n, o_ref):
        for c in range(0, n, IN_NC):
            u = jnp.dot(h, w_ref[:, lo + c:lo + c + IN_NC], preferred_element_type=F32)
            o_ref[:, c:c + IN_NC] = u + b_ref[:, lo + c:lo + c + IN_NC]

    proj(0, D_ATTN, q_ref)
    proj(D_ATTN, 2 * D_KV, kv_ref)
    proj(D_ATTN + 2 * D_KV, 2 * D_CONV, uc_ref)


def _in_proj(x2, g1, scale1, shift1, w_in_bf, b_in):
    t = x2.shape[0]
    vec = lambda n: pl.BlockSpec((1, n), lambda i: (0, 0))
    return pl.pallas_call(
        _in_kernel,
        grid=(t // IN_TM,),
        in_specs=[
            pl.BlockSpec((IN_TM, D_MODEL), lambda i: (i, 0)),
            vec(D_MODEL), vec(D_MODEL), vec(D_MODEL),
            pl.BlockSpec((D_MODEL, D_IN), lambda i: (0, 0)),
            vec(D_IN),
        ],
        out_specs=[
            pl.BlockSpec((IN_TM, D_ATTN), lambda i: (i, 0)),
            pl.BlockSpec((IN_TM, 2 * D_KV), lambda i: (i, 0)),
            pl.BlockSpec((IN_TM, 2 * D_CONV), lambda i: (i, 0)),
        ],
        out_shape=[
            jax.ShapeDtypeStruct((t, D_ATTN), F32),
            jax.ShapeDtypeStruct((t, 2 * D_KV), F32),
            jax.ShapeDtypeStruct((t, 2 * D_CONV), F32),
        ],
        compiler_params=_params(("parallel",)),
        name="in_proj",
    )(x2, g1, scale1, shift1, w_in_bf, b_in)


ATT_R = 4


def _split_dot(a, b_bf):
    hi = a.astype(BF16)
    lo = (a - hi.astype(F32)).astype(BF16)
    return (jnp.dot(hi, b_bf, preferred_element_type=F32)
            + jnp.dot(lo, b_bf, preferred_element_type=F32))


def _attn_kernel(sinks_ref, q_ref, kvp_ref, kvc_ref, bias_ref, gq_ref, gk_ref, go_ref,
                 hq_ref, hk_ref, o_ref, y_ref):
    step = pl.program_id(0)
    lane = lax.broadcasted_iota(jnp.int32, (2 * BLOCK, LANES), 1)
    low = lane < HEAD_DIM
    col = lax.broadcasted_iota(jnp.int32, (BLOCK, 2 * BLOCK), 1)
    first_lo = jnp.where(step == 0, BLOCK, 0)

    for r in range(ATT_R):
        rows = slice(r * BLOCK, (r + 1) * BLOCK)
        q = q_ref[rows, :]
        ssq = _split_dot(q * q, hq_ref[...])
        qn = (q * lax.rsqrt(ssq * (1.0 / HEAD_DIM) + EPS) * gq_ref[...]).astype(BF16)

        if r == 0:
            kv_prev = kvp_ref[...]
        else:
            kv_prev = kvc_ref[(r - 1) * BLOCK:r * BLOCK, :]
        kv = jnp.concatenate([kv_prev, kvc_ref[rows, :]], axis=0)
        k = kv[:, :LANES]
        v = kv[:, LANES:]
        kss = _split_dot(k * k, hk_ref[...])
        kn = k * lax.rsqrt(kss * (1.0 / HEAD_DIM) + EPS) * gk_ref[...]
        kn_sw = pltpu.roll(kn, HEAD_DIM, axis=1)
        v_sw = pltpu.roll(v, HEAD_DIM, axis=1)
        zero = jnp.zeros_like(kn)
        k_lo = [jnp.where(low, kn, zero).astype(BF16), jnp.where(low, kn_sw, zero).astype(BF16)]
        k_hi = [jnp.where(low, zero, kn_sw).astype(BF16), jnp.where(low, zero, kn).astype(BF16)]
        v_lo = [jnp.where(low, v, zero).astype(BF16), jnp.where(low, v_sw, zero).astype(BF16)]
        v_hi = [jnp.where(low, zero, v_sw).astype(BF16), jnp.where(low, zero, v).astype(BF16)]

        for p in range(N_Q_HEADS // 2):
            g = (2 * p) // (N_Q_HEADS // N_KV_HEADS)
            qp = qn[:, p * LANES:(p + 1) * LANES]
            acc = None
            for half, (kz, vz) in enumerate(((k_lo[g], v_lo[g]), (k_hi[g], v_hi[g]))):
                h = 2 * p + half
                s = lax.dot_general(qp, kz, (((1,), (1,)), ((), ())), preferred_element_type=F32)
                s = s + bias_ref[h]
                if r == 0:
                    s = jnp.where(col >= first_lo, s, NEG_INF)
                sink = sinks_ref[h]
                m = jnp.maximum(jnp.max(s, axis=-1, keepdims=True), sink)
                e = jnp.exp(s - m)
                denom = jnp.sum(e, axis=-1, keepdims=True) + jnp.exp(sink - m)
                pv = jnp.dot(e.astype(BF16), vz, preferred_element_type=F32)
                pv = pv * (1.0 / denom)
                acc = pv if acc is None else acc + pv
            y_ref[:, p * LANES:(p + 1) * LANES] = acc

        y = y_ref[...]
        ms = jnp.mean(y * y, axis=-1, keepdims=True)
        o_ref[rows, :] = (y * lax.rsqrt(ms + EPS) * go_ref[...]).astype(BF16)


def _attention(q, kv, bias, sinks, gq_t, gk_t, g_out, hq, hk):
    t = q.shape[0]
    tile = ATT_R * BLOCK
    grid_spec = pltpu.PrefetchScalarGridSpec(
        num_scalar_prefetch=0,
        grid=(t // tile,),
        in_specs=[
            pl.BlockSpec(memory_space=pltpu.SMEM),
            pl.BlockSpec((tile, D_ATTN), lambda i: (i, 0)),
            pl.BlockSpec((BLOCK, 2 * D_KV), lambda i: (jnp.maximum(i * ATT_R - 1, 0), 0)),
            pl.BlockSpec((tile, 2 * D_KV), lambda i: (i, 0)),
            pl.BlockSpec((N_Q_HEADS, BLOCK, 2 * BLOCK), lambda i: (0, 0, 0)),
            pl.BlockSpec((1, D_ATTN), lambda i: (0, 0)),
            pl.BlockSpec((1, LANES), lambda i: (0, 0)),
            pl.BlockSpec((1, D_ATTN), lambda i: (0, 0)),
            pl.BlockSpec((D_ATTN, D_ATTN), lambda i: (0, 0)),
            pl.BlockSpec((LANES, LANES), lambda i: (0, 0)),
        ],
        out_specs=pl.BlockSpec((tile, D_ATTN), lambda i: (i, 0)),
        scratch_shapes=[pltpu.VMEM((BLOCK, D_ATTN), F32)],
    )
    return pl.pallas_call(
        _attn_kernel,
        grid_spec=grid_spec,
        out_shape=jax.ShapeDtypeStruct((t, D_ATTN), BF16),
        compiler_params=_params(("parallel",)),
        name="attn",
    )(sinks, q, kv, kv, bias, gq_t, gk_t, g_out, hq, hk)


def _t5_bucket(dist):
    max_exact = N_BUCKETS // 2
    d = jnp.maximum(dist, 0)
    log_ratio = jnp.log(jnp.maximum(d, max_exact).astype(F32) / max_exact)
    large = max_exact + (log_ratio / math.log(MAX_DISTANCE / max_exact)
                         * (N_BUCKETS - max_exact)).astype(jnp.int32)
    large = jnp.minimum(large, N_BUCKETS - 1)
    return jnp.where(d < max_exact, d, large)


def _bias_table(rel_bias):
    q_local = jnp.arange(BLOCK, dtype=jnp.int32) + BLOCK
    k_local = jnp.arange(2 * BLOCK, dtype=jnp.int32)
    dist = q_local[:, None] - k_local[None, :]
    band = (dist >= 0) & (dist < WINDOW)
    bias = rel_bias.astype(F32)[_t5_bucket(dist)]
    bias = jnp.transpose(bias, (2, 0, 1))
    return jnp.where(band[None], bias, NEG_INF)


def _head_indicator(n):
    i = jnp.arange(n) // HEAD_DIM
    return (i[:, None] == i[None, :]).astype(BF16)


CONV_TT = 256
CONV_RC = 64
CONV_CC = 256


def _conv_kernel(u_ref, halo_ref, w_ref, b_ref, lg_ref, lb_ref, go_ref, o_ref, h_ref, y_ref):
    step = pl.program_id(0)

    def glu(u):
        return u[:, :D_CONV] * jax.nn.sigmoid(u[:, D_CONV:])

    hh = glu(halo_ref[...])
    h_ref[0:HALO, :] = jnp.where(step == 0, jnp.zeros_like(hh), hh)
    h_ref[HALO:, :] = glu(u_ref[...])

    off = HALO - (CONV_WIDTH - 1)
    for r0 in range(0, CONV_TT, CONV_RC):
        for c0 in range(0, D_CONV, CONV_CC):
            cs = slice(c0, c0 + CONV_CC)
            acc = jnp.broadcast_to(b_ref[:, cs], (CONV_RC, CONV_CC))
            for j in range(CONV_WIDTH):
                acc = acc + w_ref[j:j + 1, cs] * h_ref[r0 + off + j:r0 + off + j + CONV_RC, cs]
            y_ref[r0:r0 + CONV_RC, cs] = acc

    y = y_ref[...]
    mu = jnp.mean(y, axis=-1, keepdims=True)
    yc = y - mu
    var = jnp.mean(yc * yc, axis=-1, keepdims=True)
    z = yc * lax.rsqrt(var + EPS) * lg_ref[...] + lb_ref[...]
    s = z * jax.nn.sigmoid(z)
    ms = jnp.mean(s * s, axis=-1, keepdims=True)
    o_ref[...] = (s * lax.rsqrt(ms + EPS) * go_ref[...]).astype(BF16)


def _conv(uc, w_dw, b_dw, ln_g, ln_b, g_out):
    t = uc.shape[0]
    vec = lambda: pl.BlockSpec((1, D_CONV), lambda i: (0, 0))
    per = CONV_TT // HALO
    return pl.pallas_call(
        _conv_kernel,
        grid=(t // CONV_TT,),
        in_specs=[
            pl.BlockSpec((CONV_TT, 2 * D_CONV), lambda i: (i, 0)),
            pl.BlockSpec((HALO, 2 * D_CONV), lambda i: (jnp.maximum(i * per - 1, 0), 0)),
            pl.BlockSpec((HALO, D_CONV), lambda i: (0, 0)),
            vec(), vec(), vec(), vec(),
        ],
        out_specs=pl.BlockSpec((CONV_TT, D_CONV), lambda i: (i, 0)),
        out_shape=jax.ShapeDtypeStruct((t, D_CONV), BF16),
        scratch_shapes=[pltpu.VMEM((CONV_TT + HALO, D_CONV), F32),
                        pltpu.VMEM((CONV_TT, D_CONV), F32)],
        compiler_params=_params(("parallel",)),
        name="conv",
    )(uc, uc, w_dw, b_dw, ln_g, ln_b, g_out)


OUT_TM = 256


def _out_kernel(ma_ref, mc_ref, x_ref, w_ref, bo_ref, g1_ref, g2_ref, sc_ref, sh_ref, wr_ref, br_ref,
                tri_ref, xm_ref, h2_ref, idx_ref, rank_ref, wt_ref, cnt_ref, carry_ref):
    step = pl.program_id(0)

    @pl.when(step == 0)
    def _():
        carry_ref[...] = jnp.zeros_like(carry_ref)

    y = (jnp.dot(ma_ref[...], w_ref[0:D_ATTN, :], preferred_element_type=F32)
         + jnp.dot(mc_ref[...], w_ref[D_ATTN:, :], preferred_element_type=F32) + bo_ref[...])
    xm = x_ref[...] + g1_ref[...] * y
    xm_ref[...] = xm
    h2 = _modulated_rms(xm, g2_ref[...], sc_ref[...], sh_ref[...])
    h2_ref[...] = h2

    logits = jnp.dot(h2, wr_ref[...], preferred_element_type=F32,
                     precision=lax.Precision.HIGHEST) + br_ref[...]
    tm = logits.shape[0]
    lane = lax.broadcasted_iota(jnp.int32, (tm, N_EXPERTS), 1).astype(F32)
    vals, idxs = [], []
    l = logits
    for _ in range(TOP_K):
        m = jnp.max(l, axis=-1, keepdims=True)
        i = jnp.min(jnp.where(l == m, lane, float(N_EXPERTS)), axis=-1, keepdims=True)
        vals.append(m)
        idxs.append(i)
        l = jnp.where(lane == i, -jnp.inf, l)
    es = [jnp.exp(v - vals[0]) for v in vals]
    tot = es[0] + es[1] + es[2] + es[3]
    ws = [e / tot for e in es]

    hot = [(lane == i).astype(F32) for i in idxs]
    hot_all = hot[0] + hot[1] + hot[2] + hot[3]
    before = jnp.dot(tri_ref[...], hot_all.astype(BF16), preferred_element_type=F32) + carry_ref[...]
    ranks = [jnp.sum(h * before, axis=-1, keepdims=True) for h in hot]
    carry_ref[...] = carry_ref[...] + jnp.sum(hot_all, axis=0, keepdims=True)
    cnt_ref[...] = carry_ref[...].astype(jnp.int32)

    slot = lax.broadcasted_iota(jnp.int32, (tm, LANES), 1)

    def pack(cols):
        out = jnp.zeros((tm, LANES), F32)
        for k in range(TOP_K):
            out = jnp.where(slot == k, cols[k], out)
        return out

    idx_ref[...] = pack(idxs).astype(jnp.int32)
    rank_ref[...] = pack(ranks).astype(jnp.int32)
    wt_ref[...] = pack(ws)


def _out_proj(ma, mc, x2, w_out_bf, b_out, gate1, g2, scale2, shift2, w_router, b_router):
    t = x2.shape[0]
    vec = lambda n: pl.BlockSpec((1, n), lambda i: (0, 0))
    tri = jnp.tril(jnp.ones((OUT_TM, OUT_TM), F32), -1).astype(BF16)
    return pl.pallas_call(
        _out_kernel,
        grid=(t // OUT_TM,),
        in_specs=[
            pl.BlockSpec((OUT_TM, D_ATTN), lambda i: (i, 0)),
            pl.BlockSpec((OUT_TM, D_CONV), lambda i: (i, 0)),
            pl.BlockSpec((OUT_TM, D_MODEL), lambda i: (i, 0)),
            pl.BlockSpec((D_MODEL, D_MODEL), lambda i: (0, 0)),
            vec(D_MODEL), vec(D_MODEL), vec(D_MODEL), vec(D_MODEL), vec(D_MODEL),
            pl.BlockSpec((D_MODEL, N_EXPERTS), lambda i: (0, 0)),
            vec(N_EXPERTS),
            pl.BlockSpec((OUT_TM, OUT_TM), lambda i: (0, 0)),
        ],
        out_specs=[
            pl.BlockSpec((OUT_TM, D_MODEL), lambda i: (i, 0)),
            pl.BlockSpec((OUT_TM, D_MODEL), lambda i: (i, 0)),
            pl.BlockSpec((OUT_TM, LANES), lambda i: (i, 0)),
            pl.BlockSpec((OUT_TM, LANES), lambda i: (i, 0)),
            pl.BlockSpec((OUT_TM, LANES), lambda i: (i, 0)),
            pl.BlockSpec((1, N_EXPERTS), lambda i: (0, 0)),
        ],
        out_shape=[
            jax.ShapeDtypeStruct((t, D_MODEL), F32),
            jax.ShapeDtypeStruct((t, D_MODEL), F32),
            jax.ShapeDtypeStruct((t, LANES), jnp.int32),
            jax.ShapeDtypeStruct((t, LANES), jnp.int32),
            jax.ShapeDtypeStruct((t, LANES), F32),
            jax.ShapeDtypeStruct((1, N_EXPERTS), jnp.int32),
        ],
        scratch_shapes=[pltpu.VMEM((1, N_EXPERTS), F32)],
        compiler_params=_params(("arbitrary",)),
        name="out_proj",
    )(ma, mc, x2, w_out_bf, b_out, gate1, g2, scale2, shift2, w_router, b_router, tri)


DISP_TD = 256


def _dispatch_kernel(dest_ref, pad_lo_ref, pad_hi_ref, nu_ref, h2_ref, xs_ref, zero_ref, sem, zsem):
    step = pl.program_id(0)
    base = step * (DISP_TD * TOP_K)

    def row_copy(r, d):
        return pltpu.make_async_copy(h2_ref.at[pl.ds(r, 1), :], xs_ref.at[pl.ds(d, 1), :], sem)

    def issue(r, carry):
        for k in range(TOP_K):
            row_copy(r, dest_ref[base + r * TOP_K + k]).start()
        return carry

    lax.fori_loop(0, DISP_TD, issue, 0)

    @pl.when(step == 0)
    def _():
        zero_ref[...] = jnp.zeros_like(zero_ref)

        def zero_row(d):
            return pltpu.make_async_copy(zero_ref.at[pl.ds(0, 1), :], xs_ref.at[pl.ds(d, 1), :], zsem)

        def zero_block(b):
            d = pl.multiple_of(b * EXPERT_TM, EXPERT_TM)
            return pltpu.make_async_copy(zero_ref, xs_ref.at[pl.ds(d, EXPERT_TM), :], zsem)

        def start_row(d, carry):
            zero_row(d).start()
            return carry

        def wait_row(d, carry):
            zero_row(d).wait()
            return carry

        def start_block(b, carry):
            zero_block(b).start()
            return carry

        def wait_block(b, carry):
            zero_block(b).wait()
            return carry

        def per_expert(e, carry):
            lax.fori_loop(pad_lo_ref[e], pad_hi_ref[e], start_row, 0)
            lax.fori_loop(pad_lo_ref[e], pad_hi_ref[e], wait_row, 0)
            return carry

        lax.fori_loop(0, N_EXPERTS, per_expert, 0)
        n_blocks = xs_ref.shape[0] // EXPERT_TM
        lax.fori_loop(nu_ref[0], n_blocks, start_block, 0)
        lax.fori_loop(nu_ref[0], n_blocks, wait_block, 0)

    def drain(r, carry):
        for k in range(TOP_K):
            row_copy(0, 0).wait()
        return carry

    lax.fori_loop(0, DISP_TD, drain, 0)


def _dispatch(h2, dest_flat, pad_lo, pad_hi, n_used, n_rows):
    t = h2.shape[0]
    grid_spec = pltpu.PrefetchScalarGridSpec(
        num_scalar_prefetch=4,
        grid=(t // DISP_TD,),
        in_specs=[pl.BlockSpec((DISP_TD, D_MODEL), lambda i, *_: (i, 0))],
        out_specs=pl.BlockSpec(memory_space=pl.ANY),
        scratch_shapes=[pltpu.VMEM((EXPERT_TM, D_MODEL), F32),
                        pltpu.SemaphoreType.DMA(()), pltpu.SemaphoreType.DMA(())],
    )
    return pl.pallas_call(
        _dispatch_kernel,
        grid_spec=grid_spec,
        out_shape=jax.ShapeDtypeStruct((n_rows, D_MODEL), F32),
        compiler_params=_params(("arbitrary",)),
        name="dispatch",
    )(dest_flat, pad_lo, pad_hi, n_used, h2)


UP_TF = 512
DOWN_TN = 512


def _new_expert(be_ref, b):
    prev = be_ref[jnp.maximum(b - 1, 0)]
    return jnp.logical_or(b == 0, be_ref[b] != prev)


def _up_kernel(be_ref, nu_ref, x_ref, wg_ref, wu_ref, bg_ref, bu_ref, h_ref, wg_bf, wu_bf):
    b = pl.program_id(1)

    @pl.when(_new_expert(be_ref, b))
    def _():
        wg_bf[...] = wg_ref[0].astype(BF16)
        wu_bf[...] = wu_ref[0].astype(BF16)

    @pl.when(b < nu_ref[0])
    def _():
        x = x_ref[...].astype(BF16)
        g = jnp.dot(x, wg_bf[...], preferred_element_type=F32) + bg_ref[0]
        lin = jnp.dot(x, wu_bf[...], preferred_element_type=F32) + bu_ref[0]
        g = jnp.minimum(g, SWIGLU_LIMIT)
        lin = jnp.clip(lin, -SWIGLU_LIMIT, SWIGLU_LIMIT)
        act = g * jax.nn.sigmoid(SWIGLU_ALPHA * g) * (lin + 1.0)
        h_ref[...] = act.astype(BF16)

    @pl.when(b >= nu_ref[0])
    def _():
        h_ref[...] = jnp.zeros_like(h_ref)


def _up(xs, block_expert, n_used, w_gate, b_gate, w_up, b_up):
    n_rows = xs.shape[0]
    nb = n_rows // EXPERT_TM
    row = lambda f, b, be, nu: (jnp.minimum(b, nu[0] - 1), 0)
    wsel = lambda f, b, be, nu: (be[b], 0, f)
    grid_spec = pltpu.PrefetchScalarGridSpec(
        num_scalar_prefetch=2,
        grid=(D_FF // UP_TF, nb),
        in_specs=[
            pl.BlockSpec((EXPERT_TM, D_MODEL), row),
            pl.BlockSpec((1, D_MODEL, UP_TF), wsel),
            pl.BlockSpec((1, D_MODEL, UP_TF), wsel),
            pl.BlockSpec((1, 1, UP_TF), wsel),
            pl.BlockSpec((1, 1, UP_TF), wsel),
        ],
        out_specs=pl.BlockSpec((EXPERT_TM, UP_TF), lambda f, b, be, nu: (b, f)),
        scratch_shapes=[pltpu.VMEM((D_MODEL, UP_TF), BF16), pltpu.VMEM((D_MODEL, UP_TF), BF16)],
    )
    return pl.pallas_call(
        _up_kernel,
        grid_spec=grid_spec,
        out_shape=jax.ShapeDtypeStruct((n_rows, D_FF), BF16),
        compiler_params=_params(("arbitrary", "arbitrary")),
        name="expert_up",
    )(block_expert, n_used, xs, w_gate, w_up,
      b_gate.reshape(N_EXPERTS, 1, D_FF), b_up.reshape(N_EXPERTS, 1, D_FF))


def _down_kernel(be_ref, nu_ref, h_ref, wd_ref, bd_ref, y_ref, wd_bf):
    b = pl.program_id(1)

    @pl.when(_new_expert(be_ref, b))
    def _():
        wd_bf[...] = wd_ref[0].astype(BF16)

    @pl.when(b < nu_ref[0])
    def _():
        y_ref[...] = jnp.dot(h_ref[...], wd_bf[...], preferred_element_type=F32) + bd_ref[0]

    @pl.when(b >= nu_ref[0])
    def _():
        y_ref[...] = jnp.zeros_like(y_ref)


def _down(hs, block_expert, n_used, w_down, b_down):
    n_rows = hs.shape[0]
    nb = n_rows // EXPERT_TM
    row = lambda n, b, be, nu: (jnp.minimum(b, nu[0] - 1), 0)
    wsel = lambda n, b, be, nu: (be[b], 0, n)
    grid_spec = pltpu.PrefetchScalarGridSpec(
        num_scalar_prefetch=2,
        grid=(D_MODEL // DOWN_TN, nb),
        in_specs=[
            pl.BlockSpec((EXPERT_TM, D_FF), row),
            pl.BlockSpec((1, D_FF, DOWN_TN), wsel),
            pl.BlockSpec((1, 1, DOWN_TN), wsel),
        ],
        out_specs=pl.BlockSpec((EXPERT_TM, DOWN_TN), lambda n, b, be, nu: (b, n)),
        scratch_shapes=[pltpu.VMEM((D_FF, DOWN_TN), BF16)],
    )
    return pl.pallas_call(
        _down_kernel,
        grid_spec=grid_spec,
        out_shape=jax.ShapeDtypeStruct((n_rows, D_MODEL), F32),
        compiler_params=_params(("arbitrary", "arbitrary")),
        name="expert_down",
    )(block_expert, n_used, hs, w_down, b_down.reshape(N_EXPERTS, 1, D_MODEL))


COMB_TC = 128


def _combine_kernel(dest_ref, xm_ref, wt_ref, g2_ref, ys_ref, o_ref, buf_ref, sem):
    step = pl.program_id(0)
    base = step * (COMB_TC * TOP_K)

    def row_copy(r, k, d):
        return pltpu.make_async_copy(ys_ref.at[pl.ds(d, 1), :], buf_ref.at[k, pl.ds(r, 1), :], sem)

    def issue(r, carry):
        for k in range(TOP_K):
            row_copy(r, k, dest_ref[base + r * TOP_K + k]).start()
        return carry

    lax.fori_loop(0, COMB_TC, issue, 0)

    def drain(r, carry):
        for k in range(TOP_K):
            row_copy(0, k, 0).wait()
        return carry

    lax.fori_loop(0, COMB_TC, drain, 0)

    wt = wt_ref[...]
    acc = wt[:, 0:1] * buf_ref[0]
    for k in range(1, TOP_K):
        acc = acc + wt[:, k:k + 1] * buf_ref[k]
    o_ref[...] = xm_ref[...] + g2_ref[...] * acc


def _combine(ys, dest_flat, xm, wts, gate2):
    t = xm.shape[0]
    grid_spec = pltpu.PrefetchScalarGridSpec(
        num_scalar_prefetch=1,
        grid=(t // COMB_TC,),
        in_specs=[
            pl.BlockSpec((COMB_TC, D_MODEL), lambda i, d: (i, 0)),
            pl.BlockSpec((COMB_TC, LANES), lambda i, d: (i, 0)),
            pl.BlockSpec((1, D_MODEL), lambda i, d: (0, 0)),
            pl.BlockSpec(memory_space=pl.ANY),
        ],
        out_specs=pl.BlockSpec((COMB_TC, D_MODEL), lambda i, d: (i, 0)),
        scratch_shapes=[pltpu.VMEM((TOP_K, COMB_TC, D_MODEL), F32), pltpu.SemaphoreType.DMA(())],
    )
    return pl.pallas_call(
        _combine_kernel,
        grid_spec=grid_spec,
        out_shape=jax.ShapeDtypeStruct((t, D_MODEL), F32),
        compiler_params=_params(("arbitrary",)),
        name="combine",
    )(dest_flat, xm, wts, gate2, ys)


def _routing_tables(idx, rank, counts, n_tok):
    padded = ((counts + EXPERT_TM - 1) // EXPERT_TM) * EXPERT_TM
    pend = jnp.cumsum(padded)
    pstart = pend - padded
    dest = (pstart[idx] + rank).astype(jnp.int32).reshape(-1)
    n_rows = n_tok * TOP_K + N_EXPERTS * EXPERT_TM
    nb = n_rows // EXPERT_TM
    block_expert = jnp.minimum(
        jnp.searchsorted(pend, jnp.arange(nb, dtype=jnp.int32) * EXPERT_TM, side='right'),
        N_EXPERTS - 1).astype(jnp.int32)
    n_used = (pend[-1:] // EXPERT_TM).astype(jnp.int32)
    pad_lo = (pstart + counts).astype(jnp.int32)
    pad_hi = pend.astype(jnp.int32)
    return dest, block_expert, n_used, pad_lo, pad_hi, n_rows


def _layer(x2, c, w_ada, b_ada, g_norm1, w_in, b_in, g_q, g_k, sinks, rel_bias, w_dw, b_dw, ln_g, ln_b,
           g_out_attn, g_out_conv, w_out, b_out, g_norm2, w_router, b_router,
           w_gate, b_gate, w_up, b_up, w_down, b_down):
    t = x2.shape[0]
    row = lambda v: v.reshape(1, -1)
    mod = _ada(c, w_ada, b_ada)
    shift1, scale1, gate1, shift2, scale2, gate2 = [mod[:, i * D_MODEL:(i + 1) * D_MODEL] for i in range(6)]

    q, kv, uc = _in_proj(x2, row(g_norm1), scale1, shift1, w_in.astype(BF16), row(b_in))

    gq_t = row(jnp.tile(g_q, N_Q_HEADS)) * (HEAD_DIM ** -0.5)
    gk_t = row(jnp.tile(g_k, N_KV_HEADS))
    ma = _attention(q, kv, _bias_table(rel_bias), sinks, gq_t, gk_t, row(g_out_attn),
                    _head_indicator(D_ATTN), _head_indicator(LANES))
    w_dw_p = jnp.concatenate([w_dw, jnp.zeros((HALO - CONV_WIDTH, D_CONV), w_dw.dtype)], axis=0)
    mc = _conv(uc, w_dw_p, row(b_dw), row(ln_g), row(ln_b), row(g_out_conv))

    xm, h2, idx, rank, wts, counts = _out_proj(
        ma, mc, x2, w_out.astype(BF16), row(b_out), gate1, row(g_norm2), scale2, shift2,
        w_router, row(b_router))

    dest, block_expert, n_used, pad_lo, pad_hi, n_rows = _routing_tables(
        idx[:, :TOP_K], rank[:, :TOP_K], counts[0], t)
    xs = _dispatch(h2, dest, pad_lo, pad_hi, n_used, n_rows)
    hs = _up(xs, block_expert, n_used, w_gate, b_gate, w_up, b_up)
    ys = _down(hs, block_expert, n_used, w_down, b_down)
    return _combine(ys, dest, xm, wts, gate2)


def kernel(x, c, w_ada, b_ada, g_norm1, w_in, b_in, g_q, g_k, sinks, rel_bias, w_dw, b_dw, ln_g, ln_b,
           g_out_attn, g_out_conv, w_out, b_out, g_norm2, w_router, b_router,
           w_gate, b_gate, w_up, b_up, w_down, b_down):
    b, t, d = x.shape
    assert b == 1 and d == D_MODEL and w_ada.shape[0] == 1
    out = _layer(x.reshape(t, d), c, w_ada[0], b_ada[0], g_norm1[0], w_in[0], b_in[0], g_q[0], g_k[0],
                 sinks[0], rel_bias, w_dw[0], b_dw[0], ln_g[0], ln_b[0], g_out_attn[0], g_out_conv[0],
                 w_out[0], b_out[0], g_norm2[0], w_router[0], b_router[0],
                 w_gate[0], b_gate[0], w_up[0], b_up[0], w_down[0], b_down[0])
    return out.reshape(b, t, d)
```

```python
import functools
import math

import jax
import jax.numpy as jnp
from jax import lax
from jax.experimental import pallas as pl
from jax.experimental.pallas import tpu as pltpu

D_MODEL = 2048
HEAD_DIM = 64
N_Q_HEADS = 16
N_KV_HEADS = 2
D_ATTN = N_Q_HEADS * HEAD_DIM
D_KV = N_KV_HEADS * HEAD_DIM
D_CONV = D_MODEL - D_ATTN
D_IN = D_ATTN + 2 * D_KV + 2 * D_CONV
WINDOW = 128
BLOCK = 128
CONV_WIDTH = 31
N_BUCKETS = 32
MAX_DISTANCE = 128
N_EXPERTS = 32
TOP_K = 4
D_FF = D_MODEL
SWIGLU_LIMIT = 7.0
SWIGLU_ALPHA = 1.702
EPS = 1e-6
NEG_INF = -1e30

LANES = 128
SUBLANES = 8
VMEM_LIMIT = 56 * 1024 * 1024

HALO = 32
EXPERT_TM = 256
F32 = jnp.float32
BF16 = jnp.bfloat16


def _params(sem):
    return pltpu.CompilerParams(dimension_semantics=sem, vmem_limit_bytes=VMEM_LIMIT)


ADA_TN = 512
ADA_RC = 256


def _ada_kernel(c_ref, w_ref, b_ref, o_ref):
    tn = w_ref.shape[1]

    def body(i, acc):
        r = pl.multiple_of(i * ADA_RC, ADA_RC)
        c = c_ref[pl.ds(r, ADA_RC), :]
        sc = c * jax.nn.sigmoid(c)
        prod = w_ref[pl.ds(r, ADA_RC), :] * sc
        return acc + jnp.sum(prod.reshape(ADA_RC // 8, 8, tn), axis=0)

    acc = lax.fori_loop(0, D_MODEL // ADA_RC, body, jnp.zeros((8, tn), F32))
    o_ref[...] = jnp.sum(acc, axis=0, keepdims=True) + b_ref[...]


def _ada(c, w_ada, b_ada):
    n = w_ada.shape[1]
    return pl.pallas_call(
        _ada_kernel,
        grid=(n // ADA_TN,),
        in_specs=[
            pl.BlockSpec((D_MODEL, 1), lambda j: (0, 0)),
            pl.BlockSpec((D_MODEL, ADA_TN), lambda j: (0, j)),
            pl.BlockSpec((1, ADA_TN), lambda j: (0, j)),
        ],
        out_specs=pl.BlockSpec((1, ADA_TN), lambda j: (0, j)),
        out_shape=jax.ShapeDtypeStruct((1, n), F32),
        compiler_params=_params(("parallel",)),
        name="ada",
    )(c.reshape(D_MODEL, 1), w_ada, b_ada.reshape(1, n))


IN_TM = 512
IN_NC = 256


def _modulated_rms(x, g, scale, shift):
    ms = jnp.mean(x * x, axis=-1, keepdims=True)
    return (x * lax.rsqrt(ms + EPS) * g) * (1.0 + scale) + shift


def _in_kernel(x_ref, g_ref, sc_ref, sh_ref, w_ref, b_ref, q_ref, kv_ref, uc_ref):
    h = _modulated_rms(x_ref[...], g_ref[...], sc_ref[...], sh_ref[...]).astype(BF16)

    def proj(lo, n, o_ref):
        for c in range(0, n, IN_NC):
            u = jnp.dot(h, w_ref[:, lo + c:lo + c + IN_NC], preferred_element_type=F32)
            o_ref[:, c:c + IN_NC] = u + b_ref[:, lo + c:lo + c + IN_NC]

    proj(0, D_ATTN, q_ref)
    proj(D_ATTN, 2 * D_KV, kv_ref)
    proj(D_ATTN + 2 * D_KV, 2 * D_CONV, uc_ref)


def _in_proj(x2, g1, scale1, shift1, w_in_bf, b_in):
    t = x2.shape[0]
    vec = lambda n: pl.BlockSpec((1, n), lambda i: (0, 0))
    return pl.pallas_call(
        _in_kernel,
        grid=(t // IN_TM,),
        in_specs=[
            pl.BlockSpec((IN_TM, D_MODEL), lambda i: (i, 0)),
            vec(D_MODEL), vec(D_MODEL), vec(D_MODEL),
            pl.BlockSpec((D_MODEL, D_IN), lambda i: (0, 0)),
            vec(D_IN),
        ],
        out_specs=[
            pl.BlockSpec((IN_TM, D_ATTN), lambda i: (i, 0)),
            pl.BlockSpec((IN_TM, 2 * D_KV), lambda i: (i, 0)),
            pl.BlockSpec((IN_TM, 2 * D_CONV), lambda i: (i, 0)),
        ],
        out_shape=[
            jax.ShapeDtypeStruct((t, D_ATTN), F32),
            jax.ShapeDtypeStruct((t, 2 * D_KV), F32),
            jax.ShapeDtypeStruct((t, 2 * D_CONV), F32),
        ],
        compiler_params=_params(("parallel",)),
        name="in_proj",
    )(x2, g1, scale1, shift1, w_in_bf, b_in)


ATT_R = 4


def _split_dot(a, b_bf):
    hi = a.astype(BF16)
    lo = (a - hi.astype(F32)).astype(BF16)
    return (jnp.dot(hi, b_bf, preferred_element_type=F32)
            + jnp.dot(lo, b_bf, preferred_element_type=F32))


def _attn_kernel(sinks_ref, q_ref, kvp_ref, kvc_ref, bias_ref, gq_ref, gk_ref, go_ref,
                 hq_ref, hk_ref, o_ref, y_ref):
    step = pl.program_id(0)
    lane = lax.broadcasted_iota(jnp.int32, (2 * BLOCK, LANES), 1)
    low = lane < HEAD_DIM
    col = lax.broadcasted_iota(jnp.int32, (BLOCK, 2 * BLOCK), 1)
    first_lo = jnp.where(step == 0, BLOCK, 0)

    for r in range(ATT_R):
        rows = slice(r * BLOCK, (r + 1) * BLOCK)
        q = q_ref[rows, :]
        ssq = _split_dot(q * q, hq_ref[...])
        qn = (q * lax.rsqrt(ssq * (1.0 / HEAD_DIM) + EPS) * gq_ref[...]).astype(BF16)

        if r == 0:
            kv_prev = kvp_ref[...]
        else:
            kv_prev = kvc_ref[(r - 1) * BLOCK:r * BLOCK, :]
        kv = jnp.concatenate([kv_prev, kvc_ref[rows, :]], axis=0)
        k = kv[:, :LANES]
        v = kv[:, LANES:]
        kss = _split_dot(k * k, hk_ref[...])
        kn = k * lax.rsqrt(kss * (1.0 / HEAD_DIM) + EPS) * gk_ref[...]
        kn_sw = pltpu.roll(kn, HEAD_DIM, axis=1)
        v_sw = pltpu.roll(v, HEAD_DIM, axis=1)
        zero = jnp.zeros_like(kn)
        k_lo = [jnp.where(low, kn, zero).astype(BF16), jnp.where(low, kn_sw, zero).astype(BF16)]
        k_hi = [jnp.where(low, zero, kn_sw).astype(BF16), jnp.where(low, zero, kn).astype(BF16)]
        v_lo = [jnp.where(low, v, zero).astype(BF16), jnp.where(low, v_sw, zero).astype(BF16)]
        v_hi = [jnp.where(low, zero, v_sw).astype(BF16), jnp.where(low, zero, v).astype(BF16)]

        for p in range(N_Q_HEADS // 2):
            g = (2 * p) // (N_Q_HEADS // N_KV_HEADS)
            qp = qn[:, p * LANES:(p + 1) * LANES]
            acc = None
            for half, (kz, vz) in enumerate(((k_lo[g], v_lo[g]), (k_hi[g], v_hi[g]))):
                h = 2 * p + half
                s = lax.dot_general(qp, kz, (((1,), (1,)), ((), ())), preferred_element_type=F32)
                s = s + bias_ref[h]
                if r == 0:
                    s = jnp.where(col >= first_lo, s, NEG_INF)
                sink = sinks_ref[h]
                m = jnp.maximum(jnp.max(s, axis=-1, keepdims=True), sink)
                e = jnp.exp(s - m)
                denom = jnp.sum(e, axis=-1, keepdims=True) + jnp.exp(sink - m)
                pv = jnp.dot(e.astype(BF16), vz, preferred_element_type=F32)
                pv = pv * (1.0 / denom)
                acc = pv if acc is None else acc + pv
            y_ref[:, p * LANES:(p + 1) * LANES] = acc

        y = y_ref[...]
        ms = jnp.mean(y * y, axis=-1, keepdims=True)
        o_ref[rows, :] = (y * lax.rsqrt(ms + EPS) * go_ref[...]).astype(BF16)


def _attention(q, kv, bias, sinks, gq_t, gk_t, g_out, hq, hk):
    t = q.shape[0]
    tile = ATT_R * BLOCK
    grid_spec = pltpu.PrefetchScalarGridSpec(
        num_scalar_prefetch=0,
        grid=(t // tile,),
        in_specs=[
            pl.BlockSpec(memory_space=pltpu.SMEM),
            pl.BlockSpec((tile, D_ATTN), lambda i: (i, 0)),
            pl.BlockSpec((BLOCK, 2 * D_KV), lambda i: (jnp.maximum(i * ATT_R - 1, 0), 0)),
            pl.BlockSpec((tile, 2 * D_KV), lambda i: (i, 0)),
            pl.BlockSpec((N_Q_HEADS, BLOCK, 2 * BLOCK), lambda i: (0, 0, 0)),
            pl.BlockSpec((1, D_ATTN), lambda i: (0, 0)),
            pl.BlockSpec((1, LANES), lambda i: (0, 0)),
            pl.BlockSpec((1, D_ATTN), lambda i: (0, 0)),
            pl.BlockSpec((D_ATTN, D_ATTN), lambda i: (0, 0)),
            pl.BlockSpec((LANES, LANES), lambda i: (0, 0)),
        ],
        out_specs=pl.BlockSpec((tile, D_ATTN), lambda i: (i, 0)),
        scratch_shapes=[pltpu.VMEM((BLOCK, D_ATTN), F32)],
    )
    return pl.pallas_call(
        _attn_kernel,
        grid_spec=grid_spec,
        out_shape=jax.ShapeDtypeStruct((t, D_ATTN), BF16),
        compiler_params=_params(("parallel",)),
        name="attn",
    )(sinks, q, kv, kv, bias, gq_t, gk_t, g_out, hq, hk)


def _t5_bucket(dist):
    max_exact = N_BUCKETS // 2
    d = jnp.maximum(dist, 0)
    log_ratio = jnp.log(jnp.maximum(d, max_exact).astype(F32) / max_exact)
    large = max_exact + (log_ratio / math.log(MAX_DISTANCE / max_exact)
                         * (N_BUCKETS - max_exact)).astype(jnp.int32)
    large = jnp.minimum(large, N_BUCKETS - 1)
    return jnp.where(d < max_exact, d, large)


def _bias_table(rel_bias):
    q_local = jnp.arange(BLOCK, dtype=jnp.int32) + BLOCK
    k_local = jnp.arange(2 * BLOCK, dtype=jnp.int32)
    dist = q_local[:, None] - k_local[None, :]
    band = (dist >= 0) & (dist < WINDOW)
    bias = rel_bias.astype(F32)[_t5_bucket(dist)]
    bias = jnp.transpose(bias, (2, 0, 1))
    return jnp.where(band[None], bias, NEG_INF)


def _head_indicator(n):
    i = jnp.arange(n) // HEAD_DIM
    return (i[:, None] == i[None, :]).astype(BF16)


CONV_TT = 256
CONV_RC = 64
CONV_CC = 256


def _conv_kernel(u_ref, halo_ref, w_ref, b_ref, lg_ref, lb_ref, go_ref, o_ref, h_ref, y_ref):
    step = pl.program_id(0)

    def glu(u):
        return u[:, :D_CONV] * jax.nn.sigmoid(u[:, D_CONV:])

    hh = glu(halo_ref[...])
    h_ref[0, 0:HALO, :] = jnp.where(step == 0, jnp.zeros_like(hh), hh)
    h_ref[0, HALO:, :] = glu(u_ref[...])

    n_shifted = CONV_TT + HALO - SUBLANES
    for s in range(1, SUBLANES):
        for c0 in range(0, D_CONV, CONV_CC):
            cs = slice(c0, c0 + CONV_CC)
            h_ref[s, 0:n_shifted, cs] = h_ref[0, s:s + n_shifted, cs]

    off = HALO - (CONV_WIDTH - 1)
    for r0 in range(0, CONV_TT, CONV_RC):
        for c0 in range(0, D_CONV, CONV_CC):
            cs = slice(c0, c0 + CONV_CC)
            acc = jnp.broadcast_to(b_ref[:, cs], (CONV_RC, CONV_CC))
            for j in range(CONV_WIDTH):
                s = (r0 + off + j) % SUBLANES
                a = r0 + off + j - s
                acc = acc + w_ref[j:j + 1, cs] * h_ref[s, a:a + CONV_RC, cs]
            y_ref[r0:r0 + CONV_RC, cs] = acc

    y = y_ref[...]
    mu = jnp.mean(y, axis=-1, keepdims=True)
    yc = y - mu
    var = jnp.mean(yc * yc, axis=-1, keepdims=True)
    z = yc * lax.rsqrt(var + EPS) * lg_ref[...] + lb_ref[...]
    s = z * jax.nn.sigmoid(z)
    ms = jnp.mean(s * s, axis=-1, keepdims=True)
    o_ref[...] = (s * lax.rsqrt(ms + EPS) * go_ref[...]).astype(BF16)


def _conv(uc, w_dw, b_dw, ln_g, ln_b, g_out):
    t = uc.shape[0]
    vec = lambda: pl.BlockSpec((1, D_CONV), lambda i: (0, 0))
    per = CONV_TT // HALO
    return pl.pallas_call(
        _conv_kernel,
        grid=(t // CONV_TT,),
        in_specs=[
            pl.BlockSpec((CONV_TT, 2 * D_CONV), lambda i: (i, 0)),
            pl.BlockSpec((HALO, 2 * D_CONV), lambda i: (jnp.maximum(i * per - 1, 0), 0)),
            pl.BlockSpec((HALO, D_CONV), lambda i: (0, 0)),
            vec(), vec(), vec(), vec(),
        ],
        out_specs=pl.BlockSpec((CONV_TT, D_CONV), lambda i: (i, 0)),
        out_shape=jax.ShapeDtypeStruct((t, D_CONV), BF16),
        scratch_shapes=[pltpu.VMEM((SUBLANES, CONV_TT + HALO, D_CONV), F32),
                        pltpu.VMEM((CONV_TT, D_CONV), F32)],
        compiler_params=_params(("parallel",)),
        name="conv",
    )(uc, uc, w_dw, b_dw, ln_g, ln_b, g_out)


OUT_TM = 512
D_PACK = D_MODEL // 2
HI_MASK = 0xFFFF0000


def _pack_bf16_pairs(h):
    lo = lax.bitcast_convert_type(h[:, :D_PACK].astype(BF16).astype(F32), jnp.uint32)
    hi = lax.bitcast_convert_type(h[:, D_PACK:].astype(BF16).astype(F32), jnp.uint32)
    return (lo >> 16) | (hi & jnp.uint32(HI_MASK))


def _unpack_bf16_pairs(w):
    lo = lax.bitcast_convert_type(w << 16, F32).astype(BF16)
    hi = lax.bitcast_convert_type(w & jnp.uint32(HI_MASK), F32).astype(BF16)
    return lo, hi


def _out_kernel(ma_ref, mc_ref, x_ref, w_ref, bo_ref, g1_ref, g2_ref, sc_ref, sh_ref, wr_ref, br_ref,
                tri_ref, xm_ref, h2_ref, idx_ref, rank_ref, wt_ref, cnt_ref, carry_ref):
    step = pl.program_id(0)

    @pl.when(step == 0)
    def _():
        carry_ref[...] = jnp.zeros_like(carry_ref)

    y = (jnp.dot(ma_ref[...], w_ref[0:D_ATTN, :], preferred_element_type=F32)
         + jnp.dot(mc_ref[...], w_ref[D_ATTN:, :], preferred_element_type=F32) + bo_ref[...])
    xm = x_ref[...] + g1_ref[...] * y
    xm_ref[...] = xm
    h2 = _modulated_rms(xm, g2_ref[...], sc_ref[...], sh_ref[...])
    h2_ref[...] = _pack_bf16_pairs(h2)

    h_hi = h2.astype(BF16)
    h_lo = (h2 - h_hi.astype(F32)).astype(BF16)
    both = (jnp.dot(h_hi, wr_ref[...], preferred_element_type=F32)
            + jnp.dot(h_lo, wr_ref[...], preferred_element_type=F32))
    logits = both[:, :N_EXPERTS] + both[:, N_EXPERTS:] + br_ref[...]
    tm = logits.shape[0]
    lane = lax.broadcasted_iota(jnp.int32, (tm, N_EXPERTS), 1).astype(F32)
    vals, idxs = [], []
    l = logits
    for _ in range(TOP_K):
        m = jnp.max(l, axis=-1, keepdims=True)
        i = jnp.min(jnp.where(l == m, lane, float(N_EXPERTS)), axis=-1, keepdims=True)
        vals.append(m)
        idxs.append(i)
        l = jnp.where(lane == i, -jnp.inf, l)
    es = [jnp.exp(v - vals[0]) for v in vals]
    tot = es[0] + es[1] + es[2] + es[3]
    ws = [e / tot for e in es]

    hot = [(lane == i).astype(F32) for i in idxs]
    hot_all = hot[0] + hot[1] + hot[2] + hot[3]
    before = jnp.dot(tri_ref[...], hot_all.astype(BF16), preferred_element_type=F32) + carry_ref[...]
    ranks = [jnp.sum(h * before, axis=-1, keepdims=True) for h in hot]
    carry_ref[...] = carry_ref[...] + jnp.sum(hot_all, axis=0, keepdims=True)
    cnt_ref[...] = carry_ref[...].astype(jnp.int32)

    slot = lax.broadcasted_iota(jnp.int32, (tm, LANES), 1)

    def pack(cols):
        out = jnp.zeros((tm, LANES), F32)
        for k in range(TOP_K):
            out = jnp.where(slot == k, cols[k], out)
        return out

    idx_ref[...] = pack(idxs).astype(jnp.int32)
    rank_ref[...] = pack(ranks).astype(jnp.int32)
    wt_ref[...] = pack(ws)


def _out_proj(ma, mc, x2, w_out_bf, b_out, gate1, g2, scale2, shift2, w_router, b_router):
    t = x2.shape[0]
    vec = lambda n: pl.BlockSpec((1, n), lambda i: (0, 0))
    tri = jnp.tril(jnp.ones((OUT_TM, OUT_TM), F32), -1).astype(BF16)
    wr_hi = w_router.astype(BF16)
    wr_lo = (w_router - wr_hi.astype(F32)).astype(BF16)
    w_router = jnp.concatenate([wr_hi, wr_lo], axis=1)
    return pl.pallas_call(
        _out_kernel,
        grid=(t // OUT_TM,),
        in_specs=[
            pl.BlockSpec((OUT_TM, D_ATTN), lambda i: (i, 0)),
            pl.BlockSpec((OUT_TM, D_CONV), lambda i: (i, 0)),
            pl.BlockSpec((OUT_TM, D_MODEL), lambda i: (i, 0)),
            pl.BlockSpec((D_MODEL, D_MODEL), lambda i: (0, 0)),
            vec(D_MODEL), vec(D_MODEL), vec(D_MODEL), vec(D_MODEL), vec(D_MODEL),
            pl.BlockSpec((D_MODEL, 2 * N_EXPERTS), lambda i: (0, 0)),
            vec(N_EXPERTS),
            pl.BlockSpec((OUT_TM, OUT_TM), lambda i: (0, 0)),
        ],
        out_specs=[
            pl.BlockSpec((OUT_TM, D_MODEL), lambda i: (i, 0)),
            pl.BlockSpec((OUT_TM, D_PACK), lambda i: (i, 0)),
            pl.BlockSpec((OUT_TM, LANES), lambda i: (i, 0)),
            pl.BlockSpec((OUT_TM, LANES), lambda i: (i, 0)),
            pl.BlockSpec((OUT_TM, LANES), lambda i: (i, 0)),
            pl.BlockSpec((1, N_EXPERTS), lambda i: (0, 0)),
        ],
        out_shape=[
            jax.ShapeDtypeStruct((t, D_MODEL), F32),
            jax.ShapeDtypeStruct((t, D_PACK), jnp.uint32),
            jax.ShapeDtypeStruct((t, LANES), jnp.int32),
            jax.ShapeDtypeStruct((t, LANES), jnp.int32),
            jax.ShapeDtypeStruct((t, LANES), F32),
            jax.ShapeDtypeStruct((1, N_EXPERTS), jnp.int32),
        ],
        scratch_shapes=[pltpu.VMEM((1, N_EXPERTS), F32)],
        compiler_params=_params(("arbitrary",)),
        name="out_proj",
    )(ma, mc, x2, w_out_bf, b_out, gate1, g2, scale2, shift2, w_router, b_router, tri)


DISP_TD = 256


def _dispatch_kernel(dest_ref, pad_lo_ref, pad_hi_ref, nu_ref, h2_ref, xs_ref, zero_ref, sems, zsem):
    step = pl.program_id(0)
    last = pl.num_programs(0) - 1
    slot = step % 2
    tok0 = step * DISP_TD

    def issue(r, carry):
        t = tok0 + r
        for k in range(TOP_K):
            d = dest_ref[t * TOP_K + k]
            pltpu.make_async_copy(h2_ref.at[pl.ds(t, 1), :], xs_ref.at[pl.ds(d, 1), :], sems.at[slot]).start()
        return carry

    lax.fori_loop(0, DISP_TD, issue, 0, unroll=2)

    def wait_chunk(s):
        n = DISP_TD * TOP_K
        pltpu.make_async_copy(h2_ref.at[pl.ds(0, n), :], xs_ref.at[pl.ds(0, n), :], sems.at[s]).wait()

    @pl.when(step == 0)
    def _():
        zero_ref[...] = jnp.zeros_like(zero_ref)

        def zero_row(d):
            return pltpu.make_async_copy(zero_ref.at[pl.ds(0, 1), :], xs_ref.at[pl.ds(d, 1), :], zsem)

        def zero_block(b):
            d = pl.multiple_of(b * EXPERT_TM, EXPERT_TM)
            return pltpu.make_async_copy(zero_ref, xs_ref.at[pl.ds(d, EXPERT_TM), :], zsem)

        def start_row(d, carry):
            zero_row(d).start()
            return carry

        def wait_row(d, carry):
            zero_row(d).wait()
            return carry

        def start_block(b, carry):
            zero_block(b).start()
            return carry

        def wait_block(b, carry):
            zero_block(b).wait()
            return carry

        def per_expert(e, carry):
            lax.fori_loop(pad_lo_ref[e], pad_hi_ref[e], start_row, 0)
            lax.fori_loop(pad_lo_ref[e], pad_hi_ref[e], wait_row, 0)
            return carry

        lax.fori_loop(0, N_EXPERTS, per_expert, 0)
        n_blocks = xs_ref.shape[0] // EXPERT_TM
        lax.fori_loop(nu_ref[0], n_blocks, start_block, 0)
        lax.fori_loop(nu_ref[0], n_blocks, wait_block, 0)

    @pl.when(step > 0)
    def _():
        wait_chunk(1 - slot)

    @pl.when(step == last)
    def _():
        wait_chunk(slot)


def _dispatch(h2p, dest_flat, pad_lo, pad_hi, n_used, n_rows):
    t = h2p.shape[0]
    grid_spec = pltpu.PrefetchScalarGridSpec(
        num_scalar_prefetch=4,
        grid=(t // DISP_TD,),
        in_specs=[pl.BlockSpec(memory_space=pl.ANY)],
        out_specs=pl.BlockSpec(memory_space=pl.ANY),
        scratch_shapes=[pltpu.VMEM((EXPERT_TM, D_PACK), jnp.uint32),
                        pltpu.SemaphoreType.DMA((2,)), pltpu.SemaphoreType.DMA(())],
    )
    return pl.pallas_call(
        _dispatch_kernel,
        grid_spec=grid_spec,
        out_shape=jax.ShapeDtypeStruct((n_rows, D_PACK), jnp.uint32),
        compiler_params=_params(("arbitrary",)),
        name="dispatch",
    )(dest_flat, pad_lo, pad_hi, n_used, h2p)


UP_TF = 1024
DOWN_TN = 2048


def _new_expert(be_ref, b):
    prev = be_ref[jnp.maximum(b - 1, 0)]
    return jnp.logical_or(b == 0, be_ref[b] != prev)


def _up_kernel(be_ref, nu_ref, x_ref, wg_ref, wu_ref, bg_ref, bu_ref, h_ref, wg_bf, wu_bf):
    b = pl.program_id(1)

    @pl.when(_new_expert(be_ref, b))
    def _():
        wg_bf[...] = wg_ref[0].astype(BF16)
        wu_bf[...] = wu_ref[0].astype(BF16)

    @pl.when(b < nu_ref[0])
    def _():
        x_lo, x_hi = _unpack_bf16_pairs(x_ref[...])

        def proj(w_bf, bias_ref):
            return (jnp.dot(x_lo, w_bf[0:D_PACK, :], preferred_element_type=F32)
                    + jnp.dot(x_hi, w_bf[D_PACK:, :], preferred_element_type=F32) + bias_ref[0])

        g = proj(wg_bf, bg_ref)
        lin = proj(wu_bf, bu_ref)
        g = jnp.minimum(g, SWIGLU_LIMIT)
        lin = jnp.clip(lin, -SWIGLU_LIMIT, SWIGLU_LIMIT)
        act = g * jax.nn.sigmoid(SWIGLU_ALPHA * g) * (lin + 1.0)
        h_ref[...] = act.astype(BF16)

    @pl.when(b >= nu_ref[0])
    def _():
        h_ref[...] = jnp.zeros_like(h_ref)


def _up(xs, block_expert, n_used, w_gate, b_gate, w_up, b_up):
    n_rows = xs.shape[0]
    nb = n_rows // EXPERT_TM
    row = lambda f, b, be, nu: (jnp.minimum(b, nu[0] - 1), 0)
    wsel = lambda f, b, be, nu: (be[b], 0, f)
    grid_spec = pltpu.PrefetchScalarGridSpec(
        num_scalar_prefetch=2,
        grid=(D_FF // UP_TF, nb),
        in_specs=[
            pl.BlockSpec((EXPERT_TM, D_PACK), row),
            pl.BlockSpec((1, D_MODEL, UP_TF), wsel),
            pl.BlockSpec((1, D_MODEL, UP_TF), wsel),
            pl.BlockSpec((1, 1, UP_TF), wsel),
            pl.BlockSpec((1, 1, UP_TF), wsel),
        ],
        out_specs=pl.BlockSpec((EXPERT_TM, UP_TF), lambda f, b, be, nu: (b, f)),
        scratch_shapes=[pltpu.VMEM((D_MODEL, UP_TF), BF16), pltpu.VMEM((D_MODEL, UP_TF), BF16)],
    )
    return pl.pallas_call(
        _up_kernel,
        grid_spec=grid_spec,
        out_shape=jax.ShapeDtypeStruct((n_rows, D_FF), BF16),
        compiler_params=_params(("arbitrary", "arbitrary")),
        name="expert_up",
    )(block_expert, n_used, xs, w_gate, w_up,
      b_gate.reshape(N_EXPERTS, 1, D_FF), b_up.reshape(N_EXPERTS, 1, D_FF))


def _down_kernel(be_ref, nu_ref, h_ref, wd_ref, bd_ref, y_ref, wd_bf):
    b = pl.program_id(1)

    @pl.when(_new_expert(be_ref, b))
    def _():
        wd_bf[...] = wd_ref[0].astype(BF16)

    @pl.when(b < nu_ref[0])
    def _():
        y_ref[...] = jnp.dot(h_ref[...], wd_bf[...], preferred_element_type=F32) + bd_ref[0]

    @pl.when(b >= nu_ref[0])
    def _():
        y_ref[...] = jnp.zeros_like(y_ref)


def _down(hs, block_expert, n_used, w_down, b_down):
    n_rows = hs.shape[0]
    nb = n_rows // EXPERT_TM
    row = lambda n, b, be, nu: (jnp.minimum(b, nu[0] - 1), 0)
    wsel = lambda n, b, be, nu: (be[b], 0, n)
    grid_spec = pltpu.PrefetchScalarGridSpec(
        num_scalar_prefetch=2,
        grid=(D_MODEL // DOWN_TN, nb),
        in_specs=[
            pl.BlockSpec((EXPERT_TM, D_FF), row),
            pl.BlockSpec((1, D_FF, DOWN_TN), wsel),
            pl.BlockSpec((1, 1, DOWN_TN), wsel),
        ],
        out_specs=pl.BlockSpec((EXPERT_TM, DOWN_TN), lambda n, b, be, nu: (b, n)),
        scratch_shapes=[pltpu.VMEM((D_FF, DOWN_TN), BF16)],
    )
    return pl.pallas_call(
        _down_kernel,
        grid_spec=grid_spec,
        out_shape=jax.ShapeDtypeStruct((n_rows, D_MODEL), F32),
        compiler_params=_params(("arbitrary", "arbitrary")),
        name="expert_down",
    )(block_expert, n_used, hs, w_down, b_down.reshape(N_EXPERTS, 1, D_MODEL))


COMB_TC = 128


def _combine_kernel(dest_ref, xm_ref, wt_ref, g2_ref, ys_ref, o_ref, buf_ref, sems):
    step = pl.program_id(0)
    last = pl.num_programs(0) - 1
    slot = step % 2

    def gather_tile(tile, s):
        def issue(r, carry):
            for k in range(TOP_K):
                d = dest_ref[(tile * COMB_TC + r) * TOP_K + k]
                pltpu.make_async_copy(ys_ref.at[pl.ds(d, 1), :], buf_ref.at[s, pl.ds(k * COMB_TC + r, 1), :],
                                      sems.at[s]).start()
            return carry

        lax.fori_loop(0, COMB_TC, issue, 0, unroll=2)

    @pl.when(step == 0)
    def _():
        gather_tile(0, 0)

    @pl.when(step < last)
    def _():
        gather_tile(step + 1, 1 - slot)

    pltpu.make_async_copy(ys_ref.at[pl.ds(0, TOP_K * COMB_TC), :], buf_ref.at[slot], sems.at[slot]).wait()

    wt = wt_ref[...]
    acc = wt[:, 0:1] * buf_ref[slot, 0:COMB_TC, :]
    for k in range(1, TOP_K):
        acc = acc + wt[:, k:k + 1] * buf_ref[slot, k * COMB_TC:(k + 1) * COMB_TC, :]
    o_ref[...] = xm_ref[...] + g2_ref[...] * acc


def _combine(ys, dest_flat, xm, wts, gate2):
    t = xm.shape[0]
    grid_spec = pltpu.PrefetchScalarGridSpec(
        num_scalar_prefetch=1,
        grid=(t // COMB_TC,),
        in_specs=[
            pl.BlockSpec((COMB_TC, D_MODEL), lambda i, d: (i, 0)),
            pl.BlockSpec((COMB_TC, LANES), lambda i, d: (i, 0)),
            pl.BlockSpec((1, D_MODEL), lambda i, d: (0, 0)),
            pl.BlockSpec(memory_space=pl.ANY),
        ],
        out_specs=pl.BlockSpec((COMB_TC, D_MODEL), lambda i, d: (i, 0)),
        scratch_shapes=[pltpu.VMEM((2, TOP_K * COMB_TC, D_MODEL), F32), pltpu.SemaphoreType.DMA((2,))],
    )
    return pl.pallas_call(
        _combine_kernel,
        grid_spec=grid_spec,
        out_shape=jax.ShapeDtypeStruct((t, D_MODEL), F32),
        compiler_params=_params(("arbitrary",)),
        name="combine",
    )(dest_flat, xm, wts, gate2, ys)


def _routing_tables(idx, rank, counts, n_tok):
    padded = ((counts + EXPERT_TM - 1) // EXPERT_TM) * EXPERT_TM
    pend = jnp.cumsum(padded)
    pstart = pend - padded
    experts = jnp.arange(N_EXPERTS, dtype=jnp.int32)
    start_of = jnp.sum(jnp.where(idx[..., None] == experts, pstart, 0), axis=-1)
    dest = (start_of + rank).astype(jnp.int32).reshape(-1)
    n_rows = n_tok * TOP_K + N_EXPERTS * EXPERT_TM
    nb = n_rows // EXPERT_TM
    block_start = jnp.arange(nb, dtype=jnp.int32) * EXPERT_TM
    block_expert = jnp.minimum(jnp.sum(pend[None, :] <= block_start[:, None], axis=1),
                               N_EXPERTS - 1).astype(jnp.int32)
    n_used = (pend[-1:] // EXPERT_TM).astype(jnp.int32)
    pad_lo = (pstart + counts).astype(jnp.int32)
    pad_hi = pend.astype(jnp.int32)
    return dest, block_expert, n_used, pad_lo, pad_hi, n_rows


def _layer(x2, c, w_ada, b_ada, g_norm1, w_in, b_in, g_q, g_k, sinks, rel_bias, w_dw, b_dw, ln_g, ln_b,
           g_out_attn, g_out_conv, w_out, b_out, g_norm2, w_router, b_router,
           w_gate, b_gate, w_up, b_up, w_down, b_down):
    t = x2.shape[0]
    row = lambda v: v.reshape(1, -1)
    mod = _ada(c, w_ada, b_ada)
    shift1, scale1, gate1, shift2, scale2, gate2 = [mod[:, i * D_MODEL:(i + 1) * D_MODEL] for i in range(6)]

    q, kv, uc = _in_proj(x2, row(g_norm1), scale1, shift1, w_in.astype(BF16), row(b_in))

    gq_t = row(jnp.tile(g_q, N_Q_HEADS)) * (HEAD_DIM ** -0.5)
    gk_t = row(jnp.tile(g_k, N_KV_HEADS))
    ma = _attention(q, kv, _bias_table(rel_bias), sinks, gq_t, gk_t, row(g_out_attn),
                    _head_indicator(D_ATTN), _head_indicator(LANES))
    w_dw_p = jnp.concatenate([w_dw, jnp.zeros((HALO - CONV_WIDTH, D_CONV), w_dw.dtype)], axis=0)
    mc = _conv(uc, w_dw_p, row(b_dw), row(ln_g), row(ln_b), row(g_out_conv))

    xm, h2, idx, rank, wts, counts = _out_proj(
        ma, mc, x2, w_out.astype(BF16), row(b_out), gate1, row(g_norm2), scale2, shift2,
        w_router, row(b_router))

    dest, block_expert, n_used, pad_lo, pad_hi, n_rows = _routing_tables(
        idx[:, :TOP_K], rank[:, :TOP_K], counts[0], t)
    xs = _dispatch(h2, dest, pad_lo, pad_hi, n_used, n_rows)
    hs = _up(xs, block_expert, n_used, w_gate, b_gate, w_up, b_up)
    ys = _down(hs, block_expert, n_used, w_down, b_down)
    return _combine(ys, dest, xm, wts, gate2)


def kernel(x, c, w_ada, b_ada, g_norm1, w_in, b_in, g_q, g_k, sinks, rel_bias, w_dw, b_dw, ln_g, ln_b,
           g_out_attn, g_out_conv, w_out, b_out, g_norm2, w_router, b_router,
           w_gate, b_gate, w_up, b_up, w_down, b_down):
    b, t, d = x.shape
    assert b == 1 and d == D_MODEL and w_ada.shape[0] == 1
    out = _layer(x.reshape(t, d), c, w_ada[0], b_ada[0], g_norm1[0], w_in[0], b_in[0], g_q[0], g_k[0],
                 sinks[0], rel_bias, w_dw[0], b_dw[0], ln_g[0], ln_b[0], g_out_attn[0], g_out_conv[0],
                 w_out[0], b_out[0], g_norm2[0], w_router[0], b_router[0],
                 w_gate[0], b_gate[0], w_up[0], b_up[0], w_down[0], b_down[0])
    return out.reshape(b, t, d)
```

```python
import functools
import math

import jax
import jax.numpy as jnp
from jax import lax
from jax.experimental import pallas as pl
from jax.experimental.pallas import tpu as pltpu

D_MODEL = 2048
HEAD_DIM = 64
N_Q_HEADS = 16
N_KV_HEADS = 2
D_ATTN = N_Q_HEADS * HEAD_DIM
D_KV = N_KV_HEADS * HEAD_DIM
D_CONV = D_MODEL - D_ATTN
D_IN = D_ATTN + 2 * D_KV + 2 * D_CONV
WINDOW = 128
BLOCK = 128
CONV_WIDTH = 31
N_BUCKETS = 32
MAX_DISTANCE = 128
N_EXPERTS = 32
TOP_K = 4
D_FF = D_MODEL
SWIGLU_LIMIT = 7.0
SWIGLU_ALPHA = 1.702
EPS = 1e-6
NEG_INF = -1e30

LANES = 128
SUBLANES = 8
VMEM_LIMIT = 56 * 1024 * 1024

HALO = 32
EXPERT_TM = 256
F32 = jnp.float32
BF16 = jnp.bfloat16


def _params(sem):
    return pltpu.CompilerParams(dimension_semantics=sem, vmem_limit_bytes=VMEM_LIMIT)


ADA_TN = 512
ADA_RC = 256


def _ada_kernel(c_ref, w_ref, b_ref, o_ref):
    tn = w_ref.shape[1]

    def body(i, acc):
        r = pl.multiple_of(i * ADA_RC, ADA_RC)
        c = c_ref[pl.ds(r, ADA_RC), :]
        sc = c * jax.nn.sigmoid(c)
        prod = w_ref[pl.ds(r, ADA_RC), :] * sc
        return acc + jnp.sum(prod.reshape(ADA_RC // 8, 8, tn), axis=0)

    acc = lax.fori_loop(0, D_MODEL // ADA_RC, body, jnp.zeros((8, tn), F32))
    o_ref[...] = jnp.sum(acc, axis=0, keepdims=True) + b_ref[...]


def _ada(c, w_ada, b_ada):
    n = w_ada.shape[1]
    return pl.pallas_call(
        _ada_kernel,
        grid=(n // ADA_TN,),
        in_specs=[
            pl.BlockSpec((D_MODEL, 1), lambda j: (0, 0)),
            pl.BlockSpec((D_MODEL, ADA_TN), lambda j: (0, j)),
            pl.BlockSpec((1, ADA_TN), lambda j: (0, j)),
        ],
        out_specs=pl.BlockSpec((1, ADA_TN), lambda j: (0, j)),
        out_shape=jax.ShapeDtypeStruct((1, n), F32),
        compiler_params=_params(("parallel",)),
        name="ada",
    )(c.reshape(D_MODEL, 1), w_ada, b_ada.reshape(1, n))


IN_TM = 512
IN_NC = 256


def _modulated_rms(x, g, scale, shift):
    ms = jnp.mean(x * x, axis=-1, keepdims=True)
    return (x * lax.rsqrt(ms + EPS) * g) * (1.0 + scale) + shift


def _in_kernel(x_ref, g_ref, sc_ref, sh_ref, w_ref, b_ref, q_ref, kv_ref, uc_ref):
    h = _modulated_rms(x_ref[...], g_ref[...], sc_ref[...], sh_ref[...]).astype(BF16)

    def proj(lo, n, o_ref):
        for c in range(0, n, IN_NC):
            u = jnp.dot(h, w_ref[:, lo + c:lo + c + IN_NC], preferred_element_type=F32)
            o_ref[:, c:c + IN_NC] = u + b_ref[:, lo + c:lo + c + IN_NC]

    proj(0, D_ATTN, q_ref)
    proj(D_ATTN, 2 * D_KV, kv_ref)
    proj(D_ATTN + 2 * D_KV, 2 * D_CONV, uc_ref)


def _in_proj(x2, g1, scale1, shift1, w_in_bf, b_in):
    t = x2.shape[0]
    vec = lambda n: pl.BlockSpec((1, n), lambda i: (0, 0))
    return pl.pallas_call(
        _in_kernel,
        grid=(t // IN_TM,),
        in_specs=[
            pl.BlockSpec((IN_TM, D_MODEL), lambda i: (i, 0)),
            vec(D_MODEL), vec(D_MODEL), vec(D_MODEL),
            pl.BlockSpec((D_MODEL, D_IN), lambda i: (0, 0)),
            vec(D_IN),
        ],
        out_specs=[
            pl.BlockSpec((IN_TM, D_ATTN), lambda i: (i, 0)),
            pl.BlockSpec((IN_TM, 2 * D_KV), lambda i: (i, 0)),
            pl.BlockSpec((IN_TM, 2 * D_CONV), lambda i: (i, 0)),
        ],
        out_shape=[
            jax.ShapeDtypeStruct((t, D_ATTN), F32),
            jax.ShapeDtypeStruct((t, 2 * D_KV), F32),
            jax.ShapeDtypeStruct((t, 2 * D_CONV), F32),
        ],
        compiler_params=_params(("parallel",)),
        name="in_proj",
    )(x2, g1, scale1, shift1, w_in_bf, b_in)


ATT_R = 4


def _split_dot(a, b_bf):
    hi = a.astype(BF16)
    lo = (a - hi.astype(F32)).astype(BF16)
    return (jnp.dot(hi, b_bf, preferred_element_type=F32)
            + jnp.dot(lo, b_bf, preferred_element_type=F32))


def _attn_kernel(sinks_ref, q_ref, kvp_ref, kvc_ref, bias_ref, gq_ref, gk_ref, go_ref,
                 hq_ref, hk_ref, o_ref, y_ref):
    step = pl.program_id(0)
    lane = lax.broadcasted_iota(jnp.int32, (2 * BLOCK, LANES), 1)
    low = lane < HEAD_DIM
    col = lax.broadcasted_iota(jnp.int32, (BLOCK, 2 * BLOCK), 1)
    first_lo = jnp.where(step == 0, BLOCK, 0)

    for r in range(ATT_R):
        rows = slice(r * BLOCK, (r + 1) * BLOCK)
        q = q_ref[rows, :]
        ssq = _split_dot(q * q, hq_ref[...])
        qn = (q * lax.rsqrt(ssq * (1.0 / HEAD_DIM) + EPS) * gq_ref[...]).astype(BF16)

        if r == 0:
            kv_prev = kvp_ref[...]
        else:
            kv_prev = kvc_ref[(r - 1) * BLOCK:r * BLOCK, :]
        kv = jnp.concatenate([kv_prev, kvc_ref[rows, :]], axis=0)
        k = kv[:, :LANES]
        v = kv[:, LANES:]
        kss = _split_dot(k * k, hk_ref[...])
        kn = k * lax.rsqrt(kss * (1.0 / HEAD_DIM) + EPS) * gk_ref[...]
        kn_sw = pltpu.roll(kn, HEAD_DIM, axis=1)
        v_sw = pltpu.roll(v, HEAD_DIM, axis=1)
        zero = jnp.zeros_like(kn)
        k_lo = [jnp.where(low, kn, zero).astype(BF16), jnp.where(low, kn_sw, zero).astype(BF16)]
        k_hi = [jnp.where(low, zero, kn_sw).astype(BF16), jnp.where(low, zero, kn).astype(BF16)]
        v_lo = [jnp.where(low, v, zero).astype(BF16), jnp.where(low, v_sw, zero).astype(BF16)]
        v_hi = [jnp.where(low, zero, v_sw).astype(BF16), jnp.where(low, zero, v).astype(BF16)]

        for p in range(N_Q_HEADS // 2):
            g = (2 * p) // (N_Q_HEADS // N_KV_HEADS)
            qp = qn[:, p * LANES:(p + 1) * LANES]
            acc = None
            for half, (kz, vz) in enumerate(((k_lo[g], v_lo[g]), (k_hi[g], v_hi[g]))):
                h = 2 * p + half
                s = lax.dot_general(qp, kz, (((1,), (1,)), ((), ())), preferred_element_type=F32)
                s = s + bias_ref[h]
                if r == 0:
                    s = jnp.where(col >= first_lo, s, NEG_INF)
                sink = sinks_ref[h]
                m = jnp.maximum(jnp.max(s, axis=-1, keepdims=True), sink)
                e = jnp.exp(s - m)
                denom = jnp.sum(e, axis=-1, keepdims=True) + jnp.exp(sink - m)
                pv = jnp.dot(e.astype(BF16), vz, preferred_element_type=F32)
                pv = pv * (1.0 / denom)
                acc = pv if acc is None else acc + pv
            y_ref[:, p * LANES:(p + 1) * LANES] = acc

        y = y_ref[...]
        ms = jnp.mean(y * y, axis=-1, keepdims=True)
        o_ref[rows, :] = (y * lax.rsqrt(ms + EPS) * go_ref[...]).astype(BF16)


def _attention(q, kv, bias, sinks, gq_t, gk_t, g_out, hq, hk):
    t = q.shape[0]
    tile = ATT_R * BLOCK
    grid_spec = pltpu.PrefetchScalarGridSpec(
        num_scalar_prefetch=0,
        grid=(t // tile,),
        in_specs=[
            pl.BlockSpec(memory_space=pltpu.SMEM),
            pl.BlockSpec((tile, D_ATTN), lambda i: (i, 0)),
            pl.BlockSpec((BLOCK, 2 * D_KV), lambda i: (jnp.maximum(i * ATT_R - 1, 0), 0)),
            pl.BlockSpec((tile, 2 * D_KV), lambda i: (i, 0)),
            pl.BlockSpec((N_Q_HEADS, BLOCK, 2 * BLOCK), lambda i: (0, 0, 0)),
            pl.BlockSpec((1, D_ATTN), lambda i: (0, 0)),
            pl.BlockSpec((1, LANES), lambda i: (0, 0)),
            pl.BlockSpec((1, D_ATTN), lambda i: (0, 0)),
            pl.BlockSpec((D_ATTN, D_ATTN), lambda i: (0, 0)),
            pl.BlockSpec((LANES, LANES), lambda i: (0, 0)),
        ],
        out_specs=pl.BlockSpec((tile, D_ATTN), lambda i: (i, 0)),
        scratch_shapes=[pltpu.VMEM((BLOCK, D_ATTN), F32)],
    )
    return pl.pallas_call(
        _attn_kernel,
        grid_spec=grid_spec,
        out_shape=jax.ShapeDtypeStruct((t, D_ATTN), BF16),
        compiler_params=_params(("parallel",)),
        name="attn",
    )(sinks, q, kv, kv, bias, gq_t, gk_t, g_out, hq, hk)


def _t5_bucket(dist):
    max_exact = N_BUCKETS // 2
    d = jnp.maximum(dist, 0)
    log_ratio = jnp.log(jnp.maximum(d, max_exact).astype(F32) / max_exact)
    large = max_exact + (log_ratio / math.log(MAX_DISTANCE / max_exact)
                         * (N_BUCKETS - max_exact)).astype(jnp.int32)
    large = jnp.minimum(large, N_BUCKETS - 1)
    return jnp.where(d < max_exact, d, large)


def _bias_table(rel_bias):
    q_local = jnp.arange(BLOCK, dtype=jnp.int32) + BLOCK
    k_local = jnp.arange(2 * BLOCK, dtype=jnp.int32)
    dist = q_local[:, None] - k_local[None, :]
    band = (dist >= 0) & (dist < WINDOW)
    bucket = _t5_bucket(dist)
    table = rel_bias.astype(F32).T
    hit = bucket[None, :, :, None] == jnp.arange(N_BUCKETS, dtype=jnp.int32)
    bias = jnp.sum(jnp.where(hit, table[:, None, None, :], 0.0), axis=-1)
    return jnp.where(band[None], bias, NEG_INF)


def _head_indicator(n):
    i = jnp.arange(n) // HEAD_DIM
    return (i[:, None] == i[None, :]).astype(BF16)


CONV_TT = 256
CONV_RC = 64
CONV_CC = 256


def _conv_kernel(u_ref, halo_ref, w_ref, b_ref, lg_ref, lb_ref, go_ref, o_ref, h_ref, y_ref):
    step = pl.program_id(0)

    def glu(u):
        return u[:, :D_CONV] * jax.nn.sigmoid(u[:, D_CONV:])

    hh = glu(halo_ref[...])
    h_ref[0, 0:HALO, :] = jnp.where(step == 0, jnp.zeros_like(hh), hh)
    h_ref[0, HALO:, :] = glu(u_ref[...])

    n_shifted = CONV_TT + HALO - SUBLANES
    for s in range(1, SUBLANES):
        for c0 in range(0, D_CONV, CONV_CC):
            cs = slice(c0, c0 + CONV_CC)
            h_ref[s, 0:n_shifted, cs] = h_ref[0, s:s + n_shifted, cs]

    off = HALO - (CONV_WIDTH - 1)
    for r0 in range(0, CONV_TT, CONV_RC):
        for c0 in range(0, D_CONV, CONV_CC):
            cs = slice(c0, c0 + CONV_CC)
            acc = jnp.broadcast_to(b_ref[:, cs], (CONV_RC, CONV_CC))
            for j in range(CONV_WIDTH):
                s = (r0 + off + j) % SUBLANES
                a = r0 + off + j - s
                acc = acc + w_ref[j:j + 1, cs] * h_ref[s, a:a + CONV_RC, cs]
            y_ref[r0:r0 + CONV_RC, cs] = acc

    y = y_ref[...]
    mu = jnp.mean(y, axis=-1, keepdims=True)
    yc = y - mu
    var = jnp.mean(yc * yc, axis=-1, keepdims=True)
    z = yc * lax.rsqrt(var + EPS) * lg_ref[...] + lb_ref[...]
    s = z * jax.nn.sigmoid(z)
    ms = jnp.mean(s * s, axis=-1, keepdims=True)
    o_ref[...] = (s * lax.rsqrt(ms + EPS) * go_ref[...]).astype(BF16)


def _conv(uc, w_dw, b_dw, ln_g, ln_b, g_out):
    t = uc.shape[0]
    vec = lambda: pl.BlockSpec((1, D_CONV), lambda i: (0, 0))
    per = CONV_TT // HALO
    return pl.pallas_call(
        _conv_kernel,
        grid=(t // CONV_TT,),
        in_specs=[
            pl.BlockSpec((CONV_TT, 2 * D_CONV), lambda i: (i, 0)),
            pl.BlockSpec((HALO, 2 * D_CONV), lambda i: (jnp.maximum(i * per - 1, 0), 0)),
            pl.BlockSpec((HALO, D_CONV), lambda i: (0, 0)),
            vec(), vec(), vec(), vec(),
        ],
        out_specs=pl.BlockSpec((CONV_TT, D_CONV), lambda i: (i, 0)),
        out_shape=jax.ShapeDtypeStruct((t, D_CONV), BF16),
        scratch_shapes=[pltpu.VMEM((SUBLANES, CONV_TT + HALO, D_CONV), F32),
                        pltpu.VMEM((CONV_TT, D_CONV), F32)],
        compiler_params=_params(("parallel",)),
        name="conv",
    )(uc, uc, w_dw, b_dw, ln_g, ln_b, g_out)


OUT_TM = 512
D_PACK = D_MODEL // 2
HI_MASK = 0xFFFF0000


def _pack_bf16_pairs(h):
    lo = lax.bitcast_convert_type(h[:, :D_PACK].astype(BF16).astype(F32), jnp.uint32)
    hi = lax.bitcast_convert_type(h[:, D_PACK:].astype(BF16).astype(F32), jnp.uint32)
    return (lo >> 16) | (hi & jnp.uint32(HI_MASK))


def _store_rows(ref, v):
    for s in range(ref.shape[1]):
        ref[:, s, :] = v[:, s * LANES:(s + 1) * LANES]


def _load_rows(ref):
    return jnp.concatenate([ref[:, s, :] for s in range(ref.shape[1])], axis=1)


def _unpack_bf16_pairs(w):
    lo = lax.bitcast_convert_type(w << 16, F32).astype(BF16)
    hi = lax.bitcast_convert_type(w & jnp.uint32(HI_MASK), F32).astype(BF16)
    return lo, hi


def _out_kernel(ma_ref, mc_ref, x_ref, w_ref, bo_ref, g1_ref, g2_ref, sc_ref, sh_ref, wr_ref, br_ref,
                tri_ref, xm_ref, h2_ref, idx_ref, rank_ref, wt_ref, cnt_ref, carry_ref):
    step = pl.program_id(0)

    @pl.when(step == 0)
    def _():
        carry_ref[...] = jnp.zeros_like(carry_ref)

    y = (jnp.dot(ma_ref[...], w_ref[0:D_ATTN, :], preferred_element_type=F32)
         + jnp.dot(mc_ref[...], w_ref[D_ATTN:, :], preferred_element_type=F32) + bo_ref[...])
    xm = x_ref[...] + g1_ref[...] * y
    xm_ref[...] = xm
    h2 = _modulated_rms(xm, g2_ref[...], sc_ref[...], sh_ref[...])
    _store_rows(h2_ref, _pack_bf16_pairs(h2))

    h_hi = h2.astype(BF16)
    h_lo = (h2 - h_hi.astype(F32)).astype(BF16)
    both = (jnp.dot(h_hi, wr_ref[...], preferred_element_type=F32)
            + jnp.dot(h_lo, wr_ref[...], preferred_element_type=F32))
    logits = both[:, :N_EXPERTS] + both[:, N_EXPERTS:] + br_ref[...]
    tm = logits.shape[0]
    lane = lax.broadcasted_iota(jnp.int32, (tm, N_EXPERTS), 1).astype(F32)
    vals, idxs = [], []
    l = logits
    for _ in range(TOP_K):
        m = jnp.max(l, axis=-1, keepdims=True)
        i = jnp.min(jnp.where(l == m, lane, float(N_EXPERTS)), axis=-1, keepdims=True)
        vals.append(m)
        idxs.append(i)
        l = jnp.where(lane == i, -jnp.inf, l)
    es = [jnp.exp(v - vals[0]) for v in vals]
    tot = es[0] + es[1] + es[2] + es[3]
    ws = [e / tot for e in es]

    hot = [(lane == i).astype(F32) for i in idxs]
    hot_all = hot[0] + hot[1] + hot[2] + hot[3]
    before = jnp.dot(tri_ref[...], hot_all.astype(BF16), preferred_element_type=F32) + carry_ref[...]
    ranks = [jnp.sum(h * before, axis=-1, keepdims=True) for h in hot]
    carry_ref[...] = carry_ref[...] + jnp.sum(hot_all, axis=0, keepdims=True)
    cnt_ref[...] = carry_ref[...].astype(jnp.int32)

    slot = lax.broadcasted_iota(jnp.int32, (tm, LANES), 1)

    def pack(cols):
        out = jnp.zeros((tm, LANES), F32)
        for k in range(TOP_K):
            out = jnp.where(slot == k, cols[k], out)
        return out

    idx_ref[...] = pack(idxs).astype(jnp.int32)
    rank_ref[...] = pack(ranks).astype(jnp.int32)
    wt_ref[...] = pack(ws)


def _out_proj(ma, mc, x2, w_out_bf, b_out, gate1, g2, scale2, shift2, w_router, b_router):
    t = x2.shape[0]
    vec = lambda n: pl.BlockSpec((1, n), lambda i: (0, 0))
    tri = jnp.tril(jnp.ones((OUT_TM, OUT_TM), F32), -1).astype(BF16)
    wr_hi = w_router.astype(BF16)
    wr_lo = (w_router - wr_hi.astype(F32)).astype(BF16)
    w_router = jnp.concatenate([wr_hi, wr_lo], axis=1)
    return pl.pallas_call(
        _out_kernel,
        grid=(t // OUT_TM,),
        in_specs=[
            pl.BlockSpec((OUT_TM, D_ATTN), lambda i: (i, 0)),
            pl.BlockSpec((OUT_TM, D_CONV), lambda i: (i, 0)),
            pl.BlockSpec((OUT_TM, D_MODEL), lambda i: (i, 0)),
            pl.BlockSpec((D_MODEL, D_MODEL), lambda i: (0, 0)),
            vec(D_MODEL), vec(D_MODEL), vec(D_MODEL), vec(D_MODEL), vec(D_MODEL),
            pl.BlockSpec((D_MODEL, 2 * N_EXPERTS), lambda i: (0, 0)),
            vec(N_EXPERTS),
            pl.BlockSpec((OUT_TM, OUT_TM), lambda i: (0, 0)),
        ],
        out_specs=[
            pl.BlockSpec((OUT_TM, D_MODEL), lambda i: (i, 0)),
            pl.BlockSpec((OUT_TM, D_PACK // LANES, LANES), lambda i: (i, 0, 0)),
            pl.BlockSpec((OUT_TM, LANES), lambda i: (i, 0)),
            pl.BlockSpec((OUT_TM, LANES), lambda i: (i, 0)),
            pl.BlockSpec((OUT_TM, LANES), lambda i: (i, 0)),
            pl.BlockSpec((1, N_EXPERTS), lambda i: (0, 0)),
        ],
        out_shape=[
            jax.ShapeDtypeStruct((t, D_MODEL), F32),
            jax.ShapeDtypeStruct((t, D_PACK // LANES, LANES), jnp.uint32),
            jax.ShapeDtypeStruct((t, LANES), jnp.int32),
            jax.ShapeDtypeStruct((t, LANES), jnp.int32),
            jax.ShapeDtypeStruct((t, LANES), F32),
            jax.ShapeDtypeStruct((1, N_EXPERTS), jnp.int32),
        ],
        scratch_shapes=[pltpu.VMEM((1, N_EXPERTS), F32)],
        compiler_params=_params(("arbitrary",)),
        name="out_proj",
    )(ma, mc, x2, w_out_bf, b_out, gate1, g2, scale2, shift2, w_router, b_router, tri)


DISP_TD = 256
DISP_NBUF = 3


def _dispatch_kernel(dest_ref, pad_lo_ref, pad_hi_ref, nu_ref, h2_ref, xs_ref,
                     stage_ref, zero_ref, in_sems, out_sems, zsem):
    step = pl.program_id(0)
    last = pl.num_programs(0) - 1
    slot = lax.rem(step, DISP_NBUF)
    rows_per_chunk = DISP_TD * TOP_K

    def load(chunk, s):
        src = h2_ref.at[pl.ds(pl.multiple_of(chunk * DISP_TD, DISP_TD), DISP_TD)]
        return pltpu.make_async_copy(src, stage_ref.at[s], in_sems.at[s])

    def wait_chunk(s):
        pltpu.make_async_copy(h2_ref.at[pl.ds(0, rows_per_chunk)], xs_ref.at[pl.ds(0, rows_per_chunk)],
                              out_sems.at[s]).wait()

    @pl.when(step == 0)
    def _():
        load(0, 0).start()

        @pl.when(last >= 1)
        def _():
            load(1, 1).start()

    load(step, slot).wait()

    def issue(r, carry):
        for k in range(TOP_K):
            d = dest_ref[(step * DISP_TD + r) * TOP_K + k]
            pltpu.make_async_copy(stage_ref.at[slot, r], xs_ref.at[d], out_sems.at[slot]).start(priority=k % 2)
        return carry

    lax.fori_loop(0, DISP_TD, issue, 0, unroll=2)

    @pl.when(step == 0)
    def _():
        zero_ref[...] = jnp.zeros_like(zero_ref)

        def zero_row(d):
            return pltpu.make_async_copy(zero_ref.at[0], xs_ref.at[d], zsem)

        def zero_block(b):
            d = pl.multiple_of(b * EXPERT_TM, EXPERT_TM)
            return pltpu.make_async_copy(zero_ref, xs_ref.at[pl.ds(d, EXPERT_TM)], zsem)

        def start_row(d, carry):
            zero_row(d).start()
            return carry

        def wait_row(d, carry):
            zero_row(d).wait()
            return carry

        def start_block(b, carry):
            zero_block(b).start()
            return carry

        def wait_block(b, carry):
            zero_block(b).wait()
            return carry

        def per_expert(e, carry):
            lax.fori_loop(pad_lo_ref[e], pad_hi_ref[e], start_row, 0)
            lax.fori_loop(pad_lo_ref[e], pad_hi_ref[e], wait_row, 0)
            return carry

        lax.fori_loop(0, N_EXPERTS, per_expert, 0)
        n_blocks = xs_ref.shape[0] // EXPERT_TM
        lax.fori_loop(nu_ref[0], n_blocks, start_block, 0)
        lax.fori_loop(nu_ref[0], n_blocks, wait_block, 0)

    @pl.when(step > 0)
    def _():
        wait_chunk(lax.rem(step + DISP_NBUF - 1, DISP_NBUF))

    @pl.when(step + 2 <= last)
    def _():
        load(step + 2, lax.rem(step + 2, DISP_NBUF)).start()

    @pl.when(step == last)
    def _():
        wait_chunk(slot)


def _dispatch(h2p, dest_flat, pad_lo, pad_hi, n_used, n_rows):
    t, n_sub, _ = h2p.shape
    grid_spec = pltpu.PrefetchScalarGridSpec(
        num_scalar_prefetch=4,
        grid=(t // DISP_TD,),
        in_specs=[pl.BlockSpec(memory_space=pl.ANY)],
        out_specs=pl.BlockSpec(memory_space=pl.ANY),
        scratch_shapes=[pltpu.VMEM((DISP_NBUF, DISP_TD, n_sub, LANES), jnp.uint32),
                        pltpu.VMEM((EXPERT_TM, n_sub, LANES), jnp.uint32),
                        pltpu.SemaphoreType.DMA((DISP_NBUF,)), pltpu.SemaphoreType.DMA((DISP_NBUF,)),
                        pltpu.SemaphoreType.DMA(())],
    )
    return pl.pallas_call(
        _dispatch_kernel,
        grid_spec=grid_spec,
        out_shape=jax.ShapeDtypeStruct((n_rows, n_sub, LANES), jnp.uint32),
        compiler_params=_params(("arbitrary",)),
        name="dispatch",
    )(dest_flat, pad_lo, pad_hi, n_used, h2p)


UP_TF = 1024
DOWN_TN = 2048


def _new_expert(be_ref, b):
    prev = be_ref[jnp.maximum(b - 1, 0)]
    return jnp.logical_or(b == 0, be_ref[b] != prev)


def _up_kernel(be_ref, nu_ref, x_ref, wg_ref, wu_ref, bg_ref, bu_ref, h_ref, wg_bf, wu_bf):
    b = pl.program_id(1)

    @pl.when(_new_expert(be_ref, b))
    def _():
        wg_bf[...] = wg_ref[0].astype(BF16)
        wu_bf[...] = wu_ref[0].astype(BF16)

    @pl.when(b < nu_ref[0])
    def _():
        x_lo, x_hi = _unpack_bf16_pairs(_load_rows(x_ref))

        def proj(w_bf, bias_ref):
            return (jnp.dot(x_lo, w_bf[0:D_PACK, :], preferred_element_type=F32)
                    + jnp.dot(x_hi, w_bf[D_PACK:, :], preferred_element_type=F32) + bias_ref[0])

        g = proj(wg_bf, bg_ref)
        lin = proj(wu_bf, bu_ref)
        g = jnp.minimum(g, SWIGLU_LIMIT)
        lin = jnp.clip(lin, -SWIGLU_LIMIT, SWIGLU_LIMIT)
        act = g * jax.nn.sigmoid(SWIGLU_ALPHA * g) * (lin + 1.0)
        h_ref[...] = act.astype(BF16)

    @pl.when(b >= nu_ref[0])
    def _():
        h_ref[...] = jnp.zeros_like(h_ref)


def _up(xs, block_expert, n_used, w_gate, b_gate, w_up, b_up):
    n_rows = xs.shape[0]
    nb = n_rows // EXPERT_TM
    row = lambda f, b, be, nu: (jnp.minimum(b, nu[0] - 1), 0, 0)
    wsel = lambda f, b, be, nu: (be[b], 0, f)
    grid_spec = pltpu.PrefetchScalarGridSpec(
        num_scalar_prefetch=2,
        grid=(D_FF // UP_TF, nb),
        in_specs=[
            pl.BlockSpec((EXPERT_TM, D_PACK // LANES, LANES), row),
            pl.BlockSpec((1, D_MODEL, UP_TF), wsel),
            pl.BlockSpec((1, D_MODEL, UP_TF), wsel),
            pl.BlockSpec((1, 1, UP_TF), wsel),
            pl.BlockSpec((1, 1, UP_TF), wsel),
        ],
        out_specs=pl.BlockSpec((EXPERT_TM, UP_TF), lambda f, b, be, nu: (b, f)),
        scratch_shapes=[pltpu.VMEM((D_MODEL, UP_TF), BF16), pltpu.VMEM((D_MODEL, UP_TF), BF16)],
    )
    return pl.pallas_call(
        _up_kernel,
        grid_spec=grid_spec,
        out_shape=jax.ShapeDtypeStruct((n_rows, D_FF), BF16),
        compiler_params=_params(("arbitrary", "arbitrary")),
        name="expert_up",
    )(block_expert, n_used, xs, w_gate, w_up,
      b_gate.reshape(N_EXPERTS, 1, D_FF), b_up.reshape(N_EXPERTS, 1, D_FF))


def _down_kernel(be_ref, nu_ref, h_ref, wd_ref, bd_ref, y_ref, wd_bf):
    b = pl.program_id(1)

    @pl.when(_new_expert(be_ref, b))
    def _():
        wd_bf[...] = wd_ref[0].astype(BF16)

    @pl.when(b < nu_ref[0])
    def _():
        y_ref[...] = jnp.dot(h_ref[...], wd_bf[...], preferred_element_type=F32) + bd_ref[0]

    @pl.when(b >= nu_ref[0])
    def _():
        y_ref[...] = jnp.zeros_like(y_ref)


def _down(hs, block_expert, n_used, w_down, b_down):
    n_rows = hs.shape[0]
    nb = n_rows // EXPERT_TM
    row = lambda n, b, be, nu: (jnp.minimum(b, nu[0] - 1), 0)
    wsel = lambda n, b, be, nu: (be[b], 0, n)
    grid_spec = pltpu.PrefetchScalarGridSpec(
        num_scalar_prefetch=2,
        grid=(D_MODEL // DOWN_TN, nb),
        in_specs=[
            pl.BlockSpec((EXPERT_TM, D_FF), row),
            pl.BlockSpec((1, D_FF, DOWN_TN), wsel),
            pl.BlockSpec((1, 1, DOWN_TN), wsel),
        ],
        out_specs=pl.BlockSpec((EXPERT_TM, DOWN_TN), lambda n, b, be, nu: (b, n)),
        scratch_shapes=[pltpu.VMEM((D_FF, DOWN_TN), BF16)],
    )
    return pl.pallas_call(
        _down_kernel,
        grid_spec=grid_spec,
        out_shape=jax.ShapeDtypeStruct((n_rows, D_MODEL), F32),
        compiler_params=_params(("arbitrary", "arbitrary")),
        name="expert_down",
    )(block_expert, n_used, hs, w_down, b_down.reshape(N_EXPERTS, 1, D_MODEL))


COMB_TC = 128


def _combine_kernel(dest_ref, xm_ref, wt_ref, g2_ref, ys_ref, o_ref, buf_ref, sems):
    step = pl.program_id(0)
    last = pl.num_programs(0) - 1
    slot = step % 2

    def gather_tile(tile, s):
        def issue(r, carry):
            for k in range(TOP_K):
                d = dest_ref[(tile * COMB_TC + r) * TOP_K + k]
                pltpu.make_async_copy(ys_ref.at[pl.ds(d, 1), :], buf_ref.at[s, pl.ds(k * COMB_TC + r, 1), :],
                                      sems.at[s]).start(priority=k % 2)
            return carry

        lax.fori_loop(0, COMB_TC, issue, 0, unroll=2)

    @pl.when(step == 0)
    def _():
        gather_tile(0, 0)

    @pl.when(step < last)
    def _():
        gather_tile(step + 1, 1 - slot)

    pltpu.make_async_copy(ys_ref.at[pl.ds(0, TOP_K * COMB_TC), :], buf_ref.at[slot], sems.at[slot]).wait()

    wt = wt_ref[...]
    acc = wt[:, 0:1] * buf_ref[slot, 0:COMB_TC, :]
    for k in range(1, TOP_K):
        acc = acc + wt[:, k:k + 1] * buf_ref[slot, k * COMB_TC:(k + 1) * COMB_TC, :]
    o_ref[...] = xm_ref[...] + g2_ref[...] * acc


def _combine(ys, dest_flat, xm, wts, gate2):
    t = xm.shape[0]
    grid_spec = pltpu.PrefetchScalarGridSpec(
        num_scalar_prefetch=1,
        grid=(t // COMB_TC,),
        in_specs=[
            pl.BlockSpec((COMB_TC, D_MODEL), lambda i, d: (i, 0)),
            pl.BlockSpec((COMB_TC, LANES), lambda i, d: (i, 0)),
            pl.BlockSpec((1, D_MODEL), lambda i, d: (0, 0)),
            pl.BlockSpec(memory_space=pl.ANY),
        ],
        out_specs=pl.BlockSpec((COMB_TC, D_MODEL), lambda i, d: (i, 0)),
        scratch_shapes=[pltpu.VMEM((2, TOP_K * COMB_TC, D_MODEL), F32), pltpu.SemaphoreType.DMA((2,))],
    )
    return pl.pallas_call(
        _combine_kernel,
        grid_spec=grid_spec,
        out_shape=jax.ShapeDtypeStruct((t, D_MODEL), F32),
        compiler_params=_params(("arbitrary",)),
        name="combine",
    )(dest_flat, xm, wts, gate2, ys)


def _routing_tables(idx, rank, counts, n_tok):
    padded = ((counts + EXPERT_TM - 1) // EXPERT_TM) * EXPERT_TM
    pend = jnp.cumsum(padded)
    pstart = pend - padded
    experts = jnp.arange(N_EXPERTS, dtype=jnp.int32)
    start_of = jnp.sum(jnp.where(idx[..., None] == experts, pstart, 0), axis=-1)
    dest = (start_of + rank).astype(jnp.int32).reshape(-1)
    n_rows = n_tok * TOP_K + N_EXPERTS * EXPERT_TM
    nb = n_rows // EXPERT_TM
    block_start = jnp.arange(nb, dtype=jnp.int32) * EXPERT_TM
    block_expert = jnp.minimum(jnp.sum(pend[None, :] <= block_start[:, None], axis=1),
                               N_EXPERTS - 1).astype(jnp.int32)
    n_used = (pend[-1:] // EXPERT_TM).astype(jnp.int32)
    pad_lo = (pstart + counts).astype(jnp.int32)
    pad_hi = pend.astype(jnp.int32)
    return dest, block_expert, n_used, pad_lo, pad_hi, n_rows


def _layer(x2, c, w_ada, b_ada, g_norm1, w_in, b_in, g_q, g_k, sinks, rel_bias, w_dw, b_dw, ln_g, ln_b,
           g_out_attn, g_out_conv, w_out, b_out, g_norm2, w_router, b_router,
           w_gate, b_gate, w_up, b_up, w_down, b_down):
    t = x2.shape[0]
    row = lambda v: v.reshape(1, -1)
    mod = _ada(c, w_ada, b_ada)
    shift1, scale1, gate1, shift2, scale2, gate2 = [mod[:, i * D_MODEL:(i + 1) * D_MODEL] for i in range(6)]

    q, kv, uc = _in_proj(x2, row(g_norm1), scale1, shift1, w_in.astype(BF16), row(b_in))

    gq_t = row(jnp.tile(g_q, N_Q_HEADS)) * (HEAD_DIM ** -0.5)
    gk_t = row(jnp.tile(g_k, N_KV_HEADS))
    ma = _attention(q, kv, _bias_table(rel_bias), sinks, gq_t, gk_t, row(g_out_attn),
                    _head_indicator(D_ATTN), _head_indicator(LANES))
    w_dw_p = jnp.concatenate([w_dw, jnp.zeros((HALO - CONV_WIDTH, D_CONV), w_dw.dtype)], axis=0)
    mc = _conv(uc, w_dw_p, row(b_dw), row(ln_g), row(ln_b), row(g_out_conv))

    xm, h2, idx, rank, wts, counts = _out_proj(
        ma, mc, x2, w_out.astype(BF16), row(b_out), gate1, row(g_norm2), scale2, shift2,
        w_router, row(b_router))

    dest, block_expert, n_used, pad_lo, pad_hi, n_rows = _routing_tables(
        idx[:, :TOP_K], rank[:, :TOP_K], counts[0], t)
    xs = _dispatch(h2, dest, pad_lo, pad_hi, n_used, n_rows)
    hs = _up(xs, block_expert, n_used, w_gate, b_gate, w_up, b_up)
    ys = _down(hs, block_expert, n_used, w_down, b_down)
    return _combine(ys, dest, xm, wts, gate2)


def kernel(x, c, w_ada, b_ada, g_norm1, w_in, b_in, g_q, g_k, sinks, rel_bias, w_dw, b_dw, ln_g, ln_b,
           g_out_attn, g_out_conv, w_out, b_out, g_norm2, w_router, b_router,
           w_gate, b_gate, w_up, b_up, w_down, b_down):
    b, t, d = x.shape
    assert b == 1 and d == D_MODEL and w_ada.shape[0] == 1
    out = _layer(x.reshape(t, d), c, w_ada[0], b_ada[0], g_norm1[0], w_in[0], b_in[0], g_q[0], g_k[0],
                 sinks[0], rel_bias, w_dw[0], b_dw[0], ln_g[0], ln_b[0], g_out_attn[0], g_out_conv[0],
                 w_out[0], b_out[0], g_norm2[0], w_router[0], b_router[0],
                 w_gate[0], b_gate[0], w_up[0], b_up[0], w_down[0], b_down[0])
    return out.reshape(b, t, d)
```

```python
import functools
import math

import jax
import jax.numpy as jnp
from jax import lax
from jax.experimental import pallas as pl
from jax.experimental.pallas import tpu as pltpu

D_MODEL = 2048
HEAD_DIM = 64
N_Q_HEADS = 16
N_KV_HEADS = 2
D_ATTN = N_Q_HEADS * HEAD_DIM
D_KV = N_KV_HEADS * HEAD_DIM
D_CONV = D_MODEL - D_ATTN
D_IN = D_ATTN + 2 * D_KV + 2 * D_CONV
WINDOW = 128
BLOCK = 128
CONV_WIDTH = 31
N_BUCKETS = 32
MAX_DISTANCE = 128
N_EXPERTS = 32
TOP_K = 4
D_FF = D_MODEL
SWIGLU_LIMIT = 7.0
SWIGLU_ALPHA = 1.702
EPS = 1e-6
NEG_INF = -1e30

LANES = 128
SUBLANES = 8
VMEM_LIMIT = 56 * 1024 * 1024

HALO = 32
EXPERT_TM = 256
F32 = jnp.float32
BF16 = jnp.bfloat16


def _params(sem):
    return pltpu.CompilerParams(dimension_semantics=sem, vmem_limit_bytes=VMEM_LIMIT)


ADA_TN = 512
ADA_RC = 256


def _ada_kernel(c_ref, w_ref, b_ref, o_ref):
    tn = w_ref.shape[1]

    def body(i, acc):
        r = pl.multiple_of(i * ADA_RC, ADA_RC)
        c = c_ref[pl.ds(r, ADA_RC), :]
        sc = c * jax.nn.sigmoid(c)
        prod = w_ref[pl.ds(r, ADA_RC), :] * sc
        return acc + jnp.sum(prod.reshape(ADA_RC // 8, 8, tn), axis=0)

    acc = lax.fori_loop(0, D_MODEL // ADA_RC, body, jnp.zeros((8, tn), F32))
    o_ref[...] = jnp.sum(acc, axis=0, keepdims=True) + b_ref[...]


def _ada(c, w_ada, b_ada):
    n = w_ada.shape[1]
    return pl.pallas_call(
        _ada_kernel,
        grid=(n // ADA_TN,),
        in_specs=[
            pl.BlockSpec((D_MODEL, 1), lambda j: (0, 0)),
            pl.BlockSpec((D_MODEL, ADA_TN), lambda j: (0, j)),
            pl.BlockSpec((1, ADA_TN), lambda j: (0, j)),
        ],
        out_specs=pl.BlockSpec((1, ADA_TN), lambda j: (0, j)),
        out_shape=jax.ShapeDtypeStruct((1, n), F32),
        compiler_params=_params(("parallel",)),
        name="ada",
    )(c.reshape(D_MODEL, 1), w_ada, b_ada.reshape(1, n))


IN_TM = 512
IN_NC = 256


def _modulated_rms(x, g, scale, shift):
    ms = jnp.mean(x * x, axis=-1, keepdims=True)
    return (x * lax.rsqrt(ms + EPS) * g) * (1.0 + scale) + shift


def _in_kernel(x_ref, g_ref, sc_ref, sh_ref, w_ref, b_ref, q_ref, kv_ref, uc_ref):
    h = _modulated_rms(x_ref[...], g_ref[...], sc_ref[...], sh_ref[...]).astype(BF16)

    def proj(lo, n, o_ref):
        for c in range(0, n, IN_NC):
            u = jnp.dot(h, w_ref[:, lo + c:lo + c + IN_NC], preferred_element_type=F32)
            o_ref[:, c:c + IN_NC] = u + b_ref[:, lo + c:lo + c + IN_NC]

    proj(0, D_ATTN, q_ref)
    proj(D_ATTN, 2 * D_KV, kv_ref)
    proj(D_ATTN + 2 * D_KV, 2 * D_CONV, uc_ref)


def _in_proj(x2, g1, scale1, shift1, w_in_bf, b_in):
    t = x2.shape[0]
    vec = lambda n: pl.BlockSpec((1, n), lambda i: (0, 0))
    return pl.pallas_call(
        _in_kernel,
        grid=(t // IN_TM,),
        in_specs=[
            pl.BlockSpec((IN_TM, D_MODEL), lambda i: (i, 0)),
            vec(D_MODEL), vec(D_MODEL), vec(D_MODEL),
            pl.BlockSpec((D_MODEL, D_IN), lambda i: (0, 0)),
            vec(D_IN),
        ],
        out_specs=[
            pl.BlockSpec((IN_TM, D_ATTN), lambda i: (i, 0)),
            pl.BlockSpec((IN_TM, 2 * D_KV), lambda i: (i, 0)),
            pl.BlockSpec((IN_TM, 2 * D_CONV), lambda i: (i, 0)),
        ],
        out_shape=[
            jax.ShapeDtypeStruct((t, D_ATTN), F32),
            jax.ShapeDtypeStruct((t, 2 * D_KV), F32),
            jax.ShapeDtypeStruct((t, 2 * D_CONV), F32),
        ],
        compiler_params=_params(("parallel",)),
        name="in_proj",
    )(x2, g1, scale1, shift1, w_in_bf, b_in)


ATT_R = 4


def _split_dot(a, b_bf):
    hi = a.astype(BF16)
    lo = (a - hi.astype(F32)).astype(BF16)
    return (jnp.dot(hi, b_bf, preferred_element_type=F32)
            + jnp.dot(lo, b_bf, preferred_element_type=F32))


def _attn_kernel(sinks_ref, q_ref, kvp_ref, kvc_ref, bias_ref, gq_ref, gk_ref, go_ref,
                 hq_ref, hk_ref, o_ref, y_ref):
    step = pl.program_id(0)
    lane = lax.broadcasted_iota(jnp.int32, (2 * BLOCK, LANES), 1)
    low = lane < HEAD_DIM
    col = lax.broadcasted_iota(jnp.int32, (BLOCK, 2 * BLOCK), 1)
    first_lo = jnp.where(step == 0, BLOCK, 0)

    for r in range(ATT_R):
        rows = slice(r * BLOCK, (r + 1) * BLOCK)
        q = q_ref[rows, :]
        ssq = _split_dot(q * q, hq_ref[...])
        qn = (q * lax.rsqrt(ssq * (1.0 / HEAD_DIM) + EPS) * gq_ref[...]).astype(BF16)

        if r == 0:
            kv_prev = kvp_ref[...]
        else:
            kv_prev = kvc_ref[(r - 1) * BLOCK:r * BLOCK, :]
        kv = jnp.concatenate([kv_prev, kvc_ref[rows, :]], axis=0)
        k = kv[:, :LANES]
        v = kv[:, LANES:]
        kss = _split_dot(k * k, hk_ref[...])
        kn = k * lax.rsqrt(kss * (1.0 / HEAD_DIM) + EPS) * gk_ref[...]
        kn_sw = pltpu.roll(kn, HEAD_DIM, axis=1)
        v_sw = pltpu.roll(v, HEAD_DIM, axis=1)
        zero = jnp.zeros_like(kn)
        k_lo = [jnp.where(low, kn, zero).astype(BF16), jnp.where(low, kn_sw, zero).astype(BF16)]
        k_hi = [jnp.where(low, zero, kn_sw).astype(BF16), jnp.where(low, zero, kn).astype(BF16)]
        v_lo = [jnp.where(low, v, zero).astype(BF16), jnp.where(low, v_sw, zero).astype(BF16)]
        v_hi = [jnp.where(low, zero, v_sw).astype(BF16), jnp.where(low, zero, v).astype(BF16)]

        for p in range(N_Q_HEADS // 2):
            g = (2 * p) // (N_Q_HEADS // N_KV_HEADS)
            qp = qn[:, p * LANES:(p + 1) * LANES]
            acc = None
            for half, (kz, vz) in enumerate(((k_lo[g], v_lo[g]), (k_hi[g], v_hi[g]))):
                h = 2 * p + half
                s = lax.dot_general(qp, kz, (((1,), (1,)), ((), ())), preferred_element_type=F32)
                s = s + bias_ref[h]
                if r == 0:
                    s = jnp.where(col >= first_lo, s, NEG_INF)
                sink = sinks_ref[h]
                m = jnp.maximum(jnp.max(s, axis=-1, keepdims=True), sink)
                e = jnp.exp(s - m)
                denom = jnp.sum(e, axis=-1, keepdims=True) + jnp.exp(sink - m)
                pv = jnp.dot(e.astype(BF16), vz, preferred_element_type=F32)
                pv = pv * (1.0 / denom)
                acc = pv if acc is None else acc + pv
            y_ref[:, p * LANES:(p + 1) * LANES] = acc

        y = y_ref[...]
        ms = jnp.mean(y * y, axis=-1, keepdims=True)
        o_ref[rows, :] = (y * lax.rsqrt(ms + EPS) * go_ref[...]).astype(BF16)


def _attention(q, kv, bias, sinks, gq_t, gk_t, g_out, hq, hk):
    t = q.shape[0]
    tile = ATT_R * BLOCK
    grid_spec = pltpu.PrefetchScalarGridSpec(
        num_scalar_prefetch=0,
        grid=(t // tile,),
        in_specs=[
            pl.BlockSpec(memory_space=pltpu.SMEM),
            pl.BlockSpec((tile, D_ATTN), lambda i: (i, 0)),
            pl.BlockSpec((BLOCK, 2 * D_KV), lambda i: (jnp.maximum(i * ATT_R - 1, 0), 0)),
            pl.BlockSpec((tile, 2 * D_KV), lambda i: (i, 0)),
            pl.BlockSpec((N_Q_HEADS, BLOCK, 2 * BLOCK), lambda i: (0, 0, 0)),
            pl.BlockSpec((1, D_ATTN), lambda i: (0, 0)),
            pl.BlockSpec((1, LANES), lambda i: (0, 0)),
            pl.BlockSpec((1, D_ATTN), lambda i: (0, 0)),
            pl.BlockSpec((D_ATTN, D_ATTN), lambda i: (0, 0)),
            pl.BlockSpec((LANES, LANES), lambda i: (0, 0)),
        ],
        out_specs=pl.BlockSpec((tile, D_ATTN), lambda i: (i, 0)),
        scratch_shapes=[pltpu.VMEM((BLOCK, D_ATTN), F32)],
    )
    return pl.pallas_call(
        _attn_kernel,
        grid_spec=grid_spec,
        out_shape=jax.ShapeDtypeStruct((t, D_ATTN), BF16),
        compiler_params=_params(("parallel",)),
        name="attn",
    )(sinks, q, kv, kv, bias, gq_t, gk_t, g_out, hq, hk)


def _t5_bucket(dist):
    max_exact = N_BUCKETS // 2
    d = jnp.maximum(dist, 0)
    log_ratio = jnp.log(jnp.maximum(d, max_exact).astype(F32) / max_exact)
    large = max_exact + (log_ratio / math.log(MAX_DISTANCE / max_exact)
                         * (N_BUCKETS - max_exact)).astype(jnp.int32)
    large = jnp.minimum(large, N_BUCKETS - 1)
    return jnp.where(d < max_exact, d, large)


def _bias_table(rel_bias):
    q_local = jnp.arange(BLOCK, dtype=jnp.int32) + BLOCK
    k_local = jnp.arange(2 * BLOCK, dtype=jnp.int32)
    dist = q_local[:, None] - k_local[None, :]
    band = (dist >= 0) & (dist < WINDOW)
    bucket = _t5_bucket(dist)
    table = rel_bias.astype(F32).T
    hit = bucket[None, :, :, None] == jnp.arange(N_BUCKETS, dtype=jnp.int32)
    bias = jnp.sum(jnp.where(hit, table[:, None, None, :], 0.0), axis=-1)
    return jnp.where(band[None], bias, NEG_INF)


def _head_indicator(n):
    i = jnp.arange(n) // HEAD_DIM
    return (i[:, None] == i[None, :]).astype(BF16)


CONV_TT = 256
CONV_RC = 64
CONV_CC = 256


def _conv_kernel(u_ref, halo_ref, w_ref, b_ref, lg_ref, lb_ref, go_ref, o_ref, h_ref, y_ref):
    step = pl.program_id(0)

    def glu(u):
        return u[:, :D_CONV] * jax.nn.sigmoid(u[:, D_CONV:])

    hh = glu(halo_ref[...])
    h_ref[0, 0:HALO, :] = jnp.where(step == 0, jnp.zeros_like(hh), hh)
    h_ref[0, HALO:, :] = glu(u_ref[...])

    n_shifted = CONV_TT + HALO - SUBLANES
    for s in range(1, SUBLANES):
        for c0 in range(0, D_CONV, CONV_CC):
            cs = slice(c0, c0 + CONV_CC)
            h_ref[s, 0:n_shifted, cs] = h_ref[0, s:s + n_shifted, cs]

    off = HALO - (CONV_WIDTH - 1)
    for r0 in range(0, CONV_TT, CONV_RC):
        for c0 in range(0, D_CONV, CONV_CC):
            cs = slice(c0, c0 + CONV_CC)
            acc = jnp.broadcast_to(b_ref[:, cs], (CONV_RC, CONV_CC))
            for j in range(CONV_WIDTH):
                s = (r0 + off + j) % SUBLANES
                a = r0 + off + j - s
                acc = acc + w_ref[j:j + 1, cs] * h_ref[s, a:a + CONV_RC, cs]
            y_ref[r0:r0 + CONV_RC, cs] = acc

    y = y_ref[...]
    mu = jnp.mean(y, axis=-1, keepdims=True)
    yc = y - mu
    var = jnp.mean(yc * yc, axis=-1, keepdims=True)
    z = yc * lax.rsqrt(var + EPS) * lg_ref[...] + lb_ref[...]
    s = z * jax.nn.sigmoid(z)
    ms = jnp.mean(s * s, axis=-1, keepdims=True)
    o_ref[...] = (s * lax.rsqrt(ms + EPS) * go_ref[...]).astype(BF16)


def _conv(uc, w_dw, b_dw, ln_g, ln_b, g_out):
    t = uc.shape[0]
    vec = lambda: pl.BlockSpec((1, D_CONV), lambda i: (0, 0))
    per = CONV_TT // HALO
    return pl.pallas_call(
        _conv_kernel,
        grid=(t // CONV_TT,),
        in_specs=[
            pl.BlockSpec((CONV_TT, 2 * D_CONV), lambda i: (i, 0)),
            pl.BlockSpec((HALO, 2 * D_CONV), lambda i: (jnp.maximum(i * per - 1, 0), 0)),
            pl.BlockSpec((HALO, D_CONV), lambda i: (0, 0)),
            vec(), vec(), vec(), vec(),
        ],
        out_specs=pl.BlockSpec((CONV_TT, D_CONV), lambda i: (i, 0)),
        out_shape=jax.ShapeDtypeStruct((t, D_CONV), BF16),
        scratch_shapes=[pltpu.VMEM((SUBLANES, CONV_TT + HALO, D_CONV), F32),
                        pltpu.VMEM((CONV_TT, D_CONV), F32)],
        compiler_params=_params(("parallel",)),
        name="conv",
    )(uc, uc, w_dw, b_dw, ln_g, ln_b, g_out)


OUT_TM = 512
D_PACK = D_MODEL // 2
HI_MASK = 0xFFFF0000


def _pack_bf16_pairs(h):
    lo = lax.bitcast_convert_type(h[:, :D_PACK].astype(BF16).astype(F32), jnp.uint32)
    hi = lax.bitcast_convert_type(h[:, D_PACK:].astype(BF16).astype(F32), jnp.uint32)
    return (lo >> 16) | (hi & jnp.uint32(HI_MASK))


def _unpack_pairs_f32(w):
    lo = lax.bitcast_convert_type(w << 16, F32)
    hi = lax.bitcast_convert_type(w & jnp.uint32(HI_MASK), F32)
    return lo, hi


def _unpack_bf16_pairs(w):
    lo, hi = _unpack_pairs_f32(w)
    return lo.astype(BF16), hi.astype(BF16)


def _out_kernel(ma_ref, mc_ref, x_ref, w_ref, bo_ref, g1_ref, g2_ref, sc_ref, sh_ref, wr_ref, br_ref,
                tri_ref, xm_ref, h2_ref, idx_ref, rank_ref, wt_ref, cnt_ref, carry_ref):
    step = pl.program_id(0)

    @pl.when(step == 0)
    def _():
        carry_ref[...] = jnp.zeros_like(carry_ref)

    y = (jnp.dot(ma_ref[...], w_ref[0:D_ATTN, :], preferred_element_type=F32)
         + jnp.dot(mc_ref[...], w_ref[D_ATTN:, :], preferred_element_type=F32) + bo_ref[...])
    xm = x_ref[...] + g1_ref[...] * y
    xm_ref[...] = xm
    h2 = _modulated_rms(xm, g2_ref[...], sc_ref[...], sh_ref[...])
    h2_ref[...] = _pack_bf16_pairs(h2)

    h_hi = h2.astype(BF16)
    h_lo = (h2 - h_hi.astype(F32)).astype(BF16)
    both = (jnp.dot(h_hi, wr_ref[...], preferred_element_type=F32)
            + jnp.dot(h_lo, wr_ref[...], preferred_element_type=F32))
    logits = both[:, :N_EXPERTS] + both[:, N_EXPERTS:] + br_ref[...]
    tm = logits.shape[0]
    lane = lax.broadcasted_iota(jnp.int32, (tm, N_EXPERTS), 1).astype(F32)
    vals, idxs = [], []
    l = logits
    for _ in range(TOP_K):
        m = jnp.max(l, axis=-1, keepdims=True)
        i = jnp.min(jnp.where(l == m, lane, float(N_EXPERTS)), axis=-1, keepdims=True)
        vals.append(m)
        idxs.append(i)
        l = jnp.where(lane == i, -jnp.inf, l)
    es = [jnp.exp(v - vals[0]) for v in vals]
    tot = es[0] + es[1] + es[2] + es[3]
    ws = [e / tot for e in es]

    hot = [(lane == i).astype(F32) for i in idxs]
    hot_all = hot[0] + hot[1] + hot[2] + hot[3]
    before = jnp.dot(tri_ref[...], hot_all.astype(BF16), preferred_element_type=F32) + carry_ref[...]
    ranks = [jnp.sum(h * before, axis=-1, keepdims=True) for h in hot]
    carry_ref[...] = carry_ref[...] + jnp.sum(hot_all, axis=0, keepdims=True)
    cnt_ref[...] = carry_ref[...].astype(jnp.int32)

    slot = lax.broadcasted_iota(jnp.int32, (tm, LANES), 1)

    def pack(cols):
        out = jnp.zeros((tm, LANES), F32)
        for k in range(TOP_K):
            out = jnp.where(slot == k, cols[k], out)
        return out

    idx_ref[...] = pack(idxs).astype(jnp.int32)
    rank_ref[...] = pack(ranks).astype(jnp.int32)
    wt_ref[...] = pack(ws)


def _out_proj(ma, mc, x2, w_out_bf, b_out, gate1, g2, scale2, shift2, w_router, b_router):
    t = x2.shape[0]
    vec = lambda n: pl.BlockSpec((1, n), lambda i: (0, 0))
    tri = jnp.tril(jnp.ones((OUT_TM, OUT_TM), F32), -1).astype(BF16)
    wr_hi = w_router.astype(BF16)
    wr_lo = (w_router - wr_hi.astype(F32)).astype(BF16)
    w_router = jnp.concatenate([wr_hi, wr_lo], axis=1)
    return pl.pallas_call(
        _out_kernel,
        grid=(t // OUT_TM,),
        in_specs=[
            pl.BlockSpec((OUT_TM, D_ATTN), lambda i: (i, 0)),
            pl.BlockSpec((OUT_TM, D_CONV), lambda i: (i, 0)),
            pl.BlockSpec((OUT_TM, D_MODEL), lambda i: (i, 0)),
            pl.BlockSpec((D_MODEL, D_MODEL), lambda i: (0, 0)),
            vec(D_MODEL), vec(D_MODEL), vec(D_MODEL), vec(D_MODEL), vec(D_MODEL),
            pl.BlockSpec((D_MODEL, 2 * N_EXPERTS), lambda i: (0, 0)),
            vec(N_EXPERTS),
            pl.BlockSpec((OUT_TM, OUT_TM), lambda i: (0, 0)),
        ],
        out_specs=[
            pl.BlockSpec((OUT_TM, D_MODEL), lambda i: (i, 0)),
            pl.BlockSpec((OUT_TM, D_PACK), lambda i: (i, 0)),
            pl.BlockSpec((OUT_TM, LANES), lambda i: (i, 0)),
            pl.BlockSpec((OUT_TM, LANES), lambda i: (i, 0)),
            pl.BlockSpec((OUT_TM, LANES), lambda i: (i, 0)),
            pl.BlockSpec((1, N_EXPERTS), lambda i: (0, 0)),
        ],
        out_shape=[
            jax.ShapeDtypeStruct((t, D_MODEL), F32),
            jax.ShapeDtypeStruct((t, D_PACK), jnp.uint32),
            jax.ShapeDtypeStruct((t, LANES), jnp.int32),
            jax.ShapeDtypeStruct((t, LANES), jnp.int32),
            jax.ShapeDtypeStruct((t, LANES), F32),
            jax.ShapeDtypeStruct((1, N_EXPERTS), jnp.int32),
        ],
        scratch_shapes=[pltpu.VMEM((1, N_EXPERTS), F32)],
        compiler_params=_params(("arbitrary",)),
        name="out_proj",
    )(ma, mc, x2, w_out_bf, b_out, gate1, g2, scale2, shift2, w_router, b_router, tri)


DISP_TD = 256
DISP_NBUF = 3


def _dispatch_kernel(dest_ref, pad_lo_ref, pad_hi_ref, nu_ref, h2_ref, xs_ref,
                     stage_ref, zero_ref, in_sems, out_sems, zsem):
    step = pl.program_id(0)
    last = pl.num_programs(0) - 1
    slot = lax.rem(step, DISP_NBUF)
    rows_per_chunk = DISP_TD * TOP_K

    def load(chunk, s):
        src = h2_ref.at[pl.ds(pl.multiple_of(chunk * DISP_TD, DISP_TD), DISP_TD), :]
        return pltpu.make_async_copy(src, stage_ref.at[s], in_sems.at[s])

    def wait_chunk(s):
        pltpu.make_async_copy(h2_ref.at[pl.ds(0, rows_per_chunk), :], xs_ref.at[pl.ds(0, rows_per_chunk), :],
                              out_sems.at[s]).wait()

    @pl.when(step == 0)
    def _():
        load(0, 0).start()

        @pl.when(last >= 1)
        def _():
            load(1, 1).start()

    load(step, slot).wait()

    def issue(r, carry):
        for k in range(TOP_K):
            d = dest_ref[(step * DISP_TD + r) * TOP_K + k]
            pltpu.make_async_copy(stage_ref.at[slot, pl.ds(r, 1), :], xs_ref.at[pl.ds(d, 1), :],
                                  out_sems.at[slot]).start(priority=k % 2)
        return carry

    lax.fori_loop(0, DISP_TD, issue, 0, unroll=2)

    @pl.when(step == 0)
    def _():
        zero_ref[...] = jnp.zeros_like(zero_ref)

        def zero_row(d):
            return pltpu.make_async_copy(zero_ref.at[pl.ds(0, 1), :], xs_ref.at[pl.ds(d, 1), :], zsem)

        def zero_block(b):
            d = pl.multiple_of(b * EXPERT_TM, EXPERT_TM)
            return pltpu.make_async_copy(zero_ref, xs_ref.at[pl.ds(d, EXPERT_TM), :], zsem)

        def start_row(d, carry):
            zero_row(d).start()
            return carry

        def wait_row(d, carry):
            zero_row(d).wait()
            return carry

        def start_block(b, carry):
            zero_block(b).start()
            return carry

        def wait_block(b, carry):
            zero_block(b).wait()
            return carry

        def per_expert(e, carry):
            lax.fori_loop(pad_lo_ref[e], pad_hi_ref[e], start_row, 0)
            lax.fori_loop(pad_lo_ref[e], pad_hi_ref[e], wait_row, 0)
            return carry

        lax.fori_loop(0, N_EXPERTS, per_expert, 0)
        n_blocks = xs_ref.shape[0] // EXPERT_TM
        lax.fori_loop(nu_ref[0], n_blocks, start_block, 0)
        lax.fori_loop(nu_ref[0], n_blocks, wait_block, 0)

    @pl.when(step > 0)
    def _():
        wait_chunk(lax.rem(step + DISP_NBUF - 1, DISP_NBUF))

    @pl.when(step + 2 <= last)
    def _():
        load(step + 2, lax.rem(step + 2, DISP_NBUF)).start()

    @pl.when(step == last)
    def _():
        wait_chunk(slot)


def _dispatch(h2p, dest_flat, pad_lo, pad_hi, n_used, n_rows):
    t = h2p.shape[0]
    grid_spec = pltpu.PrefetchScalarGridSpec(
        num_scalar_prefetch=4,
        grid=(t // DISP_TD,),
        in_specs=[pl.BlockSpec(memory_space=pl.ANY)],
        out_specs=pl.BlockSpec(memory_space=pl.ANY),
        scratch_shapes=[pltpu.VMEM((DISP_NBUF, DISP_TD, D_PACK), jnp.uint32),
                        pltpu.VMEM((EXPERT_TM, D_PACK), jnp.uint32),
                        pltpu.SemaphoreType.DMA((DISP_NBUF,)), pltpu.SemaphoreType.DMA((DISP_NBUF,)),
                        pltpu.SemaphoreType.DMA(())],
    )
    return pl.pallas_call(
        _dispatch_kernel,
        grid_spec=grid_spec,
        out_shape=jax.ShapeDtypeStruct((n_rows, D_PACK), jnp.uint32),
        compiler_params=_params(("arbitrary",)),
        name="dispatch",
    )(dest_flat, pad_lo, pad_hi, n_used, h2p)


UP_TF = 1024
DOWN_TN = 2048


def _weight_index_map(n_chunks):
    def index_map(c, b, be, nu, first, nxt):
        del nu
        in_last = nxt[b] < 0
        wrap = jnp.logical_and(in_last, c + 1 < n_chunks)
        e_next = jnp.where(in_last, jnp.where(wrap, be[0], be[b]), nxt[b])
        c_next = jnp.where(wrap, c + 1, c)
        is_first = first[b] == 1
        return jnp.where(is_first, be[b], e_next), 0, jnp.where(is_first, c, c_next)

    return index_map


def _up_kernel(be_ref, nu_ref, first_ref, nxt_ref, x_ref, wg_ref, wu_ref, bg_ref, bu_ref, h_ref, wg_bf, wu_bf):
    b = pl.program_id(1)

    @pl.when(first_ref[b] == 1)
    def _():
        wg_bf[...] = wg_ref[0].astype(BF16)
        wu_bf[...] = wu_ref[0].astype(BF16)

    @pl.when(b < nu_ref[0])
    def _():
        x_lo, x_hi = _unpack_bf16_pairs(x_ref[...])

        def proj(w_bf, bias_ref):
            return (jnp.dot(x_lo, w_bf[0:D_PACK, :], preferred_element_type=F32)
                    + jnp.dot(x_hi, w_bf[D_PACK:, :], preferred_element_type=F32) + bias_ref[0])

        g = proj(wg_bf, bg_ref)
        lin = proj(wu_bf, bu_ref)
        g = jnp.minimum(g, SWIGLU_LIMIT)
        lin = jnp.clip(lin, -SWIGLU_LIMIT, SWIGLU_LIMIT)
        act = g * jax.nn.sigmoid(SWIGLU_ALPHA * g) * (lin + 1.0)
        h_ref[...] = act.astype(BF16)

    @pl.when(b >= nu_ref[0])
    def _():
        h_ref[...] = jnp.zeros_like(h_ref)


def _up(xs, tables, w_gate, b_gate, w_up, b_up):
    n_rows = xs.shape[0]
    nb = n_rows // EXPERT_TM
    n_chunks = D_FF // UP_TF
    row = lambda f, b, be, nu, first, nxt: (jnp.minimum(b, nu[0] - 1), 0)
    wsel = _weight_index_map(n_chunks)
    bsel = lambda f, b, be, nu, first, nxt: (be[b], 0, f)
    grid_spec = pltpu.PrefetchScalarGridSpec(
        num_scalar_prefetch=4,
        grid=(n_chunks, nb),
        in_specs=[
            pl.BlockSpec((EXPERT_TM, D_PACK), row),
            pl.BlockSpec((1, D_MODEL, UP_TF), wsel),
            pl.BlockSpec((1, D_MODEL, UP_TF), wsel),
            pl.BlockSpec((1, 1, UP_TF), bsel),
            pl.BlockSpec((1, 1, UP_TF), bsel),
        ],
        out_specs=pl.BlockSpec((EXPERT_TM, UP_TF), lambda f, b, be, nu, first, nxt: (b, f)),
        scratch_shapes=[pltpu.VMEM((D_MODEL, UP_TF), BF16), pltpu.VMEM((D_MODEL, UP_TF), BF16)],
    )
    return pl.pallas_call(
        _up_kernel,
        grid_spec=grid_spec,
        out_shape=jax.ShapeDtypeStruct((n_rows, D_FF), BF16),
        compiler_params=_params(("arbitrary", "arbitrary")),
        name="expert_up",
    )(*tables, xs, w_gate, w_up,
      b_gate.reshape(N_EXPERTS, 1, D_FF), b_up.reshape(N_EXPERTS, 1, D_FF))


def _down_kernel(be_ref, nu_ref, first_ref, nxt_ref, h_ref, wd_ref, bd_ref, y_ref, wd_bf):
    b = pl.program_id(1)

    @pl.when(first_ref[b] == 1)
    def _():
        wd_bf[...] = wd_ref[0].astype(BF16)

    @pl.when(b < nu_ref[0])
    def _():
        y_ref[...] = _pack_bf16_pairs(jnp.dot(h_ref[...], wd_bf[...], preferred_element_type=F32) + bd_ref[0])

    @pl.when(b >= nu_ref[0])
    def _():
        y_ref[...] = jnp.zeros_like(y_ref)


def _down(hs, tables, w_down, b_down):
    n_rows = hs.shape[0]
    nb = n_rows // EXPERT_TM
    n_chunks = D_MODEL // DOWN_TN
    assert n_chunks == 1
    row = lambda n, b, be, nu, first, nxt: (jnp.minimum(b, nu[0] - 1), 0)
    grid_spec = pltpu.PrefetchScalarGridSpec(
        num_scalar_prefetch=4,
        grid=(n_chunks, nb),
        in_specs=[
            pl.BlockSpec((EXPERT_TM, D_FF), row),
            pl.BlockSpec((1, D_FF, DOWN_TN), _weight_index_map(n_chunks)),
            pl.BlockSpec((1, 1, DOWN_TN), lambda n, b, be, nu, first, nxt: (be[b], 0, n)),
        ],
        out_specs=pl.BlockSpec((EXPERT_TM, D_PACK), lambda n, b, be, nu, first, nxt: (b, 0)),
        scratch_shapes=[pltpu.VMEM((D_FF, DOWN_TN), BF16)],
    )
    return pl.pallas_call(
        _down_kernel,
        grid_spec=grid_spec,
        out_shape=jax.ShapeDtypeStruct((n_rows, D_PACK), jnp.uint32),
        compiler_params=_params(("arbitrary", "arbitrary")),
        name="expert_down",
    )(*tables, hs, w_down, b_down.reshape(N_EXPERTS, 1, D_MODEL))


COMB_TC = 128


def _combine_kernel(dest_ref, xm_ref, wt_ref, g2_ref, ys_ref, o_ref, buf_ref, sems):
    step = pl.program_id(0)
    last = pl.num_programs(0) - 1
    slot = step % 2

    def gather_tile(tile, s):
        def issue(r, carry):
            for k in range(TOP_K):
                d = dest_ref[(tile * COMB_TC + r) * TOP_K + k]
                pltpu.make_async_copy(ys_ref.at[pl.ds(d, 1), :], buf_ref.at[s, pl.ds(k * COMB_TC + r, 1), :],
                                      sems.at[s]).start(priority=k % 2)
            return carry

        lax.fori_loop(0, COMB_TC, issue, 0, unroll=2)

    @pl.when(step == 0)
    def _():
        gather_tile(0, 0)

    @pl.when(step < last)
    def _():
        gather_tile(step + 1, 1 - slot)

    pltpu.make_async_copy(ys_ref.at[pl.ds(0, TOP_K * COMB_TC), :], buf_ref.at[slot], sems.at[slot]).wait()

    wt = wt_ref[...]
    acc_lo = acc_hi = None
    for k in range(TOP_K):
        lo, hi = _unpack_pairs_f32(buf_ref[slot, k * COMB_TC:(k + 1) * COMB_TC, :])
        w = wt[:, k:k + 1]
        acc_lo = w * lo if acc_lo is None else acc_lo + w * lo
        acc_hi = w * hi if acc_hi is None else acc_hi + w * hi
    o_ref[:, :D_PACK] = xm_ref[:, :D_PACK] + g2_ref[:, :D_PACK] * acc_lo
    o_ref[:, D_PACK:] = xm_ref[:, D_PACK:] + g2_ref[:, D_PACK:] * acc_hi


def _combine(ys, dest_flat, xm, wts, gate2):
    t = xm.shape[0]
    grid_spec = pltpu.PrefetchScalarGridSpec(
        num_scalar_prefetch=1,
        grid=(t // COMB_TC,),
        in_specs=[
            pl.BlockSpec((COMB_TC, D_MODEL), lambda i, d: (i, 0)),
            pl.BlockSpec((COMB_TC, LANES), lambda i, d: (i, 0)),
            pl.BlockSpec((1, D_MODEL), lambda i, d: (0, 0)),
            pl.BlockSpec(memory_space=pl.ANY),
        ],
        out_specs=pl.BlockSpec((COMB_TC, D_MODEL), lambda i, d: (i, 0)),
        scratch_shapes=[pltpu.VMEM((2, TOP_K * COMB_TC, D_PACK), jnp.uint32), pltpu.SemaphoreType.DMA((2,))],
    )
    return pl.pallas_call(
        _combine_kernel,
        grid_spec=grid_spec,
        out_shape=jax.ShapeDtypeStruct((t, D_MODEL), F32),
        compiler_params=_params(("arbitrary",)),
        name="combine",
    )(dest_flat, xm, wts, gate2, ys)


def _routing_tables(idx, rank, counts, n_tok):
    padded = ((counts + EXPERT_TM - 1) // EXPERT_TM) * EXPERT_TM
    pend = jnp.cumsum(padded)
    pstart = pend - padded
    experts = jnp.arange(N_EXPERTS, dtype=jnp.int32)
    start_of = jnp.sum(jnp.where(idx[..., None] == experts, pstart, 0), axis=-1)
    dest = (start_of + rank).astype(jnp.int32).reshape(-1)
    n_rows = n_tok * TOP_K + N_EXPERTS * EXPERT_TM
    nb = n_rows // EXPERT_TM
    block_start = jnp.arange(nb, dtype=jnp.int32) * EXPERT_TM
    block_expert = jnp.minimum(jnp.sum(pend[None, :] <= block_start[:, None], axis=1),
                               N_EXPERTS - 1).astype(jnp.int32)
    n_used = (pend[-1:] // EXPERT_TM).astype(jnp.int32)
    pad_lo = (pstart + counts).astype(jnp.int32)
    pad_hi = pend.astype(jnp.int32)

    blocks = jnp.arange(nb, dtype=jnp.int32)
    block_expert = jnp.where(blocks < n_used[0], block_expert, block_expert[jnp.maximum(n_used[0] - 1, 0)])
    prev = jnp.concatenate([block_expert[:1], block_expert[:-1]])
    first = jnp.logical_or(blocks == 0, block_expert != prev)
    first_pos = jnp.where(first, blocks, nb)
    next_first = jnp.flip(lax.cummin(jnp.flip(jnp.concatenate([first_pos[1:], jnp.full((1,), nb, jnp.int32)]))))
    nxt = jnp.where(next_first < nb, block_expert[jnp.minimum(next_first, nb - 1)], -1).astype(jnp.int32)
    tables = (block_expert, n_used, first.astype(jnp.int32), nxt)
    return dest, tables, pad_lo, pad_hi, n_rows


def _layer(x2, c, w_ada, b_ada, g_norm1, w_in, b_in, g_q, g_k, sinks, rel_bias, w_dw, b_dw, ln_g, ln_b,
           g_out_attn, g_out_conv, w_out, b_out, g_norm2, w_router, b_router,
           w_gate, b_gate, w_up, b_up, w_down, b_down):
    t = x2.shape[0]
    row = lambda v: v.reshape(1, -1)
    mod = _ada(c, w_ada, b_ada)
    shift1, scale1, gate1, shift2, scale2, gate2 = [mod[:, i * D_MODEL:(i + 1) * D_MODEL] for i in range(6)]

    q, kv, uc = _in_proj(x2, row(g_norm1), scale1, shift1, w_in.astype(BF16), row(b_in))

    gq_t = row(jnp.tile(g_q, N_Q_HEADS)) * (HEAD_DIM ** -0.5)
    gk_t = row(jnp.tile(g_k, N_KV_HEADS))
    ma = _attention(q, kv, _bias_table(rel_bias), sinks, gq_t, gk_t, row(g_out_attn),
                    _head_indicator(D_ATTN), _head_indicator(LANES))
    w_dw_p = jnp.concatenate([w_dw, jnp.zeros((HALO - CONV_WIDTH, D_CONV), w_dw.dtype)], axis=0)
    mc = _conv(uc, w_dw_p, row(b_dw), row(ln_g), row(ln_b), row(g_out_conv))

    xm, h2, idx, rank, wts, counts = _out_proj(
        ma, mc, x2, w_out.astype(BF16), row(b_out), gate1, row(g_norm2), scale2, shift2,
        w_router, row(b_router))

    dest, tables, pad_lo, pad_hi, n_rows = _routing_tables(idx[:, :TOP_K], rank[:, :TOP_K], counts[0], t)
    xs = _dispatch(h2, dest, pad_lo, pad_hi, tables[1], n_rows)
    hs = _up(xs, tables, w_gate, b_gate, w_up, b_up)
    ys = _down(hs, tables, w_down, b_down)
    return _combine(ys, dest, xm, wts, gate2)


def kernel(x, c, w_ada, b_ada, g_norm1, w_in, b_in, g_q, g_k, sinks, rel_bias, w_dw, b_dw, ln_g, ln_b,
           g_out_attn, g_out_conv, w_out, b_out, g_norm2, w_router, b_router,
           w_gate, b_gate, w_up, b_up, w_down, b_down):
    b, t, d = x.shape
    assert b == 1 and d == D_MODEL and w_ada.shape[0] == 1
    out = _layer(x.reshape(t, d), c, w_ada[0], b_ada[0], g_norm1[0], w_in[0], b_in[0], g_q[0], g_k[0],
                 sinks[0], rel_bias, w_dw[0], b_dw[0], ln_g[0], ln_b[0], g_out_attn[0], g_out_conv[0],
                 w_out[0], b_out[0], g_norm2[0], w_router[0], b_router[0],
                 w_gate[0], b_gate[0], w_up[0], b_up[0], w_down[0], b_down[0])
    return out.reshape(b, t, d)
```

```python
import functools
import math

import jax
import jax.numpy as jnp
from jax import lax
from jax.experimental import pallas as pl
from jax.experimental.pallas import tpu as pltpu

D_MODEL = 2048
HEAD_DIM = 64
N_Q_HEADS = 16
N_KV_HEADS = 2
D_ATTN = N_Q_HEADS * HEAD_DIM
D_KV = N_KV_HEADS * HEAD_DIM
D_CONV = D_MODEL - D_ATTN
D_IN = D_ATTN + 2 * D_KV + 2 * D_CONV
WINDOW = 128
BLOCK = 128
CONV_WIDTH = 31
N_BUCKETS = 32
MAX_DISTANCE = 128
N_EXPERTS = 32
TOP_K = 4
D_FF = D_MODEL
SWIGLU_LIMIT = 7.0
SWIGLU_ALPHA = 1.702
EPS = 1e-6
NEG_INF = -1e30

LANES = 128
SUBLANES = 8
VMEM_LIMIT = 56 * 1024 * 1024

HALO = 32
EXPERT_TM = 512
EXPERT_VMEM_LIMIT = 60 * 1024 * 1024
F32 = jnp.float32
BF16 = jnp.bfloat16


def _params(sem, vmem_limit=VMEM_LIMIT):
    return pltpu.CompilerParams(dimension_semantics=sem, vmem_limit_bytes=vmem_limit)


ADA_TN = 512
ADA_RC = 256


def _ada_kernel(c_ref, w_ref, b_ref, o_ref):
    tn = w_ref.shape[1]

    def body(i, acc):
        r = pl.multiple_of(i * ADA_RC, ADA_RC)
        c = c_ref[pl.ds(r, ADA_RC), :]
        sc = c * jax.nn.sigmoid(c)
        prod = w_ref[pl.ds(r, ADA_RC), :] * sc
        return acc + jnp.sum(prod.reshape(ADA_RC // 8, 8, tn), axis=0)

    acc = lax.fori_loop(0, D_MODEL // ADA_RC, body, jnp.zeros((8, tn), F32))
    o_ref[...] = jnp.sum(acc, axis=0, keepdims=True) + b_ref[...]


def _ada(c, w_ada, b_ada):
    n = w_ada.shape[1]
    return pl.pallas_call(
        _ada_kernel,
        grid=(n // ADA_TN,),
        in_specs=[
            pl.BlockSpec((D_MODEL, 1), lambda j: (0, 0)),
            pl.BlockSpec((D_MODEL, ADA_TN), lambda j: (0, j)),
            pl.BlockSpec((1, ADA_TN), lambda j: (0, j)),
        ],
        out_specs=pl.BlockSpec((1, ADA_TN), lambda j: (0, j)),
        out_shape=jax.ShapeDtypeStruct((1, n), F32),
        compiler_params=_params(("parallel",)),
        name="ada",
    )(c.reshape(D_MODEL, 1), w_ada, b_ada.reshape(1, n))


IN_TM = 512
IN_NC = 256


def _modulated_rms(x, g, scale, shift):
    ms = jnp.mean(x * x, axis=-1, keepdims=True)
    return (x * lax.rsqrt(ms + EPS) * g) * (1.0 + scale) + shift


def _in_kernel(x_ref, g_ref, sc_ref, sh_ref, w_ref, b_ref, q_ref, kv_ref, uc_ref):
    h = _modulated_rms(x_ref[...], g_ref[...], sc_ref[...], sh_ref[...]).astype(BF16)

    def proj(lo, n, o_ref):
        for c in range(0, n, IN_NC):
            u = jnp.dot(h, w_ref[:, lo + c:lo + c + IN_NC], preferred_element_type=F32)
            o_ref[:, c:c + IN_NC] = u + b_ref[:, lo + c:lo + c + IN_NC]

    proj(0, D_ATTN, q_ref)
    proj(D_ATTN, 2 * D_KV, kv_ref)
    proj(D_ATTN + 2 * D_KV, 2 * D_CONV, uc_ref)


def _in_proj(x2, g1, scale1, shift1, w_in_bf, b_in):
    t = x2.shape[0]
    vec = lambda n: pl.BlockSpec((1, n), lambda i: (0, 0))
    return pl.pallas_call(
        _in_kernel,
        grid=(t // IN_TM,),
        in_specs=[
            pl.BlockSpec((IN_TM, D_MODEL), lambda i: (i, 0)),
            vec(D_MODEL), vec(D_MODEL), vec(D_MODEL),
            pl.BlockSpec((D_MODEL, D_IN), lambda i: (0, 0)),
            vec(D_IN),
        ],
        out_specs=[
            pl.BlockSpec((IN_TM, D_ATTN), lambda i: (i, 0)),
            pl.BlockSpec((IN_TM, 2 * D_KV), lambda i: (i, 0)),
            pl.BlockSpec((IN_TM, 2 * D_CONV), lambda i: (i, 0)),
        ],
        out_shape=[
            jax.ShapeDtypeStruct((t, D_ATTN), F32),
            jax.ShapeDtypeStruct((t, 2 * D_KV), F32),
            jax.ShapeDtypeStruct((t, 2 * D_CONV), F32),
        ],
        compiler_params=_params(("parallel",)),
        name="in_proj",
    )(x2, g1, scale1, shift1, w_in_bf, b_in)


ATT_R = 4


def _split_dot(a, b_bf):
    hi = a.astype(BF16)
    lo = (a - hi.astype(F32)).astype(BF16)
    return (jnp.dot(hi, b_bf, preferred_element_type=F32)
            + jnp.dot(lo, b_bf, preferred_element_type=F32))


def _attn_kernel(sinks_ref, q_ref, kvp_ref, kvc_ref, bias_ref, gq_ref, gk_ref, go_ref,
                 hq_ref, hk_ref, o_ref, y_ref):
    step = pl.program_id(0)
    lane = lax.broadcasted_iota(jnp.int32, (2 * BLOCK, LANES), 1)
    low = lane < HEAD_DIM
    col = lax.broadcasted_iota(jnp.int32, (BLOCK, 2 * BLOCK), 1)
    first_lo = jnp.where(step == 0, BLOCK, 0)

    for r in range(ATT_R):
        rows = slice(r * BLOCK, (r + 1) * BLOCK)
        q = q_ref[rows, :]
        ssq = _split_dot(q * q, hq_ref[...])
        qn = (q * lax.rsqrt(ssq * (1.0 / HEAD_DIM) + EPS) * gq_ref[...]).astype(BF16)

        if r == 0:
            kv_prev = kvp_ref[...]
        else:
            kv_prev = kvc_ref[(r - 1) * BLOCK:r * BLOCK, :]
        kv = jnp.concatenate([kv_prev, kvc_ref[rows, :]], axis=0)
        k = kv[:, :LANES]
        v = kv[:, LANES:]
        kss = _split_dot(k * k, hk_ref[...])
        kn = k * lax.rsqrt(kss * (1.0 / HEAD_DIM) + EPS) * gk_ref[...]
        kn_sw = pltpu.roll(kn, HEAD_DIM, axis=1)
        v_sw = pltpu.roll(v, HEAD_DIM, axis=1)
        zero = jnp.zeros_like(kn)
        k_lo = [jnp.where(low, kn, zero).astype(BF16), jnp.where(low, kn_sw, zero).astype(BF16)]
        k_hi = [jnp.where(low, zero, kn_sw).astype(BF16), jnp.where(low, zero, kn).astype(BF16)]
        v_lo = [jnp.where(low, v, zero).astype(BF16), jnp.where(low, v_sw, zero).astype(BF16)]
        v_hi = [jnp.where(low, zero, v_sw).astype(BF16), jnp.where(low, zero, v).astype(BF16)]

        for p in range(N_Q_HEADS // 2):
            g = (2 * p) // (N_Q_HEADS // N_KV_HEADS)
            qp = qn[:, p * LANES:(p + 1) * LANES]
            acc = None
            for half, (kz, vz) in enumerate(((k_lo[g], v_lo[g]), (k_hi[g], v_hi[g]))):
                h = 2 * p + half
                s = lax.dot_general(qp, kz, (((1,), (1,)), ((), ())), preferred_element_type=F32)
                s = s + bias_ref[h]
                if r == 0:
                    s = jnp.where(col >= first_lo, s, NEG_INF)
                sink = sinks_ref[h]
                m = jnp.maximum(jnp.max(s, axis=-1, keepdims=True), sink)
                e = jnp.exp(s - m)
                denom = jnp.sum(e, axis=-1, keepdims=True) + jnp.exp(sink - m)
                pv = jnp.dot(e.astype(BF16), vz, preferred_element_type=F32)
                pv = pv * (1.0 / denom)
                acc = pv if acc is None else acc + pv
            y_ref[:, p * LANES:(p + 1) * LANES] = acc

        y = y_ref[...]
        ms = jnp.mean(y * y, axis=-1, keepdims=True)
        o_ref[rows, :] = (y * lax.rsqrt(ms + EPS) * go_ref[...]).astype(BF16)


def _attention(q, kv, bias, sinks, gq_t, gk_t, g_out, hq, hk):
    t = q.shape[0]
    tile = ATT_R * BLOCK
    grid_spec = pltpu.PrefetchScalarGridSpec(
        num_scalar_prefetch=0,
        grid=(t // tile,),
        in_specs=[
            pl.BlockSpec(memory_space=pltpu.SMEM),
            pl.BlockSpec((tile, D_ATTN), lambda i: (i, 0)),
            pl.BlockSpec((BLOCK, 2 * D_KV), lambda i: (jnp.maximum(i * ATT_R - 1, 0), 0)),
            pl.BlockSpec((tile, 2 * D_KV), lambda i: (i, 0)),
            pl.BlockSpec((N_Q_HEADS, BLOCK, 2 * BLOCK), lambda i: (0, 0, 0)),
            pl.BlockSpec((1, D_ATTN), lambda i: (0, 0)),
            pl.BlockSpec((1, LANES), lambda i: (0, 0)),
            pl.BlockSpec((1, D_ATTN), lambda i: (0, 0)),
            pl.BlockSpec((D_ATTN, D_ATTN), lambda i: (0, 0)),
            pl.BlockSpec((LANES, LANES), lambda i: (0, 0)),
        ],
        out_specs=pl.BlockSpec((tile, D_ATTN), lambda i: (i, 0)),
        scratch_shapes=[pltpu.VMEM((BLOCK, D_ATTN), F32)],
    )
    return pl.pallas_call(
        _attn_kernel,
        grid_spec=grid_spec,
        out_shape=jax.ShapeDtypeStruct((t, D_ATTN), BF16),
        compiler_params=_params(("parallel",)),
        name="attn",
    )(sinks, q, kv, kv, bias, gq_t, gk_t, g_out, hq, hk)


def _t5_bucket(dist):
    max_exact = N_BUCKETS // 2
    d = jnp.maximum(dist, 0)
    log_ratio = jnp.log(jnp.maximum(d, max_exact).astype(F32) / max_exact)
    large = max_exact + (log_ratio / math.log(MAX_DISTANCE / max_exact)
                         * (N_BUCKETS - max_exact)).astype(jnp.int32)
    large = jnp.minimum(large, N_BUCKETS - 1)
    return jnp.where(d < max_exact, d, large)


def _bias_table(rel_bias):
    q_local = jnp.arange(BLOCK, dtype=jnp.int32) + BLOCK
    k_local = jnp.arange(2 * BLOCK, dtype=jnp.int32)
    dist = q_local[:, None] - k_local[None, :]
    band = (dist >= 0) & (dist < WINDOW)
    bucket = _t5_bucket(dist)
    table = rel_bias.astype(F32).T
    hit = bucket[None, :, :, None] == jnp.arange(N_BUCKETS, dtype=jnp.int32)
    bias = jnp.sum(jnp.where(hit, table[:, None, None, :], 0.0), axis=-1)
    return jnp.where(band[None], bias, NEG_INF)


def _head_indicator(n):
    i = jnp.arange(n) // HEAD_DIM
    return (i[:, None] == i[None, :]).astype(BF16)


CONV_TT = 256
CONV_RC = 64
CONV_CC = 256


def _conv_kernel(u_ref, halo_ref, w_ref, b_ref, lg_ref, lb_ref, go_ref, o_ref, h_ref, y_ref):
    step = pl.program_id(0)

    def glu(u):
        return u[:, :D_CONV] * jax.nn.sigmoid(u[:, D_CONV:])

    hh = glu(halo_ref[...])
    h_ref[0, 0:HALO, :] = jnp.where(step == 0, jnp.zeros_like(hh), hh)
    h_ref[0, HALO:, :] = glu(u_ref[...])

    n_shifted = CONV_TT + HALO - SUBLANES
    for s in range(1, SUBLANES):
        for c0 in range(0, D_CONV, CONV_CC):
            cs = slice(c0, c0 + CONV_CC)
            h_ref[s, 0:n_shifted, cs] = h_ref[0, s:s + n_shifted, cs]

    off = HALO - (CONV_WIDTH - 1)
    for r0 in range(0, CONV_TT, CONV_RC):
        for c0 in range(0, D_CONV, CONV_CC):
            cs = slice(c0, c0 + CONV_CC)
            acc = jnp.broadcast_to(b_ref[:, cs], (CONV_RC, CONV_CC))
            for j in range(CONV_WIDTH):
                s = (r0 + off + j) % SUBLANES
                a = r0 + off + j - s
                acc = acc + w_ref[j:j + 1, cs] * h_ref[s, a:a + CONV_RC, cs]
            y_ref[r0:r0 + CONV_RC, cs] = acc

    y = y_ref[...]
    mu = jnp.mean(y, axis=-1, keepdims=True)
    yc = y - mu
    var = jnp.mean(yc * yc, axis=-1, keepdims=True)
    z = yc * lax.rsqrt(var + EPS) * lg_ref[...] + lb_ref[...]
    s = z * jax.nn.sigmoid(z)
    ms = jnp.mean(s * s, axis=-1, keepdims=True)
    o_ref[...] = (s * lax.rsqrt(ms + EPS) * go_ref[...]).astype(BF16)


def _conv(uc, w_dw, b_dw, ln_g, ln_b, g_out):
    t = uc.shape[0]
    vec = lambda: pl.BlockSpec((1, D_CONV), lambda i: (0, 0))
    per = CONV_TT // HALO
    return pl.pallas_call(
        _conv_kernel,
        grid=(t // CONV_TT,),
        in_specs=[
            pl.BlockSpec((CONV_TT, 2 * D_CONV), lambda i: (i, 0)),
            pl.BlockSpec((HALO, 2 * D_CONV), lambda i: (jnp.maximum(i * per - 1, 0), 0)),
            pl.BlockSpec((HALO, D_CONV), lambda i: (0, 0)),
            vec(), vec(), vec(), vec(),
        ],
        out_specs=pl.BlockSpec((CONV_TT, D_CONV), lambda i: (i, 0)),
        out_shape=jax.ShapeDtypeStruct((t, D_CONV), BF16),
        scratch_shapes=[pltpu.VMEM((SUBLANES, CONV_TT + HALO, D_CONV), F32),
                        pltpu.VMEM((CONV_TT, D_CONV), F32)],
        compiler_params=_params(("parallel",)),
        name="conv",
    )(uc, uc, w_dw, b_dw, ln_g, ln_b, g_out)


OUT_TM = 512
D_PACK = D_MODEL // 2
HI_MASK = 0xFFFF0000


def _pack_bf16_pairs(h):
    lo = lax.bitcast_convert_type(h[:, :D_PACK].astype(BF16).astype(F32), jnp.uint32)
    hi = lax.bitcast_convert_type(h[:, D_PACK:].astype(BF16).astype(F32), jnp.uint32)
    return (lo >> 16) | (hi & jnp.uint32(HI_MASK))


def _unpack_pairs_f32(w):
    lo = lax.bitcast_convert_type(w << 16, F32)
    hi = lax.bitcast_convert_type(w & jnp.uint32(HI_MASK), F32)
    return lo, hi


def _unpack_bf16_pairs(w):
    lo, hi = _unpack_pairs_f32(w)
    return lo.astype(BF16), hi.astype(BF16)


def _out_kernel(ma_ref, mc_ref, x_ref, w_ref, bo_ref, g1_ref, g2_ref, sc_ref, sh_ref, wr_ref, br_ref,
                tri_ref, xm_ref, h2_ref, idx_ref, rank_ref, wt_ref, cnt_ref, carry_ref):
    step = pl.program_id(0)

    @pl.when(step == 0)
    def _():
        carry_ref[...] = jnp.zeros_like(carry_ref)

    y = (jnp.dot(ma_ref[...], w_ref[0:D_ATTN, :], preferred_element_type=F32)
         + jnp.dot(mc_ref[...], w_ref[D_ATTN:, :], preferred_element_type=F32) + bo_ref[...])
    xm = x_ref[...] + g1_ref[...] * y
    xm_ref[...] = xm
    h2 = _modulated_rms(xm, g2_ref[...], sc_ref[...], sh_ref[...])
    h2_ref[...] = _pack_bf16_pairs(h2)

    h_hi = h2.astype(BF16)
    h_lo = (h2 - h_hi.astype(F32)).astype(BF16)
    both = (jnp.dot(h_hi, wr_ref[...], preferred_element_type=F32)
            + jnp.dot(h_lo, wr_ref[...], preferred_element_type=F32))
    logits = both[:, :N_EXPERTS] + both[:, N_EXPERTS:] + br_ref[...]
    tm = logits.shape[0]
    lane = lax.broadcasted_iota(jnp.int32, (tm, N_EXPERTS), 1).astype(F32)
    vals, idxs = [], []
    l = logits
    for _ in range(TOP_K):
        m = jnp.max(l, axis=-1, keepdims=True)
        i = jnp.min(jnp.where(l == m, lane, float(N_EXPERTS)), axis=-1, keepdims=True)
        vals.append(m)
        idxs.append(i)
        l = jnp.where(lane == i, -jnp.inf, l)
    es = [jnp.exp(v - vals[0]) for v in vals]
    tot = es[0] + es[1] + es[2] + es[3]
    ws = [e / tot for e in es]

    hot = [(lane == i).astype(F32) for i in idxs]
    hot_all = hot[0] + hot[1] + hot[2] + hot[3]
    before = jnp.dot(tri_ref[...], hot_all.astype(BF16), preferred_element_type=F32) + carry_ref[...]
    ranks = [jnp.sum(h * before, axis=-1, keepdims=True) for h in hot]
    carry_ref[...] = carry_ref[...] + jnp.sum(hot_all, axis=0, keepdims=True)
    cnt_ref[...] = carry_ref[...].astype(jnp.int32)

    slot = lax.broadcasted_iota(jnp.int32, (tm, LANES), 1)

    def pack(cols):
        out = jnp.zeros((tm, LANES), F32)
        for k in range(TOP_K):
            out = jnp.where(slot == k, cols[k], out)
        return out

    idx_ref[...] = pack(idxs).astype(jnp.int32)
    rank_ref[...] = pack(ranks).astype(jnp.int32)
    wt_ref[...] = pack(ws)


def _out_proj(ma, mc, x2, w_out_bf, b_out, gate1, g2, scale2, shift2, w_router, b_router):
    t = x2.shape[0]
    vec = lambda n: pl.BlockSpec((1, n), lambda i: (0, 0))
    tri = jnp.tril(jnp.ones((OUT_TM, OUT_TM), F32), -1).astype(BF16)
    wr_hi = w_router.astype(BF16)
    wr_lo = (w_router - wr_hi.astype(F32)).astype(BF16)
    w_router = jnp.concatenate([wr_hi, wr_lo], axis=1)
    return pl.pallas_call(
        _out_kernel,
        grid=(t // OUT_TM,),
        in_specs=[
            pl.BlockSpec((OUT_TM, D_ATTN), lambda i: (i, 0)),
            pl.BlockSpec((OUT_TM, D_CONV), lambda i: (i, 0)),
            pl.BlockSpec((OUT_TM, D_MODEL), lambda i: (i, 0)),
            pl.BlockSpec((D_MODEL, D_MODEL), lambda i: (0, 0)),
            vec(D_MODEL), vec(D_MODEL), vec(D_MODEL), vec(D_MODEL), vec(D_MODEL),
            pl.BlockSpec((D_MODEL, 2 * N_EXPERTS), lambda i: (0, 0)),
            vec(N_EXPERTS),
            pl.BlockSpec((OUT_TM, OUT_TM), lambda i: (0, 0)),
        ],
        out_specs=[
            pl.BlockSpec((OUT_TM, D_MODEL), lambda i: (i, 0)),
            pl.BlockSpec((OUT_TM, D_PACK), lambda i: (i, 0)),
            pl.BlockSpec((OUT_TM, LANES), lambda i: (i, 0)),
            pl.BlockSpec((OUT_TM, LANES), lambda i: (i, 0)),
            pl.BlockSpec((OUT_TM, LANES), lambda i: (i, 0)),
            pl.BlockSpec((1, N_EXPERTS), lambda i: (0, 0)),
        ],
        out_shape=[
            jax.ShapeDtypeStruct((t, D_MODEL), F32),
            jax.ShapeDtypeStruct((t, D_PACK), jnp.uint32),
            jax.ShapeDtypeStruct((t, LANES), jnp.int32),
            jax.ShapeDtypeStruct((t, LANES), jnp.int32),
            jax.ShapeDtypeStruct((t, LANES), F32),
            jax.ShapeDtypeStruct((1, N_EXPERTS), jnp.int32),
        ],
        scratch_shapes=[pltpu.VMEM((1, N_EXPERTS), F32)],
        compiler_params=_params(("arbitrary",)),
        name="out_proj",
    )(ma, mc, x2, w_out_bf, b_out, gate1, g2, scale2, shift2, w_router, b_router, tri)


DISP_TD = 256
DISP_NBUF = 3


def _dispatch_kernel(dest_ref, pad_lo_ref, pad_hi_ref, nu_ref, h2_ref, xs_ref,
                     stage_ref, zero_ref, in_sems, out_sems, zsem):
    step = pl.program_id(0)
    last = pl.num_programs(0) - 1
    slot = lax.rem(step, DISP_NBUF)
    rows_per_chunk = DISP_TD * TOP_K

    def load(chunk, s):
        src = h2_ref.at[pl.ds(pl.multiple_of(chunk * DISP_TD, DISP_TD), DISP_TD), :]
        return pltpu.make_async_copy(src, stage_ref.at[s], in_sems.at[s])

    def wait_chunk(s):
        pltpu.make_async_copy(h2_ref.at[pl.ds(0, rows_per_chunk), :], xs_ref.at[pl.ds(0, rows_per_chunk), :],
                              out_sems.at[s]).wait()

    @pl.when(step == 0)
    def _():
        load(0, 0).start()

        @pl.when(last >= 1)
        def _():
            load(1, 1).start()

    load(step, slot).wait()

    def issue(r, carry):
        for k in range(TOP_K):
            d = dest_ref[(step * DISP_TD + r) * TOP_K + k]
            pltpu.make_async_copy(stage_ref.at[slot, pl.ds(r, 1), :], xs_ref.at[pl.ds(d, 1), :],
                                  out_sems.at[slot]).start(priority=k % 2)
        return carry

    lax.fori_loop(0, DISP_TD, issue, 0, unroll=2)

    @pl.when(step == 0)
    def _():
        zero_ref[...] = jnp.zeros_like(zero_ref)

        def zero_row(d):
            return pltpu.make_async_copy(zero_ref.at[pl.ds(0, 1), :], xs_ref.at[pl.ds(d, 1), :], zsem)

        def zero_block(b):
            d = pl.multiple_of(b * EXPERT_TM, EXPERT_TM)
            return pltpu.make_async_copy(zero_ref, xs_ref.at[pl.ds(d, EXPERT_TM), :], zsem)

        def start_row(d, carry):
            zero_row(d).start()
            return carry

        def wait_row(d, carry):
            zero_row(d).wait()
            return carry

        def start_block(b, carry):
            zero_block(b).start()
            return carry

        def wait_block(b, carry):
            zero_block(b).wait()
            return carry

        def per_expert(e, carry):
            lax.fori_loop(pad_lo_ref[e], pad_hi_ref[e], start_row, 0)
            lax.fori_loop(pad_lo_ref[e], pad_hi_ref[e], wait_row, 0)
            return carry

        lax.fori_loop(0, N_EXPERTS, per_expert, 0)
        n_blocks = xs_ref.shape[0] // EXPERT_TM
        lax.fori_loop(nu_ref[0], n_blocks, start_block, 0)
        lax.fori_loop(nu_ref[0], n_blocks, wait_block, 0)

    @pl.when(step > 0)
    def _():
        wait_chunk(lax.rem(step + DISP_NBUF - 1, DISP_NBUF))

    @pl.when(step + 2 <= last)
    def _():
        load(step + 2, lax.rem(step + 2, DISP_NBUF)).start()

    @pl.when(step == last)
    def _():
        wait_chunk(slot)


def _dispatch(h2p, dest_flat, pad_lo, pad_hi, n_used, n_rows):
    t = h2p.shape[0]
    grid_spec = pltpu.PrefetchScalarGridSpec(
        num_scalar_prefetch=4,
        grid=(t // DISP_TD,),
        in_specs=[pl.BlockSpec(memory_space=pl.ANY)],
        out_specs=pl.BlockSpec(memory_space=pl.ANY),
        scratch_shapes=[pltpu.VMEM((DISP_NBUF, DISP_TD, D_PACK), jnp.uint32),
                        pltpu.VMEM((EXPERT_TM, D_PACK), jnp.uint32),
                        pltpu.SemaphoreType.DMA((DISP_NBUF,)), pltpu.SemaphoreType.DMA((DISP_NBUF,)),
                        pltpu.SemaphoreType.DMA(())],
    )
    return pl.pallas_call(
        _dispatch_kernel,
        grid_spec=grid_spec,
        out_shape=jax.ShapeDtypeStruct((n_rows, D_PACK), jnp.uint32),
        compiler_params=_params(("arbitrary",)),
        name="dispatch",
    )(dest_flat, pad_lo, pad_hi, n_used, h2p)


UP_TF = 1024
DOWN_TN = 2048


def _weight_index_map(n_chunks):
    def index_map(c, b, be, nu, first, nxt, full):
        del nu, full
        in_last = nxt[b] < 0
        wrap = jnp.logical_and(in_last, c + 1 < n_chunks)
        e_next = jnp.where(in_last, jnp.where(wrap, be[0], be[b]), nxt[b])
        c_next = jnp.where(wrap, c + 1, c)
        is_first = first[b] == 1
        return jnp.where(is_first, be[b], e_next), 0, jnp.where(is_first, c, c_next)

    return index_map


def _row_cases(b, nu_ref, full_ref, o_ref, compute):
    half = EXPERT_TM // 2
    used = b < nu_ref[0]

    @pl.when(jnp.logical_and(used, full_ref[b] == 1))
    def _():
        compute(EXPERT_TM)

    @pl.when(jnp.logical_and(used, full_ref[b] == 0))
    def _():
        compute(half)
        o_ref[half:, :] = jnp.zeros((half, o_ref.shape[1]), o_ref.dtype)

    @pl.when(jnp.logical_not(used))
    def _():
        o_ref[...] = jnp.zeros_like(o_ref)


def _up_kernel(be_ref, nu_ref, first_ref, nxt_ref, full_ref, x_ref, wg_ref, wu_ref, bg_ref, bu_ref, h_ref,
               wg_bf, wu_bf):
    b = pl.program_id(1)

    @pl.when(first_ref[b] == 1)
    def _():
        wg_bf[...] = wg_ref[0].astype(BF16)
        wu_bf[...] = wu_ref[0].astype(BF16)

    def compute(m):
        x_lo, x_hi = _unpack_bf16_pairs(x_ref[0:m, :])

        def proj(w_bf, bias_ref):
            return (jnp.dot(x_lo, w_bf[0:D_PACK, :], preferred_element_type=F32)
                    + jnp.dot(x_hi, w_bf[D_PACK:, :], preferred_element_type=F32) + bias_ref[0])

        g = proj(wg_bf, bg_ref)
        lin = proj(wu_bf, bu_ref)
        g = jnp.minimum(g, SWIGLU_LIMIT)
        lin = jnp.clip(lin, -SWIGLU_LIMIT, SWIGLU_LIMIT)
        act = g * jax.nn.sigmoid(SWIGLU_ALPHA * g) * (lin + 1.0)
        h_ref[0:m, :] = act.astype(BF16)

    _row_cases(b, nu_ref, full_ref, h_ref, compute)


def _up(xs, tables, w_gate, b_gate, w_up, b_up):
    n_rows = xs.shape[0]
    nb = n_rows // EXPERT_TM
    n_chunks = D_FF // UP_TF
    row = lambda f, b, be, nu, first, nxt, full: (jnp.minimum(b, nu[0] - 1), 0)
    wsel = _weight_index_map(n_chunks)
    bsel = lambda f, b, be, nu, first, nxt, full: (be[b], 0, f)
    grid_spec = pltpu.PrefetchScalarGridSpec(
        num_scalar_prefetch=5,
        grid=(n_chunks, nb),
        in_specs=[
            pl.BlockSpec((EXPERT_TM, D_PACK), row),
            pl.BlockSpec((1, D_MODEL, UP_TF), wsel),
            pl.BlockSpec((1, D_MODEL, UP_TF), wsel),
            pl.BlockSpec((1, 1, UP_TF), bsel),
            pl.BlockSpec((1, 1, UP_TF), bsel),
        ],
        out_specs=pl.BlockSpec((EXPERT_TM, UP_TF), lambda f, b, be, nu, first, nxt, full: (b, f)),
        scratch_shapes=[pltpu.VMEM((D_MODEL, UP_TF), BF16), pltpu.VMEM((D_MODEL, UP_TF), BF16)],
    )
    return pl.pallas_call(
        _up_kernel,
        grid_spec=grid_spec,
        out_shape=jax.ShapeDtypeStruct((n_rows, D_FF), BF16),
        compiler_params=_params(("arbitrary", "arbitrary"), EXPERT_VMEM_LIMIT),
        name="expert_up",
    )(*tables, xs, w_gate, w_up,
      b_gate.reshape(N_EXPERTS, 1, D_FF), b_up.reshape(N_EXPERTS, 1, D_FF))


def _down_kernel(be_ref, nu_ref, first_ref, nxt_ref, full_ref, h_ref, wd_ref, bd_ref, y_ref, wd_bf):
    b = pl.program_id(1)

    @pl.when(first_ref[b] == 1)
    def _():
        wd_bf[...] = wd_ref[0].astype(BF16)

    def compute(m):
        y = jnp.dot(h_ref[0:m, :], wd_bf[...], preferred_element_type=F32) + bd_ref[0]
        y_ref[0:m, :] = _pack_bf16_pairs(y)

    _row_cases(b, nu_ref, full_ref, y_ref, compute)


def _down(hs, tables, w_down, b_down):
    n_rows = hs.shape[0]
    nb = n_rows // EXPERT_TM
    n_chunks = D_MODEL // DOWN_TN
    assert n_chunks == 1
    row = lambda n, b, be, nu, first, nxt, full: (jnp.minimum(b, nu[0] - 1), 0)
    grid_spec = pltpu.PrefetchScalarGridSpec(
        num_scalar_prefetch=5,
        grid=(n_chunks, nb),
        in_specs=[
            pl.BlockSpec((EXPERT_TM, D_FF), row),
            pl.BlockSpec((1, D_FF, DOWN_TN), _weight_index_map(n_chunks)),
            pl.BlockSpec((1, 1, DOWN_TN), lambda n, b, be, nu, first, nxt, full: (be[b], 0, n)),
        ],
        out_specs=pl.BlockSpec((EXPERT_TM, D_PACK), lambda n, b, be, nu, first, nxt, full: (b, 0)),
        scratch_shapes=[pltpu.VMEM((D_FF, DOWN_TN), BF16)],
    )
    return pl.pallas_call(
        _down_kernel,
        grid_spec=grid_spec,
        out_shape=jax.ShapeDtypeStruct((n_rows, D_PACK), jnp.uint32),
        compiler_params=_params(("arbitrary", "arbitrary"), EXPERT_VMEM_LIMIT),
        name="expert_down",
    )(*tables, hs, w_down, b_down.reshape(N_EXPERTS, 1, D_MODEL))


COMB_TC = 256


def _combine_kernel(dest_ref, xm_ref, wt_ref, g2_ref, ys_ref, o_ref, buf_ref, sems):
    step = pl.program_id(0)
    last = pl.num_programs(0) - 1
    slot = step % 2

    def gather_tile(tile, s):
        def issue(r, carry):
            for k in range(TOP_K):
                d = dest_ref[(tile * COMB_TC + r) * TOP_K + k]
                pltpu.make_async_copy(ys_ref.at[pl.ds(d, 1), :], buf_ref.at[s, pl.ds(k * COMB_TC + r, 1), :],
                                      sems.at[s]).start(priority=k % 2)
            return carry

        lax.fori_loop(0, COMB_TC, issue, 0, unroll=2)

    @pl.when(step == 0)
    def _():
        gather_tile(0, 0)

    @pl.when(step < last)
    def _():
        gather_tile(step + 1, 1 - slot)

    pltpu.make_async_copy(ys_ref.at[pl.ds(0, TOP_K * COMB_TC), :], buf_ref.at[slot], sems.at[slot]).wait()

    wt = wt_ref[...]
    acc_lo = acc_hi = None
    for k in range(TOP_K):
        lo, hi = _unpack_pairs_f32(buf_ref[slot, k * COMB_TC:(k + 1) * COMB_TC, :])
        w = wt[:, k:k + 1]
        acc_lo = w * lo if acc_lo is None else acc_lo + w * lo
        acc_hi = w * hi if acc_hi is None else acc_hi + w * hi
    o_ref[:, :D_PACK] = xm_ref[:, :D_PACK] + g2_ref[:, :D_PACK] * acc_lo
    o_ref[:, D_PACK:] = xm_ref[:, D_PACK:] + g2_ref[:, D_PACK:] * acc_hi


def _combine(ys, dest_flat, xm, wts, gate2):
    t = xm.shape[0]
    grid_spec = pltpu.PrefetchScalarGridSpec(
        num_scalar_prefetch=1,
        grid=(t // COMB_TC,),
        in_specs=[
            pl.BlockSpec((COMB_TC, D_MODEL), lambda i, d: (i, 0)),
            pl.BlockSpec((COMB_TC, LANES), lambda i, d: (i, 0)),
            pl.BlockSpec((1, D_MODEL), lambda i, d: (0, 0)),
            pl.BlockSpec(memory_space=pl.ANY),
        ],
        out_specs=pl.BlockSpec((COMB_TC, D_MODEL), lambda i, d: (i, 0)),
        scratch_shapes=[pltpu.VMEM((2, TOP_K * COMB_TC, D_PACK), jnp.uint32), pltpu.SemaphoreType.DMA((2,))],
    )
    return pl.pallas_call(
        _combine_kernel,
        grid_spec=grid_spec,
        out_shape=jax.ShapeDtypeStruct((t, D_MODEL), F32),
        compiler_params=_params(("arbitrary",)),
        name="combine",
    )(dest_flat, xm, wts, gate2, ys)


def _routing_tables(idx, rank, counts, n_tok):
    padded = ((counts + EXPERT_TM - 1) // EXPERT_TM) * EXPERT_TM
    pend = jnp.cumsum(padded)
    pstart = pend - padded
    experts = jnp.arange(N_EXPERTS, dtype=jnp.int32)
    start_of = jnp.sum(jnp.where(idx[..., None] == experts, pstart, 0), axis=-1)
    dest = (start_of + rank).astype(jnp.int32).reshape(-1)
    n_rows = n_tok * TOP_K + N_EXPERTS * EXPERT_TM
    nb = n_rows // EXPERT_TM
    block_start = jnp.arange(nb, dtype=jnp.int32) * EXPERT_TM
    block_expert = jnp.minimum(jnp.sum(pend[None, :] <= block_start[:, None], axis=1),
                               N_EXPERTS - 1).astype(jnp.int32)
    n_used = (pend[-1:] // EXPERT_TM).astype(jnp.int32)
    pad_lo = (pstart + counts).astype(jnp.int32)
    pad_hi = pend.astype(jnp.int32)

    blocks = jnp.arange(nb, dtype=jnp.int32)
    block_expert = jnp.where(blocks < n_used[0], block_expert, block_expert[jnp.maximum(n_used[0] - 1, 0)])
    prev = jnp.concatenate([block_expert[:1], block_expert[:-1]])
    first = jnp.logical_or(blocks == 0, block_expert != prev)
    first_pos = jnp.where(first, blocks, nb)
    next_first = jnp.flip(lax.cummin(jnp.flip(jnp.concatenate([first_pos[1:], jnp.full((1,), nb, jnp.int32)]))))
    nxt = jnp.where(next_first < nb, block_expert[jnp.minimum(next_first, nb - 1)], -1).astype(jnp.int32)
    rows_end = jnp.sum(jnp.where(block_expert[:, None] == experts, pad_lo, 0), axis=-1)
    full = (rows_end > block_start + EXPERT_TM // 2).astype(jnp.int32)
    tables = (block_expert, n_used, first.astype(jnp.int32), nxt, full)
    return dest, tables, pad_lo, pad_hi, n_rows


def _layer(x2, c, w_ada, b_ada, g_norm1, w_in, b_in, g_q, g_k, sinks, rel_bias, w_dw, b_dw, ln_g, ln_b,
           g_out_attn, g_out_conv, w_out, b_out, g_norm2, w_router, b_router,
           w_gate, b_gate, w_up, b_up, w_down, b_down):
    t = x2.shape[0]
    row = lambda v: v.reshape(1, -1)
    mod = _ada(c, w_ada, b_ada)
    shift1, scale1, gate1, shift2, scale2, gate2 = [mod[:, i * D_MODEL:(i + 1) * D_MODEL] for i in range(6)]

    q, kv, uc = _in_proj(x2, row(g_norm1), scale1, shift1, w_in.astype(BF16), row(b_in))

    gq_t = row(jnp.tile(g_q, N_Q_HEADS)) * (HEAD_DIM ** -0.5)
    gk_t = row(jnp.tile(g_k, N_KV_HEADS))
    ma = _attention(q, kv, _bias_table(rel_bias), sinks, gq_t, gk_t, row(g_out_attn),
                    _head_indicator(D_ATTN), _head_indicator(LANES))
    w_dw_p = jnp.concatenate([w_dw, jnp.zeros((HALO - CONV_WIDTH, D_CONV), w_dw.dtype)], axis=0)
    mc = _conv(uc, w_dw_p, row(b_dw), row(ln_g), row(ln_b), row(g_out_conv))

    xm, h2, idx, rank, wts, counts = _out_proj(
        ma, mc, x2, w_out.astype(BF16), row(b_out), gate1, row(g_norm2), scale2, shift2,
        w_router, row(b_router))

    dest, tables, pad_lo, pad_hi, n_rows = _routing_tables(idx[:, :TOP_K], rank[:, :TOP_K], counts[0], t)
    xs = _dispatch(h2, dest, pad_lo, pad_hi, tables[1], n_rows)
    hs = _up(xs, tables, w_gate, b_gate, w_up, b_up)
    ys = _down(hs, tables, w_down, b_down)
    return _combine(ys, dest, xm, wts, gate2)


def kernel(x, c, w_ada, b_ada, g_norm1, w_in, b_in, g_q, g_k, sinks, rel_bias, w_dw, b_dw, ln_g, ln_b,
           g_out_attn, g_out_conv, w_out, b_out, g_norm2, w_router, b_router,
           w_gate, b_gate, w_up, b_up, w_down, b_down):
    b, t, d = x.shape
    assert b == 1 and d == D_MODEL and w_ada.shape[0] == 1
    out = _layer(x.reshape(t, d), c, w_ada[0], b_ada[0], g_norm1[0], w_in[0], b_in[0], g_q[0], g_k[0],
                 sinks[0], rel_bias, w_dw[0], b_dw[0], ln_g[0], ln_b[0], g_out_attn[0], g_out_conv[0],
                 w_out[0], b_out[0], g_norm2[0], w_router[0], b_router[0],
                 w_gate[0], b_gate[0], w_up[0], b_up[0], w_down[0], b_down[0])
    return out.reshape(b, t, d)
```

```python
import functools
import math

import jax
import jax.numpy as jnp
from jax import lax
from jax.experimental import pallas as pl
from jax.experimental.pallas import tpu as pltpu

D_MODEL = 2048
HEAD_DIM = 64
N_Q_HEADS = 16
N_KV_HEADS = 2
D_ATTN = N_Q_HEADS * HEAD_DIM
D_KV = N_KV_HEADS * HEAD_DIM
D_CONV = D_MODEL - D_ATTN
D_IN = D_ATTN + 2 * D_KV + 2 * D_CONV
WINDOW = 128
BLOCK = 128
CONV_WIDTH = 31
N_BUCKETS = 32
MAX_DISTANCE = 128
N_EXPERTS = 32
TOP_K = 4
D_FF = D_MODEL
SWIGLU_LIMIT = 7.0
SWIGLU_ALPHA = 1.702
EPS = 1e-6
NEG_INF = -1e30

LANES = 128
SUBLANES = 8
VMEM_LIMIT = 56 * 1024 * 1024

HALO = 32
EXPERT_TM = 512
EXPERT_VMEM_LIMIT = 60 * 1024 * 1024
F32 = jnp.float32
BF16 = jnp.bfloat16


def _params(sem, vmem_limit=VMEM_LIMIT):
    return pltpu.CompilerParams(dimension_semantics=sem, vmem_limit_bytes=vmem_limit)


ADA_TN = 512
ADA_RC = 256


def _ada_kernel(c_ref, w_ref, b_ref, o_ref):
    tn = w_ref.shape[1]

    def body(i, acc):
        r = pl.multiple_of(i * ADA_RC, ADA_RC)
        c = c_ref[pl.ds(r, ADA_RC), :]
        sc = c * jax.nn.sigmoid(c)
        prod = w_ref[pl.ds(r, ADA_RC), :] * sc
        return acc + jnp.sum(prod.reshape(ADA_RC // 8, 8, tn), axis=0)

    acc = lax.fori_loop(0, D_MODEL // ADA_RC, body, jnp.zeros((8, tn), F32))
    o_ref[...] = jnp.sum(acc, axis=0, keepdims=True) + b_ref[...]


def _ada(c, w_ada, b_ada):
    n = w_ada.shape[1]
    return pl.pallas_call(
        _ada_kernel,
        grid=(n // ADA_TN,),
        in_specs=[
            pl.BlockSpec((D_MODEL, 1), lambda j: (0, 0)),
            pl.BlockSpec((D_MODEL, ADA_TN), lambda j: (0, j)),
            pl.BlockSpec((1, ADA_TN), lambda j: (0, j)),
        ],
        out_specs=pl.BlockSpec((1, ADA_TN), lambda j: (0, j)),
        out_shape=jax.ShapeDtypeStruct((1, n), F32),
        compiler_params=_params(("parallel",)),
        name="ada",
    )(c.reshape(D_MODEL, 1), w_ada, b_ada.reshape(1, n))


IN_TM = 512
IN_NC = 256


def _modulated_rms(x, g, scale, shift):
    ms = jnp.mean(x * x, axis=-1, keepdims=True)
    return (x * lax.rsqrt(ms + EPS) * g) * (1.0 + scale) + shift


def _in_kernel(x_ref, g_ref, sc_ref, sh_ref, w_ref, b_ref, q_ref, kv_ref, uc_ref):
    h = _modulated_rms(x_ref[...], g_ref[...], sc_ref[...], sh_ref[...]).astype(BF16)

    def proj(lo, n, o_ref):
        for c in range(0, n, IN_NC):
            u = jnp.dot(h, w_ref[:, lo + c:lo + c + IN_NC], preferred_element_type=F32)
            o_ref[:, c:c + IN_NC] = u + b_ref[:, lo + c:lo + c + IN_NC]

    proj(0, D_ATTN, q_ref)
    proj(D_ATTN, 2 * D_KV, kv_ref)
    proj(D_ATTN + 2 * D_KV, 2 * D_CONV, uc_ref)


def _in_proj(x2, g1, scale1, shift1, w_in_bf, b_in):
    t = x2.shape[0]
    vec = lambda n: pl.BlockSpec((1, n), lambda i: (0, 0))
    return pl.pallas_call(
        _in_kernel,
        grid=(t // IN_TM,),
        in_specs=[
            pl.BlockSpec((IN_TM, D_MODEL), lambda i: (i, 0)),
            vec(D_MODEL), vec(D_MODEL), vec(D_MODEL),
            pl.BlockSpec((D_MODEL, D_IN), lambda i: (0, 0)),
            vec(D_IN),
        ],
        out_specs=[
            pl.BlockSpec((IN_TM, D_ATTN), lambda i: (i, 0)),
            pl.BlockSpec((IN_TM, 2 * D_KV), lambda i: (i, 0)),
            pl.BlockSpec((IN_TM, 2 * D_CONV), lambda i: (i, 0)),
        ],
        out_shape=[
            jax.ShapeDtypeStruct((t, D_ATTN), F32),
            jax.ShapeDtypeStruct((t, 2 * D_KV), F32),
            jax.ShapeDtypeStruct((t, 2 * D_CONV), F32),
        ],
        compiler_params=_params(("parallel",)),
        name="in_proj",
    )(x2, g1, scale1, shift1, w_in_bf, b_in)


ATT_R = 4


def _split_dot(a, b_bf):
    hi = a.astype(BF16)
    lo = (a - hi.astype(F32)).astype(BF16)
    return (jnp.dot(hi, b_bf, preferred_element_type=F32)
            + jnp.dot(lo, b_bf, preferred_element_type=F32))


def _attn_kernel(sinks_ref, q_ref, kvp_ref, kvc_ref, bias_ref, gq_ref, gk_ref, go_ref,
                 hq_ref, hk_ref, o_ref, y_ref):
    step = pl.program_id(0)
    lane = lax.broadcasted_iota(jnp.int32, (2 * BLOCK, LANES), 1)
    low = lane < HEAD_DIM
    col = lax.broadcasted_iota(jnp.int32, (BLOCK, 2 * BLOCK), 1)
    first_lo = jnp.where(step == 0, BLOCK, 0)

    for r in range(ATT_R):
        rows = slice(r * BLOCK, (r + 1) * BLOCK)
        q = q_ref[rows, :]
        ssq = _split_dot(q * q, hq_ref[...])
        qn = (q * lax.rsqrt(ssq * (1.0 / HEAD_DIM) + EPS) * gq_ref[...]).astype(BF16)

        if r == 0:
            kv_prev = kvp_ref[...]
        else:
            kv_prev = kvc_ref[(r - 1) * BLOCK:r * BLOCK, :]
        kv = jnp.concatenate([kv_prev, kvc_ref[rows, :]], axis=0)
        k = kv[:, :LANES]
        v = kv[:, LANES:]
        kss = _split_dot(k * k, hk_ref[...])
        kn = k * lax.rsqrt(kss * (1.0 / HEAD_DIM) + EPS) * gk_ref[...]
        kn_sw = pltpu.roll(kn, HEAD_DIM, axis=1)
        v_sw = pltpu.roll(v, HEAD_DIM, axis=1)
        zero = jnp.zeros_like(kn)
        k_lo = [jnp.where(low, kn, zero).astype(BF16), jnp.where(low, kn_sw, zero).astype(BF16)]
        k_hi = [jnp.where(low, zero, kn_sw).astype(BF16), jnp.where(low, zero, kn).astype(BF16)]
        v_lo = [jnp.where(low, v, zero).astype(BF16), jnp.where(low, v_sw, zero).astype(BF16)]
        v_hi = [jnp.where(low, zero, v_sw).astype(BF16), jnp.where(low, zero, v).astype(BF16)]

        for p in range(N_Q_HEADS // 2):
            g = (2 * p) // (N_Q_HEADS // N_KV_HEADS)
            qp = qn[:, p * LANES:(p + 1) * LANES]
            acc = None
            for half, (kz, vz) in enumerate(((k_lo[g], v_lo[g]), (k_hi[g], v_hi[g]))):
                h = 2 * p + half
                s = lax.dot_general(qp, kz, (((1,), (1,)), ((), ())), preferred_element_type=F32)
                s = s + bias_ref[h]
                if r == 0:
                    s = jnp.where(col >= first_lo, s, NEG_INF)
                sink = sinks_ref[h]
                m = jnp.maximum(jnp.max(s, axis=-1, keepdims=True), sink)
                e = jnp.exp(s - m)
                denom = jnp.sum(e, axis=-1, keepdims=True) + jnp.exp(sink - m)
                pv = jnp.dot(e.astype(BF16), vz, preferred_element_type=F32)
                pv = pv * (1.0 / denom)
                acc = pv if acc is None else acc + pv
            y_ref[:, p * LANES:(p + 1) * LANES] = acc

        y = y_ref[...]
        ms = jnp.mean(y * y, axis=-1, keepdims=True)
        o_ref[rows, :] = (y * lax.rsqrt(ms + EPS) * go_ref[...]).astype(BF16)


def _attention(q, kv, bias, sinks, gq_t, gk_t, g_out, hq, hk):
    t = q.shape[0]
    tile = ATT_R * BLOCK
    grid_spec = pltpu.PrefetchScalarGridSpec(
        num_scalar_prefetch=0,
        grid=(t // tile,),
        in_specs=[
            pl.BlockSpec(memory_space=pltpu.SMEM),
            pl.BlockSpec((tile, D_ATTN), lambda i: (i, 0)),
            pl.BlockSpec((BLOCK, 2 * D_KV), lambda i: (jnp.maximum(i * ATT_R - 1, 0), 0)),
            pl.BlockSpec((tile, 2 * D_KV), lambda i: (i, 0)),
            pl.BlockSpec((N_Q_HEADS, BLOCK, 2 * BLOCK), lambda i: (0, 0, 0)),
            pl.BlockSpec((1, D_ATTN), lambda i: (0, 0)),
            pl.BlockSpec((1, LANES), lambda i: (0, 0)),
            pl.BlockSpec((1, D_ATTN), lambda i: (0, 0)),
            pl.BlockSpec((D_ATTN, D_ATTN), lambda i: (0, 0)),
            pl.BlockSpec((LANES, LANES), lambda i: (0, 0)),
        ],
        out_specs=pl.BlockSpec((tile, D_ATTN), lambda i: (i, 0)),
        scratch_shapes=[pltpu.VMEM((BLOCK, D_ATTN), F32)],
    )
    return pl.pallas_call(
        _attn_kernel,
        grid_spec=grid_spec,
        out_shape=jax.ShapeDtypeStruct((t, D_ATTN), BF16),
        compiler_params=_params(("parallel",)),
        name="attn",
    )(sinks, q, kv, kv, bias, gq_t, gk_t, g_out, hq, hk)


def _t5_bucket(dist):
    max_exact = N_BUCKETS // 2
    d = jnp.maximum(dist, 0)
    log_ratio = jnp.log(jnp.maximum(d, max_exact).astype(F32) / max_exact)
    large = max_exact + (log_ratio / math.log(MAX_DISTANCE / max_exact)
                         * (N_BUCKETS - max_exact)).astype(jnp.int32)
    large = jnp.minimum(large, N_BUCKETS - 1)
    return jnp.where(d < max_exact, d, large)


def _bias_table(rel_bias):
    q_local = jnp.arange(BLOCK, dtype=jnp.int32) + BLOCK
    k_local = jnp.arange(2 * BLOCK, dtype=jnp.int32)
    dist = q_local[:, None] - k_local[None, :]
    band = (dist >= 0) & (dist < WINDOW)
    bucket = _t5_bucket(dist)
    hit = (jnp.arange(N_BUCKETS, dtype=jnp.int32)[:, None] == bucket.reshape(1, -1)).astype(F32)
    bias = jnp.dot(rel_bias.astype(F32).T, hit, precision=lax.Precision.HIGHEST)
    return jnp.where(band[None], bias.reshape(N_Q_HEADS, BLOCK, 2 * BLOCK), NEG_INF)


def _head_indicator(n):
    i = jnp.arange(n) // HEAD_DIM
    return (i[:, None] == i[None, :]).astype(BF16)


CONV_TT = 256
CONV_RC = 64
CONV_CC = 256


def _conv_kernel(u_ref, halo_ref, w_ref, b_ref, lg_ref, lb_ref, go_ref, o_ref, h_ref, y_ref):
    step = pl.program_id(0)

    def glu(u):
        return u[:, :D_CONV] * jax.nn.sigmoid(u[:, D_CONV:])

    hh = glu(halo_ref[...])
    h_ref[0, 0:HALO, :] = jnp.where(step == 0, jnp.zeros_like(hh), hh)
    h_ref[0, HALO:, :] = glu(u_ref[...])

    n_shifted = CONV_TT + HALO - SUBLANES
    for s in range(1, SUBLANES):
        for c0 in range(0, D_CONV, CONV_CC):
            cs = slice(c0, c0 + CONV_CC)
            h_ref[s, 0:n_shifted, cs] = h_ref[0, s:s + n_shifted, cs]

    off = HALO - (CONV_WIDTH - 1)
    for r0 in range(0, CONV_TT, CONV_RC):
        for c0 in range(0, D_CONV, CONV_CC):
            cs = slice(c0, c0 + CONV_CC)
            acc = jnp.broadcast_to(b_ref[:, cs], (CONV_RC, CONV_CC))
            for j in range(CONV_WIDTH):
                s = (r0 + off + j) % SUBLANES
                a = r0 + off + j - s
                acc = acc + w_ref[j:j + 1, cs] * h_ref[s, a:a + CONV_RC, cs]
            y_ref[r0:r0 + CONV_RC, cs] = acc

    y = y_ref[...]
    mu = jnp.mean(y, axis=-1, keepdims=True)
    yc = y - mu
    var = jnp.mean(yc * yc, axis=-1, keepdims=True)
    z = yc * lax.rsqrt(var + EPS) * lg_ref[...] + lb_ref[...]
    s = z * jax.nn.sigmoid(z)
    ms = jnp.mean(s * s, axis=-1, keepdims=True)
    o_ref[...] = (s * lax.rsqrt(ms + EPS) * go_ref[...]).astype(BF16)


def _conv(uc, w_dw, b_dw, ln_g, ln_b, g_out):
    t = uc.shape[0]
    vec = lambda: pl.BlockSpec((1, D_CONV), lambda i: (0, 0))
    per = CONV_TT // HALO
    return pl.pallas_call(
        _conv_kernel,
        grid=(t // CONV_TT,),
        in_specs=[
            pl.BlockSpec((CONV_TT, 2 * D_CONV), lambda i: (i, 0)),
            pl.BlockSpec((HALO, 2 * D_CONV), lambda i: (jnp.maximum(i * per - 1, 0), 0)),
            pl.BlockSpec((HALO, D_CONV), lambda i: (0, 0)),
            vec(), vec(), vec(), vec(),
        ],
        out_specs=pl.BlockSpec((CONV_TT, D_CONV), lambda i: (i, 0)),
        out_shape=jax.ShapeDtypeStruct((t, D_CONV), BF16),
        scratch_shapes=[pltpu.VMEM((SUBLANES, CONV_TT + HALO, D_CONV), F32),
                        pltpu.VMEM((CONV_TT, D_CONV), F32)],
        compiler_params=_params(("parallel",)),
        name="conv",
    )(uc, uc, w_dw, b_dw, ln_g, ln_b, g_out)


OUT_TM = 512
D_PACK = D_MODEL // 2
HI_MASK = 0xFFFF0000


def _pack_bf16_pairs(h):
    lo = lax.bitcast_convert_type(h[:, :D_PACK].astype(BF16).astype(F32), jnp.uint32)
    hi = lax.bitcast_convert_type(h[:, D_PACK:].astype(BF16).astype(F32), jnp.uint32)
    return (lo >> 16) | (hi & jnp.uint32(HI_MASK))


def _unpack_pairs_f32(w):
    lo = lax.bitcast_convert_type(w << 16, F32)
    hi = lax.bitcast_convert_type(w & jnp.uint32(HI_MASK), F32)
    return lo, hi


def _unpack_bf16_pairs(w):
    lo, hi = _unpack_pairs_f32(w)
    return lo.astype(BF16), hi.astype(BF16)


def _out_kernel(ma_ref, mc_ref, x_ref, w_ref, bo_ref, g1_ref, g2_ref, sc_ref, sh_ref, wr_ref, br_ref,
                tri_ref, xm_ref, h2_ref, idx_ref, rank_ref, wt_ref, cnt_ref, carry_ref):
    step = pl.program_id(0)

    @pl.when(step == 0)
    def _():
        carry_ref[...] = jnp.zeros_like(carry_ref)

    y = (jnp.dot(ma_ref[...], w_ref[0:D_ATTN, :], preferred_element_type=F32)
         + jnp.dot(mc_ref[...], w_ref[D_ATTN:, :], preferred_element_type=F32) + bo_ref[...])
    xm = x_ref[...] + g1_ref[...] * y
    xm_ref[...] = xm
    h2 = _modulated_rms(xm, g2_ref[...], sc_ref[...], sh_ref[...])
    h2_ref[...] = _pack_bf16_pairs(h2)

    h_hi = h2.astype(BF16)
    h_lo = (h2 - h_hi.astype(F32)).astype(BF16)
    both = (jnp.dot(h_hi, wr_ref[...], preferred_element_type=F32)
            + jnp.dot(h_lo, wr_ref[...], preferred_element_type=F32))
    logits = both[:, :N_EXPERTS] + both[:, N_EXPERTS:] + br_ref[...]
    tm = logits.shape[0]
    lane = lax.broadcasted_iota(jnp.int32, (tm, N_EXPERTS), 1).astype(F32)
    vals, idxs = [], []
    l = logits
    for _ in range(TOP_K):
        m = jnp.max(l, axis=-1, keepdims=True)
        i = jnp.min(jnp.where(l == m, lane, float(N_EXPERTS)), axis=-1, keepdims=True)
        vals.append(m)
        idxs.append(i)
        l = jnp.where(lane == i, -jnp.inf, l)
    es = [jnp.exp(v - vals[0]) for v in vals]
    tot = es[0] + es[1] + es[2] + es[3]
    ws = [e / tot for e in es]

    hot = [(lane == i).astype(F32) for i in idxs]
    hot_all = hot[0] + hot[1] + hot[2] + hot[3]
    before = jnp.dot(tri_ref[...], hot_all.astype(BF16), preferred_element_type=F32) + carry_ref[...]
    ranks = [jnp.sum(h * before, axis=-1, keepdims=True) for h in hot]
    carry_ref[...] = carry_ref[...] + jnp.sum(hot_all, axis=0, keepdims=True)
    cnt_ref[...] = carry_ref[...].astype(jnp.int32)

    slot = lax.broadcasted_iota(jnp.int32, (tm, LANES), 1)

    def pack(cols):
        out = jnp.zeros((tm, LANES), F32)
        for k in range(TOP_K):
            out = jnp.where(slot == k, cols[k], out)
        return out

    idx_ref[...] = pack(idxs).astype(jnp.int32)
    rank_ref[...] = pack(ranks).astype(jnp.int32)
    wt_ref[...] = pack(ws)


def _out_proj(ma, mc, x2, w_out_bf, b_out, gate1, g2, scale2, shift2, w_router, b_router):
    t = x2.shape[0]
    vec = lambda n: pl.BlockSpec((1, n), lambda i: (0, 0))
    tri = jnp.tril(jnp.ones((OUT_TM, OUT_TM), F32), -1).astype(BF16)
    wr_hi = w_router.astype(BF16)
    wr_lo = (w_router - wr_hi.astype(F32)).astype(BF16)
    w_router = jnp.concatenate([wr_hi, wr_lo], axis=1)
    return pl.pallas_call(
        _out_kernel,
        grid=(t // OUT_TM,),
        in_specs=[
            pl.BlockSpec((OUT_TM, D_ATTN), lambda i: (i, 0)),
            pl.BlockSpec((OUT_TM, D_CONV), lambda i: (i, 0)),
            pl.BlockSpec((OUT_TM, D_MODEL), lambda i: (i, 0)),
            pl.BlockSpec((D_MODEL, D_MODEL), lambda i: (0, 0)),
            vec(D_MODEL), vec(D_MODEL), vec(D_MODEL), vec(D_MODEL), vec(D_MODEL),
            pl.BlockSpec((D_MODEL, 2 * N_EXPERTS), lambda i: (0, 0)),
            vec(N_EXPERTS),
            pl.BlockSpec((OUT_TM, OUT_TM), lambda i: (0, 0)),
        ],
        out_specs=[
            pl.BlockSpec((OUT_TM, D_MODEL), lambda i: (i, 0)),
            pl.BlockSpec((OUT_TM, D_PACK), lambda i: (i, 0)),
            pl.BlockSpec((OUT_TM, LANES), lambda i: (i, 0)),
            pl.BlockSpec((OUT_TM, LANES), lambda i: (i, 0)),
            pl.BlockSpec((OUT_TM, LANES), lambda i: (i, 0)),
            pl.BlockSpec((1, N_EXPERTS), lambda i: (0, 0)),
        ],
        out_shape=[
            jax.ShapeDtypeStruct((t, D_MODEL), F32),
            jax.ShapeDtypeStruct((t, D_PACK), jnp.uint32),
            jax.ShapeDtypeStruct((t, LANES), jnp.int32),
            jax.ShapeDtypeStruct((t, LANES), jnp.int32),
            jax.ShapeDtypeStruct((t, LANES), F32),
            jax.ShapeDtypeStruct((1, N_EXPERTS), jnp.int32),
        ],
        scratch_shapes=[pltpu.VMEM((1, N_EXPERTS), F32)],
        compiler_params=_params(("arbitrary",)),
        name="out_proj",
    )(ma, mc, x2, w_out_bf, b_out, gate1, g2, scale2, shift2, w_router, b_router, tri)


DISP_TD = 256
DISP_NBUF = 3
ZERO_RUN = 64


def _dispatch_kernel(dest_ref, pad_lo_ref, pad_hi_ref, nu_ref, h2_ref, xs_ref,
                     stage_ref, zero_ref, in_sems, out_sems, zsem):
    step = pl.program_id(0)
    last = pl.num_programs(0) - 1
    slot = lax.rem(step, DISP_NBUF)
    rows_per_chunk = DISP_TD * TOP_K

    def load(chunk, s):
        src = h2_ref.at[pl.ds(pl.multiple_of(chunk * DISP_TD, DISP_TD), DISP_TD), :]
        return pltpu.make_async_copy(src, stage_ref.at[s], in_sems.at[s])

    def wait_chunk(s):
        pltpu.make_async_copy(h2_ref.at[pl.ds(0, rows_per_chunk), :], xs_ref.at[pl.ds(0, rows_per_chunk), :],
                              out_sems.at[s]).wait()

    @pl.when(step == 0)
    def _():
        load(0, 0).start()

        @pl.when(last >= 1)
        def _():
            load(1, 1).start()

    load(step, slot).wait()

    def issue(r, carry):
        for k in range(TOP_K):
            d = dest_ref[(step * DISP_TD + r) * TOP_K + k]
            pltpu.make_async_copy(stage_ref.at[slot, pl.ds(r, 1), :], xs_ref.at[pl.ds(d, 1), :],
                                  out_sems.at[slot]).start(priority=k % 2)
        return carry

    lax.fori_loop(0, DISP_TD, issue, 0, unroll=2)

    @pl.when(step == 0)
    def _():
        zero_ref[...] = jnp.zeros_like(zero_ref)

        def zero_rows(d, n):
            d = d if n == 1 else pl.multiple_of(d, n)
            return pltpu.make_async_copy(zero_ref.at[pl.ds(0, n), :], xs_ref.at[pl.ds(d, n), :], zsem)

        def ranges(e):
            lo, hi = pad_lo_ref[e], pad_hi_ref[e]
            a = jnp.minimum((lo + SUBLANES - 1) // SUBLANES * SUBLANES, hi)
            b = jnp.minimum((lo + ZERO_RUN - 1) // ZERO_RUN * ZERO_RUN, hi)
            return ((lo, a - lo, 1), (a, (b - a) // SUBLANES, SUBLANES), (b, (hi - b) // ZERO_RUN, ZERO_RUN))

        def for_all_runs(act):
            def per_expert(e, carry):
                for start, count, n in ranges(e):
                    lax.fori_loop(0, count, lambda i, c, s=start, n=n: (act(zero_rows(s + i * n, n)), c)[1], 0)
                return carry

            lax.fori_loop(0, N_EXPERTS, per_expert, 0)
            n_blocks = xs_ref.shape[0] // EXPERT_TM
            lax.fori_loop(nu_ref[0], n_blocks,
                          lambda blk, c: (act(zero_rows(blk * EXPERT_TM, EXPERT_TM)), c)[1], 0)

        for_all_runs(lambda copy: copy.start())
        for_all_runs(lambda copy: copy.wait())

    @pl.when(step > 0)
    def _():
        wait_chunk(lax.rem(step + DISP_NBUF - 1, DISP_NBUF))

    @pl.when(step + 2 <= last)
    def _():
        load(step + 2, lax.rem(step + 2, DISP_NBUF)).start()

    @pl.when(step == last)
    def _():
        wait_chunk(slot)


def _dispatch(h2p, dest_flat, pad_lo, pad_hi, n_used, n_rows):
    t = h2p.shape[0]
    grid_spec = pltpu.PrefetchScalarGridSpec(
        num_scalar_prefetch=4,
        grid=(t // DISP_TD,),
        in_specs=[pl.BlockSpec(memory_space=pl.ANY)],
        out_specs=pl.BlockSpec(memory_space=pl.ANY),
        scratch_shapes=[pltpu.VMEM((DISP_NBUF, DISP_TD, D_PACK), jnp.uint32),
                        pltpu.VMEM((EXPERT_TM, D_PACK), jnp.uint32),
                        pltpu.SemaphoreType.DMA((DISP_NBUF,)), pltpu.SemaphoreType.DMA((DISP_NBUF,)),
                        pltpu.SemaphoreType.DMA(())],
    )
    return pl.pallas_call(
        _dispatch_kernel,
        grid_spec=grid_spec,
        out_shape=jax.ShapeDtypeStruct((n_rows, D_PACK), jnp.uint32),
        compiler_params=_params(("arbitrary",)),
        name="dispatch",
    )(dest_flat, pad_lo, pad_hi, n_used, h2p)


UP_TF = 1024
DOWN_TN = 2048


def _weight_index_map(n_chunks):
    def index_map(c, b, be, nu, first, nxt, full):
        del nu, full
        in_last = nxt[b] < 0
        wrap = jnp.logical_and(in_last, c + 1 < n_chunks)
        e_next = jnp.where(in_last, jnp.where(wrap, be[0], be[b]), nxt[b])
        c_next = jnp.where(wrap, c + 1, c)
        is_first = first[b] == 1
        return jnp.where(is_first, be[b], e_next), 0, jnp.where(is_first, c, c_next)

    return index_map


def _row_cases(b, nu_ref, full_ref, o_ref, compute):
    half = EXPERT_TM // 2
    used = b < nu_ref[0]

    @pl.when(jnp.logical_and(used, full_ref[b] == 1))
    def _():
        compute(EXPERT_TM)

    @pl.when(jnp.logical_and(used, full_ref[b] == 0))
    def _():
        compute(half)
        o_ref[half:, :] = jnp.zeros((half, o_ref.shape[1]), o_ref.dtype)

    @pl.when(jnp.logical_not(used))
    def _():
        o_ref[...] = jnp.zeros_like(o_ref)


def _up_kernel(be_ref, nu_ref, first_ref, nxt_ref, full_ref, x_ref, wg_ref, wu_ref, bg_ref, bu_ref, h_ref,
               wg_bf, wu_bf):
    b = pl.program_id(1)

    @pl.when(first_ref[b] == 1)
    def _():
        wg_bf[...] = wg_ref[0].astype(BF16)
        wu_bf[...] = wu_ref[0].astype(BF16)

    def compute(m):
        x_lo, x_hi = _unpack_bf16_pairs(x_ref[0:m, :])

        def proj(w_bf, bias_ref):
            return (jnp.dot(x_lo, w_bf[0:D_PACK, :], preferred_element_type=F32)
                    + jnp.dot(x_hi, w_bf[D_PACK:, :], preferred_element_type=F32) + bias_ref[0])

        g = proj(wg_bf, bg_ref)
        lin = proj(wu_bf, bu_ref)
        g = jnp.minimum(g, SWIGLU_LIMIT)
        lin = jnp.clip(lin, -SWIGLU_LIMIT, SWIGLU_LIMIT)
        act = g * jax.nn.sigmoid(SWIGLU_ALPHA * g) * (lin + 1.0)
        h_ref[0:m, :] = act.astype(BF16)

    _row_cases(b, nu_ref, full_ref, h_ref, compute)


def _up(xs, tables, w_gate, b_gate, w_up, b_up):
    n_rows = xs.shape[0]
    nb = n_rows // EXPERT_TM
    n_chunks = D_FF // UP_TF
    row = lambda f, b, be, nu, first, nxt, full: (jnp.minimum(b, nu[0] - 1), 0)
    wsel = _weight_index_map(n_chunks)
    bsel = lambda f, b, be, nu, first, nxt, full: (be[b], 0, f)
    grid_spec = pltpu.PrefetchScalarGridSpec(
        num_scalar_prefetch=5,
        grid=(n_chunks, nb),
        in_specs=[
            pl.BlockSpec((EXPERT_TM, D_PACK), row),
            pl.BlockSpec((1, D_MODEL, UP_TF), wsel),
            pl.BlockSpec((1, D_MODEL, UP_TF), wsel),
            pl.BlockSpec((1, 1, UP_TF), bsel),
            pl.BlockSpec((1, 1, UP_TF), bsel),
        ],
        out_specs=pl.BlockSpec((EXPERT_TM, UP_TF), lambda f, b, be, nu, first, nxt, full: (b, f)),
        scratch_shapes=[pltpu.VMEM((D_MODEL, UP_TF), BF16), pltpu.VMEM((D_MODEL, UP_TF), BF16)],
    )
    return pl.pallas_call(
        _up_kernel,
        grid_spec=grid_spec,
        out_shape=jax.ShapeDtypeStruct((n_rows, D_FF), BF16),
        compiler_params=_params(("arbitrary", "arbitrary"), EXPERT_VMEM_LIMIT),
        name="expert_up",
    )(*tables, xs, w_gate, w_up,
      b_gate.reshape(N_EXPERTS, 1, D_FF), b_up.reshape(N_EXPERTS, 1, D_FF))


def _down_kernel(be_ref, nu_ref, first_ref, nxt_ref, full_ref, h_ref, wd_ref, bd_ref, y_ref, wd_bf):
    b = pl.program_id(1)

    @pl.when(first_ref[b] == 1)
    def _():
        wd_bf[...] = wd_ref[0].astype(BF16)

    def compute(m):
        y = jnp.dot(h_ref[0:m, :], wd_bf[...], preferred_element_type=F32) + bd_ref[0]
        y_ref[0:m, :] = _pack_bf16_pairs(y)

    _row_cases(b, nu_ref, full_ref, y_ref, compute)


def _down(hs, tables, w_down, b_down):
    n_rows = hs.shape[0]
    nb = n_rows // EXPERT_TM
    n_chunks = D_MODEL // DOWN_TN
    assert n_chunks == 1
    row = lambda n, b, be, nu, first, nxt, full: (jnp.minimum(b, nu[0] - 1), 0)
    grid_spec = pltpu.PrefetchScalarGridSpec(
        num_scalar_prefetch=5,
        grid=(n_chunks, nb),
        in_specs=[
            pl.BlockSpec((EXPERT_TM, D_FF), row),
            pl.BlockSpec((1, D_FF, DOWN_TN), _weight_index_map(n_chunks)),
            pl.BlockSpec((1, 1, DOWN_TN), lambda n, b, be, nu, first, nxt, full: (be[b], 0, n)),
        ],
        out_specs=pl.BlockSpec((EXPERT_TM, D_PACK), lambda n, b, be, nu, first, nxt, full: (b, 0)),
        scratch_shapes=[pltpu.VMEM((D_FF, DOWN_TN), BF16)],
    )
    return pl.pallas_call(
        _down_kernel,
        grid_spec=grid_spec,
        out_shape=jax.ShapeDtypeStruct((n_rows, D_PACK), jnp.uint32),
        compiler_params=_params(("arbitrary", "arbitrary"), EXPERT_VMEM_LIMIT),
        name="expert_down",
    )(*tables, hs, w_down, b_down.reshape(N_EXPERTS, 1, D_MODEL))


COMB_TC = 256


def _combine_kernel(dest_ref, xm_ref, wt_ref, g2_ref, ys_ref, o_ref, buf_ref, sems):
    step = pl.program_id(0)
    last = pl.num_programs(0) - 1
    slot = step % 2

    def gather_tile(tile, s):
        def issue(r, carry):
            for k in range(TOP_K):
                d = dest_ref[(tile * COMB_TC + r) * TOP_K + k]
                pltpu.make_async_copy(ys_ref.at[pl.ds(d, 1), :], buf_ref.at[s, pl.ds(k * COMB_TC + r, 1), :],
                                      sems.at[s]).start(priority=k % 2)
            return carry

        lax.fori_loop(0, COMB_TC, issue, 0, unroll=2)

    @pl.when(step == 0)
    def _():
        gather_tile(0, 0)

    @pl.when(step < last)
    def _():
        gather_tile(step + 1, 1 - slot)

    pltpu.make_async_copy(ys_ref.at[pl.ds(0, TOP_K * COMB_TC), :], buf_ref.at[slot], sems.at[slot]).wait()

    wt = wt_ref[...]
    acc_lo = acc_hi = None
    for k in range(TOP_K):
        lo, hi = _unpack_pairs_f32(buf_ref[slot, k * COMB_TC:(k + 1) * COMB_TC, :])
        w = wt[:, k:k + 1]
        acc_lo = w * lo if acc_lo is None else acc_lo + w * lo
        acc_hi = w * hi if acc_hi is None else acc_hi + w * hi
    o_ref[:, :D_PACK] = xm_ref[:, :D_PACK] + g2_ref[:, :D_PACK] * acc_lo
    o_ref[:, D_PACK:] = xm_ref[:, D_PACK:] + g2_ref[:, D_PACK:] * acc_hi


def _combine(ys, dest_flat, xm, wts, gate2):
    t = xm.shape[0]
    grid_spec = pltpu.PrefetchScalarGridSpec(
        num_scalar_prefetch=1,
        grid=(t // COMB_TC,),
        in_specs=[
            pl.BlockSpec((COMB_TC, D_MODEL), lambda i, d: (i, 0)),
            pl.BlockSpec((COMB_TC, LANES), lambda i, d: (i, 0)),
            pl.BlockSpec((1, D_MODEL), lambda i, d: (0, 0)),
            pl.BlockSpec(memory_space=pl.ANY),
        ],
        out_specs=pl.BlockSpec((COMB_TC, D_MODEL), lambda i, d: (i, 0)),
        scratch_shapes=[pltpu.VMEM((2, TOP_K * COMB_TC, D_PACK), jnp.uint32), pltpu.SemaphoreType.DMA((2,))],
    )
    return pl.pallas_call(
        _combine_kernel,
        grid_spec=grid_spec,
        out_shape=jax.ShapeDtypeStruct((t, D_MODEL), F32),
        compiler_params=_params(("arbitrary",)),
        name="combine",
    )(dest_flat, xm, wts, gate2, ys)


def _routing_tables(idx, rank, counts, n_tok):
    padded = ((counts + EXPERT_TM - 1) // EXPERT_TM) * EXPERT_TM
    pend = jnp.cumsum(padded)
    pstart = pend - padded
    experts = jnp.arange(N_EXPERTS, dtype=jnp.int32)
    start_of = jnp.sum(jnp.where(idx[..., None] == experts, pstart, 0), axis=-1)
    dest = (start_of + rank).astype(jnp.int32).reshape(-1)
    n_rows = n_tok * TOP_K + N_EXPERTS * EXPERT_TM
    nb = n_rows // EXPERT_TM
    block_start = jnp.arange(nb, dtype=jnp.int32) * EXPERT_TM
    block_expert = jnp.minimum(jnp.sum(pend[None, :] <= block_start[:, None], axis=1),
                               N_EXPERTS - 1).astype(jnp.int32)
    n_used = (pend[-1:] // EXPERT_TM).astype(jnp.int32)
    pad_lo = (pstart + counts).astype(jnp.int32)
    pad_hi = pend.astype(jnp.int32)

    blocks = jnp.arange(nb, dtype=jnp.int32)
    block_expert = jnp.where(blocks < n_used[0], block_expert, block_expert[jnp.maximum(n_used[0] - 1, 0)])
    prev = jnp.concatenate([block_expert[:1], block_expert[:-1]])
    first = jnp.logical_or(blocks == 0, block_expert != prev)
    first_pos = jnp.where(first, blocks, nb)
    next_first = jnp.flip(lax.cummin(jnp.flip(jnp.concatenate([first_pos[1:], jnp.full((1,), nb, jnp.int32)]))))
    nxt = jnp.where(next_first < nb, block_expert[jnp.minimum(next_first, nb - 1)], -1).astype(jnp.int32)
    rows_end = jnp.sum(jnp.where(block_expert[:, None] == experts, pad_lo, 0), axis=-1)
    full = (rows_end > block_start + EXPERT_TM // 2).astype(jnp.int32)
    tables = (block_expert, n_used, first.astype(jnp.int32), nxt, full)
    return dest, tables, pad_lo, pad_hi, n_rows


def _layer(x2, c, w_ada, b_ada, g_norm1, w_in, b_in, g_q, g_k, sinks, rel_bias, w_dw, b_dw, ln_g, ln_b,
           g_out_attn, g_out_conv, w_out, b_out, g_norm2, w_router, b_router,
           w_gate, b_gate, w_up, b_up, w_down, b_down):
    t = x2.shape[0]
    row = lambda v: v.reshape(1, -1)
    mod = _ada(c, w_ada, b_ada)
    shift1, scale1, gate1, shift2, scale2, gate2 = [mod[:, i * D_MODEL:(i + 1) * D_MODEL] for i in range(6)]

    q, kv, uc = _in_proj(x2, row(g_norm1), scale1, shift1, w_in.astype(BF16), row(b_in))

    gq_t = row(jnp.tile(g_q, N_Q_HEADS)) * (HEAD_DIM ** -0.5)
    gk_t = row(jnp.tile(g_k, N_KV_HEADS))
    ma = _attention(q, kv, _bias_table(rel_bias), sinks, gq_t, gk_t, row(g_out_attn),
                    _head_indicator(D_ATTN), _head_indicator(LANES))
    w_dw_p = jnp.concatenate([w_dw, jnp.zeros((HALO - CONV_WIDTH, D_CONV), w_dw.dtype)], axis=0)
    mc = _conv(uc, w_dw_p, row(b_dw), row(ln_g), row(ln_b), row(g_out_conv))

    xm, h2, idx, rank, wts, counts = _out_proj(
        ma, mc, x2, w_out.astype(BF16), row(b_out), gate1, row(g_norm2), scale2, shift2,
        w_router, row(b_router))

    dest, tables, pad_lo, pad_hi, n_rows = _routing_tables(idx[:, :TOP_K], rank[:, :TOP_K], counts[0], t)
    xs = _dispatch(h2, dest, pad_lo, pad_hi, tables[1], n_rows)
    hs = _up(xs, tables, w_gate, b_gate, w_up, b_up)
    ys = _down(hs, tables, w_down, b_down)
    return _combine(ys, dest, xm, wts, gate2)


def kernel(x, c, w_ada, b_ada, g_norm1, w_in, b_in, g_q, g_k, sinks, rel_bias, w_dw, b_dw, ln_g, ln_b,
           g_out_attn, g_out_conv, w_out, b_out, g_norm2, w_router, b_router,
           w_gate, b_gate, w_up, b_up, w_down, b_down):
    b, t, d = x.shape
    assert b == 1 and d == D_MODEL and w_ada.shape[0] == 1
    out = _layer(x.reshape(t, d), c, w_ada[0], b_ada[0], g_norm1[0], w_in[0], b_in[0], g_q[0], g_k[0],
                 sinks[0], rel_bias, w_dw[0], b_dw[0], ln_g[0], ln_b[0], g_out_attn[0], g_out_conv[0],
                 w_out[0], b_out[0], g_norm2[0], w_router[0], b_router[0],
                 w_gate[0], b_gate[0], w_up[0], b_up[0], w_down[0], b_down[0])
    return out.reshape(b, t, d)
```

```python
import functools
import math

import jax
import jax.numpy as jnp
from jax import lax
from jax.experimental import pallas as pl
from jax.experimental.pallas import tpu as pltpu

D_MODEL = 2048
HEAD_DIM = 64
N_Q_HEADS = 16
N_KV_HEADS = 2
D_ATTN = N_Q_HEADS * HEAD_DIM
D_KV = N_KV_HEADS * HEAD_DIM
D_CONV = D_MODEL - D_ATTN
D_IN = D_ATTN + 2 * D_KV + 2 * D_CONV
WINDOW = 128
BLOCK = 128
CONV_WIDTH = 31
N_BUCKETS = 32
MAX_DISTANCE = 128
N_EXPERTS = 32
TOP_K = 4
D_FF = D_MODEL
SWIGLU_LIMIT = 7.0
SWIGLU_ALPHA = 1.702
EPS = 1e-6
NEG_INF = -1e30

LANES = 128
SUBLANES = 8
VMEM_LIMIT = 56 * 1024 * 1024

HALO = 32
EXPERT_TM = 512
EXPERT_VMEM_LIMIT = 60 * 1024 * 1024
F32 = jnp.float32
BF16 = jnp.bfloat16


def _params(sem, vmem_limit=VMEM_LIMIT):
    return pltpu.CompilerParams(dimension_semantics=sem, vmem_limit_bytes=vmem_limit)


ADA_TN = 1024
ADA_RC = 256


def _ada_kernel(c_ref, w_ref, b_ref, o_ref):
    tn = w_ref.shape[1]

    def body(i, acc):
        r = pl.multiple_of(i * ADA_RC, ADA_RC)
        c = c_ref[pl.ds(r, ADA_RC), :]
        sc = c * jax.nn.sigmoid(c)
        prod = w_ref[pl.ds(r, ADA_RC), :] * sc
        return acc + jnp.sum(prod.reshape(ADA_RC // 8, 8, tn), axis=0)

    acc = lax.fori_loop(0, D_MODEL // ADA_RC, body, jnp.zeros((8, tn), F32))
    o_ref[...] = jnp.sum(acc, axis=0, keepdims=True) + b_ref[...]


def _ada(c, w_ada, b_ada):
    n = w_ada.shape[1]
    return pl.pallas_call(
        _ada_kernel,
        grid=(n // ADA_TN,),
        in_specs=[
            pl.BlockSpec((D_MODEL, 1), lambda j: (0, 0)),
            pl.BlockSpec((D_MODEL, ADA_TN), lambda j: (0, j)),
            pl.BlockSpec((1, ADA_TN), lambda j: (0, j)),
        ],
        out_specs=pl.BlockSpec((1, ADA_TN), lambda j: (0, j)),
        out_shape=jax.ShapeDtypeStruct((1, n), F32),
        compiler_params=_params(("parallel",)),
        name="ada",
    )(c.reshape(D_MODEL, 1), w_ada, b_ada.reshape(1, n))


IN_TM = 512
IN_NC = 256


def _modulated_rms(x, g, scale, shift):
    ms = jnp.mean(x * x, axis=-1, keepdims=True)
    return (x * lax.rsqrt(ms + EPS) * g) * (1.0 + scale) + shift


def _in_kernel(x_ref, g_ref, sc_ref, sh_ref, w_ref, b_ref, q_ref, kv_ref, uc_ref):
    h = _modulated_rms(x_ref[...], g_ref[...], sc_ref[...], sh_ref[...]).astype(BF16)

    def proj(lo, n, o_ref):
        for c in range(0, n, IN_NC):
            u = jnp.dot(h, w_ref[:, lo + c:lo + c + IN_NC], preferred_element_type=F32)
            o_ref[:, c:c + IN_NC] = u + b_ref[:, lo + c:lo + c + IN_NC]

    proj(0, D_ATTN, q_ref)
    proj(D_ATTN, 2 * D_KV, kv_ref)
    proj(D_ATTN + 2 * D_KV, 2 * D_CONV, uc_ref)


def _in_proj(x2, g1, scale1, shift1, w_in_bf, b_in):
    t = x2.shape[0]
    vec = lambda n: pl.BlockSpec((1, n), lambda i: (0, 0))
    return pl.pallas_call(
        _in_kernel,
        grid=(t // IN_TM,),
        in_specs=[
            pl.BlockSpec((IN_TM, D_MODEL), lambda i: (i, 0)),
            vec(D_MODEL), vec(D_MODEL), vec(D_MODEL),
            pl.BlockSpec((D_MODEL, D_IN), lambda i: (0, 0)),
            vec(D_IN),
        ],
        out_specs=[
            pl.BlockSpec((IN_TM, D_ATTN), lambda i: (i, 0)),
            pl.BlockSpec((IN_TM, 2 * D_KV), lambda i: (i, 0)),
            pl.BlockSpec((IN_TM, 2 * D_CONV), lambda i: (i, 0)),
        ],
        out_shape=[
            jax.ShapeDtypeStruct((t, D_ATTN), F32),
            jax.ShapeDtypeStruct((t, 2 * D_KV), F32),
            jax.ShapeDtypeStruct((t, 2 * D_CONV), F32),
        ],
        compiler_params=_params(("parallel",)),
        name="in_proj",
    )(x2, g1, scale1, shift1, w_in_bf, b_in)


ATT_R = 4


def _split_dot(a, b_bf):
    hi = a.astype(BF16)
    lo = (a - hi.astype(F32)).astype(BF16)
    return (jnp.dot(hi, b_bf, preferred_element_type=F32)
            + jnp.dot(lo, b_bf, preferred_element_type=F32))


def _attn_kernel(sinks_ref, q_ref, kvp_ref, kvc_ref, bias_ref, gq_ref, gk_ref, go_ref,
                 hq_ref, hqt_ref, hk_ref, o_ref, y_ref):
    step = pl.program_id(0)
    lane = lax.broadcasted_iota(jnp.int32, (2 * BLOCK, LANES), 1)
    low = lane < HEAD_DIM
    col = lax.broadcasted_iota(jnp.int32, (BLOCK, 2 * BLOCK), 1)
    first_lo = jnp.where(step == 0, BLOCK, 0)

    for r in range(ATT_R):
        rows = slice(r * BLOCK, (r + 1) * BLOCK)
        q = q_ref[rows, :]
        ssq = _split_dot(_split_dot(q * q, hq_ref[...]), hqt_ref[...])
        qn = (q * lax.rsqrt(ssq * (1.0 / HEAD_DIM) + EPS) * gq_ref[...]).astype(BF16)

        if r == 0:
            kv_prev = kvp_ref[...]
        else:
            kv_prev = kvc_ref[(r - 1) * BLOCK:r * BLOCK, :]
        kv = jnp.concatenate([kv_prev, kvc_ref[rows, :]], axis=0)
        k = kv[:, :LANES]
        v = kv[:, LANES:]
        kss = _split_dot(k * k, hk_ref[...])
        kn = k * lax.rsqrt(kss * (1.0 / HEAD_DIM) + EPS) * gk_ref[...]
        kn_sw = pltpu.roll(kn, HEAD_DIM, axis=1)
        v_sw = pltpu.roll(v, HEAD_DIM, axis=1)
        zero = jnp.zeros_like(kn)
        k_lo = [jnp.where(low, kn, zero).astype(BF16), jnp.where(low, kn_sw, zero).astype(BF16)]
        k_hi = [jnp.where(low, zero, kn_sw).astype(BF16), jnp.where(low, zero, kn).astype(BF16)]
        v_lo = [jnp.where(low, v, zero).astype(BF16), jnp.where(low, v_sw, zero).astype(BF16)]
        v_hi = [jnp.where(low, zero, v_sw).astype(BF16), jnp.where(low, zero, v).astype(BF16)]

        for p in range(N_Q_HEADS // 2):
            g = (2 * p) // (N_Q_HEADS // N_KV_HEADS)
            qp = qn[:, p * LANES:(p + 1) * LANES]
            acc = None
            for half, (kz, vz) in enumerate(((k_lo[g], v_lo[g]), (k_hi[g], v_hi[g]))):
                h = 2 * p + half
                s = lax.dot_general(qp, kz, (((1,), (1,)), ((), ())), preferred_element_type=F32)
                s = s + bias_ref[h]
                if r == 0:
                    s = jnp.where(col >= first_lo, s, NEG_INF)
                sink = sinks_ref[h]
                m = jnp.maximum(jnp.max(s, axis=-1, keepdims=True), sink)
                e = jnp.exp(s - m)
                denom = jnp.sum(e, axis=-1, keepdims=True) + jnp.exp(sink - m)
                pv = jnp.dot(e.astype(BF16), vz, preferred_element_type=F32)
                pv = pv * (1.0 / denom)
                acc = pv if acc is None else acc + pv
            y_ref[:, p * LANES:(p + 1) * LANES] = acc

        y = y_ref[...]
        ms = jnp.mean(y * y, axis=-1, keepdims=True)
        o_ref[rows, :] = (y * lax.rsqrt(ms + EPS) * go_ref[...]).astype(BF16)


def _attention(q, kv, bias, sinks, gq_t, gk_t, g_out):
    t = q.shape[0]
    tile = ATT_R * BLOCK
    head_of_lane = jnp.arange(D_ATTN) // HEAD_DIM
    hq = (head_of_lane[:, None] == jnp.arange(LANES)[None, :]).astype(BF16)
    hk = _head_indicator(LANES)
    grid_spec = pltpu.PrefetchScalarGridSpec(
        num_scalar_prefetch=0,
        grid=(t // tile,),
        in_specs=[
            pl.BlockSpec(memory_space=pltpu.SMEM),
            pl.BlockSpec((tile, D_ATTN), lambda i: (i, 0)),
            pl.BlockSpec((BLOCK, 2 * D_KV), lambda i: (jnp.maximum(i * ATT_R - 1, 0), 0)),
            pl.BlockSpec((tile, 2 * D_KV), lambda i: (i, 0)),
            pl.BlockSpec((N_Q_HEADS, BLOCK, 2 * BLOCK), lambda i: (0, 0, 0)),
            pl.BlockSpec((1, D_ATTN), lambda i: (0, 0)),
            pl.BlockSpec((1, LANES), lambda i: (0, 0)),
            pl.BlockSpec((1, D_ATTN), lambda i: (0, 0)),
            pl.BlockSpec((D_ATTN, LANES), lambda i: (0, 0)),
            pl.BlockSpec((LANES, D_ATTN), lambda i: (0, 0)),
            pl.BlockSpec((LANES, LANES), lambda i: (0, 0)),
        ],
        out_specs=pl.BlockSpec((tile, D_ATTN), lambda i: (i, 0)),
        scratch_shapes=[pltpu.VMEM((BLOCK, D_ATTN), F32)],
    )
    return pl.pallas_call(
        _attn_kernel,
        grid_spec=grid_spec,
        out_shape=jax.ShapeDtypeStruct((t, D_ATTN), BF16),
        compiler_params=_params(("parallel",)),
        name="attn",
    )(sinks, q, kv, kv, bias, gq_t, gk_t, g_out, hq, hq.T, hk)


def _t5_bucket(dist):
    max_exact = N_BUCKETS // 2
    d = jnp.maximum(dist, 0)
    log_ratio = jnp.log(jnp.maximum(d, max_exact).astype(F32) / max_exact)
    large = max_exact + (log_ratio / math.log(MAX_DISTANCE / max_exact)
                         * (N_BUCKETS - max_exact)).astype(jnp.int32)
    large = jnp.minimum(large, N_BUCKETS - 1)
    return jnp.where(d < max_exact, d, large)


def _bias_table(rel_bias):
    q_local = jnp.arange(BLOCK, dtype=jnp.int32) + BLOCK
    k_local = jnp.arange(2 * BLOCK, dtype=jnp.int32)
    dist = q_local[:, None] - k_local[None, :]
    band = (dist >= 0) & (dist < WINDOW)
    bucket = _t5_bucket(dist)
    hit = (jnp.arange(N_BUCKETS, dtype=jnp.int32)[:, None] == bucket.reshape(1, -1)).astype(F32)
    bias = jnp.dot(rel_bias.astype(F32).T, hit, precision=lax.Precision.HIGHEST)
    return jnp.where(band[None], bias.reshape(N_Q_HEADS, BLOCK, 2 * BLOCK), NEG_INF)


def _head_indicator(n):
    i = jnp.arange(n) // HEAD_DIM
    return (i[:, None] == i[None, :]).astype(BF16)


CONV_TT = 512
CONV_RC = 64
CONV_CC = 256


def _conv_kernel(u_ref, halo_ref, w_ref, b_ref, lg_ref, lb_ref, go_ref, o_ref, h_ref, y_ref):
    step = pl.program_id(0)

    def glu(u):
        return u[:, :D_CONV] * jax.nn.sigmoid(u[:, D_CONV:])

    hh = glu(halo_ref[...])
    h_ref[0, 0:HALO, :] = jnp.where(step == 0, jnp.zeros_like(hh), hh)
    h_ref[0, HALO:, :] = glu(u_ref[...])

    n_shifted = CONV_TT + HALO - SUBLANES
    for s in range(1, SUBLANES):
        for c0 in range(0, D_CONV, CONV_CC):
            cs = slice(c0, c0 + CONV_CC)
            h_ref[s, 0:n_shifted, cs] = h_ref[0, s:s + n_shifted, cs]

    off = HALO - (CONV_WIDTH - 1)
    for r0 in range(0, CONV_TT, CONV_RC):
        for c0 in range(0, D_CONV, CONV_CC):
            cs = slice(c0, c0 + CONV_CC)
            acc = jnp.broadcast_to(b_ref[:, cs], (CONV_RC, CONV_CC))
            for j in range(CONV_WIDTH):
                s = (r0 + off + j) % SUBLANES
                a = r0 + off + j - s
                acc = acc + w_ref[j:j + 1, cs] * h_ref[s, a:a + CONV_RC, cs]
            y_ref[r0:r0 + CONV_RC, cs] = acc

    y = y_ref[...]
    mu = jnp.mean(y, axis=-1, keepdims=True)
    yc = y - mu
    var = jnp.mean(yc * yc, axis=-1, keepdims=True)
    z = yc * lax.rsqrt(var + EPS) * lg_ref[...] + lb_ref[...]
    s = z * jax.nn.sigmoid(z)
    ms = jnp.mean(s * s, axis=-1, keepdims=True)
    o_ref[...] = (s * lax.rsqrt(ms + EPS) * go_ref[...]).astype(BF16)


def _conv(uc, w_dw, b_dw, ln_g, ln_b, g_out):
    t = uc.shape[0]
    vec = lambda: pl.BlockSpec((1, D_CONV), lambda i: (0, 0))
    per = CONV_TT // HALO
    return pl.pallas_call(
        _conv_kernel,
        grid=(t // CONV_TT,),
        in_specs=[
            pl.BlockSpec((CONV_TT, 2 * D_CONV), lambda i: (i, 0)),
            pl.BlockSpec((HALO, 2 * D_CONV), lambda i: (jnp.maximum(i * per - 1, 0), 0)),
            pl.BlockSpec((HALO, D_CONV), lambda i: (0, 0)),
            vec(), vec(), vec(), vec(),
        ],
        out_specs=pl.BlockSpec((CONV_TT, D_CONV), lambda i: (i, 0)),
        out_shape=jax.ShapeDtypeStruct((t, D_CONV), BF16),
        scratch_shapes=[pltpu.VMEM((SUBLANES, CONV_TT + HALO, D_CONV), F32),
                        pltpu.VMEM((CONV_TT, D_CONV), F32)],
        compiler_params=_params(("parallel",)),
        name="conv",
    )(uc, uc, w_dw, b_dw, ln_g, ln_b, g_out)


OUT_TM = 512
D_PACK = D_MODEL // 2
HI_MASK = 0xFFFF0000


def _pack_bf16_pairs(h):
    lo = lax.bitcast_convert_type(h[:, :D_PACK].astype(BF16).astype(F32), jnp.uint32)
    hi = lax.bitcast_convert_type(h[:, D_PACK:].astype(BF16).astype(F32), jnp.uint32)
    return (lo >> 16) | (hi & jnp.uint32(HI_MASK))


def _unpack_pairs_f32(w):
    lo = lax.bitcast_convert_type(w << 16, F32)
    hi = lax.bitcast_convert_type(w & jnp.uint32(HI_MASK), F32)
    return lo, hi


def _unpack_bf16_pairs(w):
    lo, hi = _unpack_pairs_f32(w)
    return lo.astype(BF16), hi.astype(BF16)


def _out_kernel(ma_ref, mc_ref, x_ref, w_ref, bo_ref, g1_ref, g2_ref, sc_ref, sh_ref, wr_ref, br_ref,
                tri_ref, xm_ref, h2_ref, idx_ref, rank_ref, wt_ref, cnt_ref, carry_ref):
    step = pl.program_id(0)

    @pl.when(step == 0)
    def _():
        carry_ref[...] = jnp.zeros_like(carry_ref)

    y = (jnp.dot(ma_ref[...], w_ref[0:D_ATTN, :], preferred_element_type=F32)
         + jnp.dot(mc_ref[...], w_ref[D_ATTN:, :], preferred_element_type=F32) + bo_ref[...])
    xm = x_ref[...] + g1_ref[...] * y
    xm_ref[...] = xm
    h2 = _modulated_rms(xm, g2_ref[...], sc_ref[...], sh_ref[...])
    h2_ref[...] = _pack_bf16_pairs(h2)

    both = jnp.dot(h2.astype(BF16), wr_ref[...], preferred_element_type=F32)
    logits = both[:, :N_EXPERTS] + both[:, N_EXPERTS:] + br_ref[...]
    tm = logits.shape[0]
    lane = lax.broadcasted_iota(jnp.int32, (tm, N_EXPERTS), 1).astype(F32)
    vals, idxs = [], []
    l = logits
    for _ in range(TOP_K):
        m = jnp.max(l, axis=-1, keepdims=True)
        i = jnp.min(jnp.where(l == m, lane, float(N_EXPERTS)), axis=-1, keepdims=True)
        vals.append(m)
        idxs.append(i)
        l = jnp.where(lane == i, -jnp.inf, l)
    es = [jnp.exp(v - vals[0]) for v in vals]
    tot = es[0] + es[1] + es[2] + es[3]
    ws = [e / tot for e in es]

    hot = [(lane == i).astype(F32) for i in idxs]
    hot_all = hot[0] + hot[1] + hot[2] + hot[3]
    before = jnp.dot(tri_ref[...], hot_all.astype(BF16), preferred_element_type=F32) + carry_ref[...]
    ranks = [jnp.sum(h * before, axis=-1, keepdims=True) for h in hot]
    carry_ref[...] = carry_ref[...] + jnp.sum(hot_all, axis=0, keepdims=True)
    cnt_ref[...] = carry_ref[...].astype(jnp.int32)

    slot = lax.broadcasted_iota(jnp.int32, (tm, LANES), 1)

    def pack(cols):
        out = jnp.zeros((tm, LANES), F32)
        for k in range(TOP_K):
            out = jnp.where(slot == k, cols[k], out)
        return out

    idx_ref[...] = pack(idxs).astype(jnp.int32)
    rank_ref[...] = pack(ranks).astype(jnp.int32)
    wt_ref[...] = pack(ws)


def _out_proj(ma, mc, x2, w_out_bf, b_out, gate1, g2, scale2, shift2, w_router, b_router):
    t = x2.shape[0]
    vec = lambda n: pl.BlockSpec((1, n), lambda i: (0, 0))
    tri = jnp.tril(jnp.ones((OUT_TM, OUT_TM), F32), -1).astype(BF16)
    wr_hi = w_router.astype(BF16)
    wr_lo = (w_router - wr_hi.astype(F32)).astype(BF16)
    w_router = jnp.concatenate([wr_hi, wr_lo], axis=1)
    return pl.pallas_call(
        _out_kernel,
        grid=(t // OUT_TM,),
        in_specs=[
            pl.BlockSpec((OUT_TM, D_ATTN), lambda i: (i, 0)),
            pl.BlockSpec((OUT_TM, D_CONV), lambda i: (i, 0)),
            pl.BlockSpec((OUT_TM, D_MODEL), lambda i: (i, 0)),
            pl.BlockSpec((D_MODEL, D_MODEL), lambda i: (0, 0)),
            vec(D_MODEL), vec(D_MODEL), vec(D_MODEL), vec(D_MODEL), vec(D_MODEL),
            pl.BlockSpec((D_MODEL, 2 * N_EXPERTS), lambda i: (0, 0)),
            vec(N_EXPERTS),
            pl.BlockSpec((OUT_TM, OUT_TM), lambda i: (0, 0)),
        ],
        out_specs=[
            pl.BlockSpec((OUT_TM, D_MODEL), lambda i: (i, 0)),
            pl.BlockSpec((OUT_TM, D_PACK), lambda i: (i, 0)),
            pl.BlockSpec((OUT_TM, LANES), lambda i: (i, 0)),
            pl.BlockSpec((OUT_TM, LANES), lambda i: (i, 0)),
            pl.BlockSpec((OUT_TM, LANES), lambda i: (i, 0)),
            pl.BlockSpec((1, N_EXPERTS), lambda i: (0, 0)),
        ],
        out_shape=[
            jax.ShapeDtypeStruct((t, D_MODEL), F32),
            jax.ShapeDtypeStruct((t, D_PACK), jnp.uint32),
            jax.ShapeDtypeStruct((t, LANES), jnp.int32),
            jax.ShapeDtypeStruct((t, LANES), jnp.int32),
            jax.ShapeDtypeStruct((t, LANES), F32),
            jax.ShapeDtypeStruct((1, N_EXPERTS), jnp.int32),
        ],
        scratch_shapes=[pltpu.VMEM((1, N_EXPERTS), F32)],
        compiler_params=_params(("arbitrary",)),
        name="out_proj",
    )(ma, mc, x2, w_out_bf, b_out, gate1, g2, scale2, shift2, w_router, b_router, tri)


DISP_TD = 256
DISP_NBUF = 3
ZERO_RUN = 64


def _dispatch_kernel(dest_ref, pad_lo_ref, pad_hi_ref, nu_ref, h2_ref, xs_ref,
                     stage_ref, zero_ref, in_sems, out_sems, zsem):
    step = pl.program_id(0)
    last = pl.num_programs(0) - 1
    slot = lax.rem(step, DISP_NBUF)
    rows_per_chunk = DISP_TD * TOP_K

    def load(chunk, s):
        src = h2_ref.at[pl.ds(pl.multiple_of(chunk * DISP_TD, DISP_TD), DISP_TD), :]
        return pltpu.make_async_copy(src, stage_ref.at[s], in_sems.at[s])

    def wait_chunk(s):
        pltpu.make_async_copy(h2_ref.at[pl.ds(0, rows_per_chunk), :], xs_ref.at[pl.ds(0, rows_per_chunk), :],
                              out_sems.at[s]).wait()

    @pl.when(step == 0)
    def _():
        load(0, 0).start()

        @pl.when(last >= 1)
        def _():
            load(1, 1).start()

    load(step, slot).wait()

    def issue(r, carry):
        for k in range(TOP_K):
            d = dest_ref[(step * DISP_TD + r) * TOP_K + k]
            pltpu.make_async_copy(stage_ref.at[slot, pl.ds(r, 1), :], xs_ref.at[pl.ds(d, 1), :],
                                  out_sems.at[slot]).start(priority=k % 2)
        return carry

    lax.fori_loop(0, DISP_TD, issue, 0, unroll=2)

    @pl.when(step == 0)
    def _():
        zero_ref[...] = jnp.zeros_like(zero_ref)

        def zero_rows(d, n):
            d = d if n == 1 else pl.multiple_of(d, n)
            return pltpu.make_async_copy(zero_ref.at[pl.ds(0, n), :], xs_ref.at[pl.ds(d, n), :], zsem)

        def ranges(e):
            lo, hi = pad_lo_ref[e], pad_hi_ref[e]
            a = jnp.minimum((lo + SUBLANES - 1) // SUBLANES * SUBLANES, hi)
            b = jnp.minimum((lo + ZERO_RUN - 1) // ZERO_RUN * ZERO_RUN, hi)
            return ((lo, a - lo, 1), (a, (b - a) // SUBLANES, SUBLANES), (b, (hi - b) // ZERO_RUN, ZERO_RUN))

        def for_all_runs(act):
            def per_expert(e, carry):
                for start, count, n in ranges(e):
                    lax.fori_loop(0, count, lambda i, c, s=start, n=n: (act(zero_rows(s + i * n, n)), c)[1], 0)
                return carry

            lax.fori_loop(0, N_EXPERTS, per_expert, 0)
            n_blocks = xs_ref.shape[0] // EXPERT_TM
            lax.fori_loop(nu_ref[0], n_blocks,
                          lambda blk, c: (act(zero_rows(blk * EXPERT_TM, EXPERT_TM)), c)[1], 0)

        for_all_runs(lambda copy: copy.start())
        for_all_runs(lambda copy: copy.wait())

    @pl.when(step > 0)
    def _():
        wait_chunk(lax.rem(step + DISP_NBUF - 1, DISP_NBUF))

    @pl.when(step + 2 <= last)
    def _():
        load(step + 2, lax.rem(step + 2, DISP_NBUF)).start()

    @pl.when(step == last)
    def _():
        wait_chunk(slot)


def _dispatch(h2p, dest_flat, pad_lo, pad_hi, n_used, n_rows):
    t = h2p.shape[0]
    grid_spec = pltpu.PrefetchScalarGridSpec(
        num_scalar_prefetch=4,
        grid=(t // DISP_TD,),
        in_specs=[pl.BlockSpec(memory_space=pl.ANY)],
        out_specs=pl.BlockSpec(memory_space=pl.ANY),
        scratch_shapes=[pltpu.VMEM((DISP_NBUF, DISP_TD, D_PACK), jnp.uint32),
                        pltpu.VMEM((EXPERT_TM, D_PACK), jnp.uint32),
                        pltpu.SemaphoreType.DMA((DISP_NBUF,)), pltpu.SemaphoreType.DMA((DISP_NBUF,)),
                        pltpu.SemaphoreType.DMA(())],
    )
    return pl.pallas_call(
        _dispatch_kernel,
        grid_spec=grid_spec,
        out_shape=jax.ShapeDtypeStruct((n_rows, D_PACK), jnp.uint32),
        compiler_params=_params(("arbitrary",)),
        name="dispatch",
    )(dest_flat, pad_lo, pad_hi, n_used, h2p)


UP_TF = 1024
DOWN_TN = 2048


def _weight_index_map(n_chunks):
    def index_map(c, b, be, nu, first, nxt, full):
        del nu, full
        in_last = nxt[b] < 0
        wrap = jnp.logical_and(in_last, c + 1 < n_chunks)
        e_next = jnp.where(in_last, jnp.where(wrap, be[0], be[b]), nxt[b])
        c_next = jnp.where(wrap, c + 1, c)
        is_first = first[b] == 1
        return jnp.where(is_first, be[b], e_next), 0, jnp.where(is_first, c, c_next)

    return index_map


def _row_cases(b, nu_ref, full_ref, o_ref, compute):
    half = EXPERT_TM // 2
    used = b < nu_ref[0]

    @pl.when(jnp.logical_and(used, full_ref[b] == 1))
    def _():
        compute(EXPERT_TM)

    @pl.when(jnp.logical_and(used, full_ref[b] == 0))
    def _():
        compute(half)
        o_ref[half:, :] = jnp.zeros((half, o_ref.shape[1]), o_ref.dtype)

    @pl.when(jnp.logical_not(used))
    def _():
        o_ref[...] = jnp.zeros_like(o_ref)


def _up_kernel(be_ref, nu_ref, first_ref, nxt_ref, full_ref, x_ref, wg_ref, wu_ref, bg_ref, bu_ref, h_ref,
               wg_bf, wu_bf):
    b = pl.program_id(1)

    @pl.when(first_ref[b] == 1)
    def _():
        wg_bf[...] = wg_ref[0].astype(BF16)
        wu_bf[...] = wu_ref[0].astype(BF16)

    def compute(m):
        x_lo, x_hi = _unpack_bf16_pairs(x_ref[0:m, :])

        def proj(w_bf, bias_ref):
            return (jnp.dot(x_lo, w_bf[0:D_PACK, :], preferred_element_type=F32)
                    + jnp.dot(x_hi, w_bf[D_PACK:, :], preferred_element_type=F32) + bias_ref[0])

        g = proj(wg_bf, bg_ref)
        lin = proj(wu_bf, bu_ref)
        g = jnp.minimum(g, SWIGLU_LIMIT)
        lin = jnp.clip(lin, -SWIGLU_LIMIT, SWIGLU_LIMIT)
        act = g * jax.nn.sigmoid(SWIGLU_ALPHA * g) * (lin + 1.0)
        h_ref[0:m, :] = act.astype(BF16)

    _row_cases(b, nu_ref, full_ref, h_ref, compute)


def _up(xs, tables, w_gate, b_gate, w_up, b_up):
    n_rows = xs.shape[0]
    nb = n_rows // EXPERT_TM
    n_chunks = D_FF // UP_TF
    row = lambda f, b, be, nu, first, nxt, full: (jnp.minimum(b, nu[0] - 1), 0)
    wsel = _weight_index_map(n_chunks)
    bsel = lambda f, b, be, nu, first, nxt, full: (be[b], 0, f)
    grid_spec = pltpu.PrefetchScalarGridSpec(
        num_scalar_prefetch=5,
        grid=(n_chunks, nb),
        in_specs=[
            pl.BlockSpec((EXPERT_TM, D_PACK), row),
            pl.BlockSpec((1, D_MODEL, UP_TF), wsel),
            pl.BlockSpec((1, D_MODEL, UP_TF), wsel),
            pl.BlockSpec((1, 1, UP_TF), bsel),
            pl.BlockSpec((1, 1, UP_TF), bsel),
        ],
        out_specs=pl.BlockSpec((EXPERT_TM, UP_TF), lambda f, b, be, nu, first, nxt, full: (b, f)),
        scratch_shapes=[pltpu.VMEM((D_MODEL, UP_TF), BF16), pltpu.VMEM((D_MODEL, UP_TF), BF16)],
    )
    return pl.pallas_call(
        _up_kernel,
        grid_spec=grid_spec,
        out_shape=jax.ShapeDtypeStruct((n_rows, D_FF), BF16),
        compiler_params=_params(("arbitrary", "arbitrary"), EXPERT_VMEM_LIMIT),
        name="expert_up",
    )(*tables, xs, w_gate, w_up,
      b_gate.reshape(N_EXPERTS, 1, D_FF), b_up.reshape(N_EXPERTS, 1, D_FF))


def _down_kernel(be_ref, nu_ref, first_ref, nxt_ref, full_ref, h_ref, wd_ref, bd_ref, y_ref, wd_bf):
    b = pl.program_id(1)

    @pl.when(first_ref[b] == 1)
    def _():
        wd_bf[...] = wd_ref[0].astype(BF16)

    def compute(m):
        y = jnp.dot(h_ref[0:m, :], wd_bf[...], preferred_element_type=F32) + bd_ref[0]
        y_ref[0:m, :] = _pack_bf16_pairs(y)

    _row_cases(b, nu_ref, full_ref, y_ref, compute)


def _down(hs, tables, w_down, b_down):
    n_rows = hs.shape[0]
    nb = n_rows // EXPERT_TM
    n_chunks = D_MODEL // DOWN_TN
    assert n_chunks == 1
    row = lambda n, b, be, nu, first, nxt, full: (jnp.minimum(b, nu[0] - 1), 0)
    grid_spec = pltpu.PrefetchScalarGridSpec(
        num_scalar_prefetch=5,
        grid=(n_chunks, nb),
        in_specs=[
            pl.BlockSpec((EXPERT_TM, D_FF), row),
            pl.BlockSpec((1, D_FF, DOWN_TN), _weight_index_map(n_chunks)),
            pl.BlockSpec((1, 1, DOWN_TN), lambda n, b, be, nu, first, nxt, full: (be[b], 0, n)),
        ],
        out_specs=pl.BlockSpec((EXPERT_TM, D_PACK), lambda n, b, be, nu, first, nxt, full: (b, 0)),
        scratch_shapes=[pltpu.VMEM((D_FF, DOWN_TN), BF16)],
    )
    return pl.pallas_call(
        _down_kernel,
        grid_spec=grid_spec,
        out_shape=jax.ShapeDtypeStruct((n_rows, D_PACK), jnp.uint32),
        compiler_params=_params(("arbitrary", "arbitrary"), EXPERT_VMEM_LIMIT),
        name="expert_down",
    )(*tables, hs, w_down, b_down.reshape(N_EXPERTS, 1, D_MODEL))


COMB_TC = 256


def _combine_kernel(dest_ref, xm_ref, wt_ref, g2_ref, ys_ref, o_ref, buf_ref, sems):
    step = pl.program_id(0)
    last = pl.num_programs(0) - 1
    slot = step % 2

    def gather_tile(tile, s):
        def issue(r, carry):
            for k in range(TOP_K):
                d = dest_ref[(tile * COMB_TC + r) * TOP_K + k]
                pltpu.make_async_copy(ys_ref.at[pl.ds(d, 1), :], buf_ref.at[s, pl.ds(k * COMB_TC + r, 1), :],
                                      sems.at[s]).start(priority=k % 2)
            return carry

        lax.fori_loop(0, COMB_TC, issue, 0, unroll=2)

    @pl.when(step == 0)
    def _():
        gather_tile(0, 0)

    @pl.when(step < last)
    def _():
        gather_tile(step + 1, 1 - slot)

    pltpu.make_async_copy(ys_ref.at[pl.ds(0, TOP_K * COMB_TC), :], buf_ref.at[slot], sems.at[slot]).wait()

    wt = wt_ref[...]
    acc_lo = acc_hi = None
    for k in range(TOP_K):
        lo, hi = _unpack_pairs_f32(buf_ref[slot, k * COMB_TC:(k + 1) * COMB_TC, :])
        w = wt[:, k:k + 1]
        acc_lo = w * lo if acc_lo is None else acc_lo + w * lo
        acc_hi = w * hi if acc_hi is None else acc_hi + w * hi
    o_ref[:, :D_PACK] = xm_ref[:, :D_PACK] + g2_ref[:, :D_PACK] * acc_lo
    o_ref[:, D_PACK:] = xm_ref[:, D_PACK:] + g2_ref[:, D_PACK:] * acc_hi


def _combine(ys, dest_flat, xm, wts, gate2):
    t = xm.shape[0]
    grid_spec = pltpu.PrefetchScalarGridSpec(
        num_scalar_prefetch=1,
        grid=(t // COMB_TC,),
        in_specs=[
            pl.BlockSpec((COMB_TC, D_MODEL), lambda i, d: (i, 0)),
            pl.BlockSpec((COMB_TC, LANES), lambda i, d: (i, 0)),
            pl.BlockSpec((1, D_MODEL), lambda i, d: (0, 0)),
            pl.BlockSpec(memory_space=pl.ANY),
        ],
        out_specs=pl.BlockSpec((COMB_TC, D_MODEL), lambda i, d: (i, 0)),
        scratch_shapes=[pltpu.VMEM((2, TOP_K * COMB_TC, D_PACK), jnp.uint32), pltpu.SemaphoreType.DMA((2,))],
    )
    return pl.pallas_call(
        _combine_kernel,
        grid_spec=grid_spec,
        out_shape=jax.ShapeDtypeStruct((t, D_MODEL), F32),
        compiler_params=_params(("arbitrary",)),
        name="combine",
    )(dest_flat, xm, wts, gate2, ys)


def _routing_tables(idx, rank, counts, n_tok):
    padded = ((counts + EXPERT_TM - 1) // EXPERT_TM) * EXPERT_TM
    pend = jnp.cumsum(padded)
    pstart = pend - padded
    experts = jnp.arange(N_EXPERTS, dtype=jnp.int32)
    start_of = jnp.sum(jnp.where(idx[..., None] == experts, pstart, 0), axis=-1)
    dest = (start_of + rank).astype(jnp.int32).reshape(-1)
    n_rows = n_tok * TOP_K + N_EXPERTS * EXPERT_TM
    nb = n_rows // EXPERT_TM
    block_start = jnp.arange(nb, dtype=jnp.int32) * EXPERT_TM
    block_expert = jnp.minimum(jnp.sum(pend[None, :] <= block_start[:, None], axis=1),
                               N_EXPERTS - 1).astype(jnp.int32)
    n_used = (pend[-1:] // EXPERT_TM).astype(jnp.int32)
    pad_lo = (pstart + counts).astype(jnp.int32)
    pad_hi = pend.astype(jnp.int32)

    blocks = jnp.arange(nb, dtype=jnp.int32)
    block_expert = jnp.where(blocks < n_used[0], block_expert, block_expert[jnp.maximum(n_used[0] - 1, 0)])
    prev = jnp.concatenate([block_expert[:1], block_expert[:-1]])
    first = jnp.logical_or(blocks == 0, block_expert != prev)
    first_pos = jnp.where(first, blocks, nb)
    next_first = jnp.flip(lax.cummin(jnp.flip(jnp.concatenate([first_pos[1:], jnp.full((1,), nb, jnp.int32)]))))
    nxt = jnp.where(next_first < nb, block_expert[jnp.minimum(next_first, nb - 1)], -1).astype(jnp.int32)
    rows_end = jnp.sum(jnp.where(block_expert[:, None] == experts, pad_lo, 0), axis=-1)
    full = (rows_end > block_start + EXPERT_TM // 2).astype(jnp.int32)
    tables = (block_expert, n_used, first.astype(jnp.int32), nxt, full)
    return dest, tables, pad_lo, pad_hi, n_rows


def _layer(x2, c, w_ada, b_ada, g_norm1, w_in, b_in, g_q, g_k, sinks, rel_bias, w_dw, b_dw, ln_g, ln_b,
           g_out_attn, g_out_conv, w_out, b_out, g_norm2, w_router, b_router,
           w_gate, b_gate, w_up, b_up, w_down, b_down):
    t = x2.shape[0]
    row = lambda v: v.reshape(1, -1)
    mod = _ada(c, w_ada, b_ada)
    shift1, scale1, gate1, shift2, scale2, gate2 = [mod[:, i * D_MODEL:(i + 1) * D_MODEL] for i in range(6)]

    q, kv, uc = _in_proj(x2, row(g_norm1), scale1, shift1, w_in.astype(BF16), row(b_in))

    gq_t = row(jnp.tile(g_q, N_Q_HEADS)) * (HEAD_DIM ** -0.5)
    gk_t = row(jnp.tile(g_k, N_KV_HEADS))
    ma = _attention(q, kv, _bias_table(rel_bias), sinks, gq_t, gk_t, row(g_out_attn))
    w_dw_p = jnp.concatenate([w_dw, jnp.zeros((HALO - CONV_WIDTH, D_CONV), w_dw.dtype)], axis=0)
    mc = _conv(uc, w_dw_p, row(b_dw), row(ln_g), row(ln_b), row(g_out_conv))

    xm, h2, idx, rank, wts, counts = _out_proj(
        ma, mc, x2, w_out.astype(BF16), row(b_out), gate1, row(g_norm2), scale2, shift2,
        w_router, row(b_router))

    dest, tables, pad_lo, pad_hi, n_rows = _routing_tables(idx[:, :TOP_K], rank[:, :TOP_K], counts[0], t)
    xs = _dispatch(h2, dest, pad_lo, pad_hi, tables[1], n_rows)
    hs = _up(xs, tables, w_gate, b_gate, w_up, b_up)
    ys = _down(hs, tables, w_down, b_down)
    return _combine(ys, dest, xm, wts, gate2)


def kernel(x, c, w_ada, b_ada, g_norm1, w_in, b_in, g_q, g_k, sinks, rel_bias, w_dw, b_dw, ln_g, ln_b,
           g_out_attn, g_out_conv, w_out, b_out, g_norm2, w_router, b_router,
           w_gate, b_gate, w_up, b_up, w_down, b_down):
    b, t, d = x.shape
    assert b == 1 and d == D_MODEL and w_ada.shape[0] == 1
    out = _layer(x.reshape(t, d), c, w_ada[0], b_ada[0], g_norm1[0], w_in[0], b_in[0], g_q[0], g_k[0],
                 sinks[0], rel_bias, w_dw[0], b_dw[0], ln_g[0], ln_b[0], g_out_attn[0], g_out_conv[0],
                 w_out[0], b_out[0], g_norm2[0], w_router[0], b_router[0],
                 w_gate[0], b_gate[0], w_up[0], b_up[0], w_down[0], b_down[0])
    return out.reshape(b, t, d)
```

```python
import functools
import math

import jax
import jax.numpy as jnp
from jax import lax
from jax.experimental import pallas as pl
from jax.experimental.pallas import tpu as pltpu

D_MODEL = 2048
HEAD_DIM = 64
N_Q_HEADS = 16
N_KV_HEADS = 2
D_ATTN = N_Q_HEADS * HEAD_DIM
D_KV = N_KV_HEADS * HEAD_DIM
D_CONV = D_MODEL - D_ATTN
D_IN = D_ATTN + 2 * D_KV + 2 * D_CONV
WINDOW = 128
BLOCK = 128
CONV_WIDTH = 31
N_BUCKETS = 32
MAX_DISTANCE = 128
N_EXPERTS = 32
TOP_K = 4
D_FF = D_MODEL
SWIGLU_LIMIT = 7.0
SWIGLU_ALPHA = 1.702
EPS = 1e-6
NEG_INF = -1e30

LANES = 128
SUBLANES = 8
VMEM_LIMIT = 56 * 1024 * 1024

HALO = 32
EXPERT_TM = 512
EXPERT_VMEM_LIMIT = 60 * 1024 * 1024
F32 = jnp.float32
BF16 = jnp.bfloat16


def _params(sem, vmem_limit=VMEM_LIMIT):
    return pltpu.CompilerParams(dimension_semantics=sem, vmem_limit_bytes=vmem_limit)


ADA_TN = 1024
ADA_RC = 256


def _ada_kernel(c_ref, w_ref, b_ref, o_ref):
    tn = w_ref.shape[1]

    def body(i, acc):
        r = pl.multiple_of(i * ADA_RC, ADA_RC)
        c = c_ref[pl.ds(r, ADA_RC), :]
        sc = c * jax.nn.sigmoid(c)
        prod = w_ref[pl.ds(r, ADA_RC), :] * sc
        return acc + jnp.sum(prod.reshape(ADA_RC // 8, 8, tn), axis=0)

    acc = lax.fori_loop(0, D_MODEL // ADA_RC, body, jnp.zeros((8, tn), F32))
    o_ref[...] = jnp.sum(acc, axis=0, keepdims=True) + b_ref[...]


def _ada(c, w_ada, b_ada):
    n = w_ada.shape[1]
    return pl.pallas_call(
        _ada_kernel,
        grid=(n // ADA_TN,),
        in_specs=[
            pl.BlockSpec((D_MODEL, 1), lambda j: (0, 0)),
            pl.BlockSpec((D_MODEL, ADA_TN), lambda j: (0, j)),
            pl.BlockSpec((1, ADA_TN), lambda j: (0, j)),
        ],
        out_specs=pl.BlockSpec((1, ADA_TN), lambda j: (0, j)),
        out_shape=jax.ShapeDtypeStruct((1, n), F32),
        compiler_params=_params(("parallel",)),
        name="ada",
    )(c.reshape(D_MODEL, 1), w_ada, b_ada.reshape(1, n))


IN_TM = 512
IN_NC = 256


def _modulated_rms(x, g, scale, shift):
    ms = jnp.mean(x * x, axis=-1, keepdims=True)
    return (x * lax.rsqrt(ms + EPS) * g) * (1.0 + scale) + shift


def _in_kernel(x_ref, g_ref, sc_ref, sh_ref, w_ref, b_ref, q_ref, kv_ref, uc_ref):
    h = _modulated_rms(x_ref[...], g_ref[...], sc_ref[...], sh_ref[...]).astype(BF16)

    def proj(lo, n, o_ref):
        for c in range(0, n, IN_NC):
            u = jnp.dot(h, w_ref[:, lo + c:lo + c + IN_NC], preferred_element_type=F32)
            o_ref[:, c:c + IN_NC] = u + b_ref[:, lo + c:lo + c + IN_NC]

    proj(0, D_ATTN, q_ref)
    proj(D_ATTN, 2 * D_KV, kv_ref)
    proj(D_ATTN + 2 * D_KV, 2 * D_CONV, uc_ref)


def _in_proj(x2, g1, scale1, shift1, w_in_bf, b_in):
    t = x2.shape[0]
    vec = lambda n: pl.BlockSpec((1, n), lambda i: (0, 0))
    return pl.pallas_call(
        _in_kernel,
        grid=(t // IN_TM,),
        in_specs=[
            pl.BlockSpec((IN_TM, D_MODEL), lambda i: (i, 0)),
            vec(D_MODEL), vec(D_MODEL), vec(D_MODEL),
            pl.BlockSpec((D_MODEL, D_IN), lambda i: (0, 0)),
            vec(D_IN),
        ],
        out_specs=[
            pl.BlockSpec((IN_TM, D_ATTN), lambda i: (i, 0)),
            pl.BlockSpec((IN_TM, 2 * D_KV), lambda i: (i, 0)),
            pl.BlockSpec((IN_TM, 2 * D_CONV), lambda i: (i, 0)),
        ],
        out_shape=[
            jax.ShapeDtypeStruct((t, D_ATTN), F32),
            jax.ShapeDtypeStruct((t, 2 * D_KV), F32),
            jax.ShapeDtypeStruct((t, 2 * D_CONV), F32),
        ],
        compiler_params=_params(("parallel",)),
        name="in_proj",
    )(x2, g1, scale1, shift1, w_in_bf, b_in)


ATT_R = 4


def _split_dot(a, b_bf):
    hi = a.astype(BF16)
    lo = (a - hi.astype(F32)).astype(BF16)
    return (jnp.dot(hi, b_bf, preferred_element_type=F32)
            + jnp.dot(lo, b_bf, preferred_element_type=F32))


def _attn_kernel(sinks_ref, q_ref, kvp_ref, kvc_ref, bias_ref, gq_ref, gk_ref, go_ref,
                 hq_ref, hqt_ref, hk_ref, o_ref, y_ref):
    step = pl.program_id(0)
    lane = lax.broadcasted_iota(jnp.int32, (2 * BLOCK, LANES), 1)
    low = lane < HEAD_DIM
    col = lax.broadcasted_iota(jnp.int32, (BLOCK, 2 * BLOCK), 1)
    first_lo = jnp.where(step == 0, BLOCK, 0)

    for r in range(ATT_R):
        rows = slice(r * BLOCK, (r + 1) * BLOCK)
        q = q_ref[rows, :]
        ssq = _split_dot(_split_dot(q * q, hq_ref[...]), hqt_ref[...])
        qn = (q * lax.rsqrt(ssq * (1.0 / HEAD_DIM) + EPS) * gq_ref[...]).astype(BF16)

        if r == 0:
            kv_prev = kvp_ref[...]
        else:
            kv_prev = kvc_ref[(r - 1) * BLOCK:r * BLOCK, :]
        kv = jnp.concatenate([kv_prev, kvc_ref[rows, :]], axis=0)
        k = kv[:, :LANES]
        v = kv[:, LANES:]
        kss = _split_dot(k * k, hk_ref[...])
        kn = k * lax.rsqrt(kss * (1.0 / HEAD_DIM) + EPS) * gk_ref[...]
        kn_sw = pltpu.roll(kn, HEAD_DIM, axis=1)
        v_sw = pltpu.roll(v, HEAD_DIM, axis=1)
        zero = jnp.zeros_like(kn)
        k_lo = [jnp.where(low, kn, zero).astype(BF16), jnp.where(low, kn_sw, zero).astype(BF16)]
        k_hi = [jnp.where(low, zero, kn_sw).astype(BF16), jnp.where(low, zero, kn).astype(BF16)]
        v_lo = [jnp.where(low, v, zero).astype(BF16), jnp.where(low, v_sw, zero).astype(BF16)]
        v_hi = [jnp.where(low, zero, v_sw).astype(BF16), jnp.where(low, zero, v).astype(BF16)]

        for p in range(N_Q_HEADS // 2):
            g = (2 * p) // (N_Q_HEADS // N_KV_HEADS)
            qp = qn[:, p * LANES:(p + 1) * LANES]
            acc = None
            for half, (kz, vz) in enumerate(((k_lo[g], v_lo[g]), (k_hi[g], v_hi[g]))):
                h = 2 * p + half
                s = lax.dot_general(qp, kz, (((1,), (1,)), ((), ())), preferred_element_type=F32)
                s = s + bias_ref[h]
                if r == 0:
                    s = jnp.where(col >= first_lo, s, NEG_INF)
                sink = sinks_ref[h]
                m = jnp.maximum(jnp.max(s, axis=-1, keepdims=True), sink)
                e = jnp.exp(s - m)
                denom = jnp.sum(e, axis=-1, keepdims=True) + jnp.exp(sink - m)
                pv = jnp.dot(e.astype(BF16), vz, preferred_element_type=F32)
                pv = pv * (1.0 / denom)
                acc = pv if acc is None else acc + pv
            y_ref[:, p * LANES:(p + 1) * LANES] = acc

        y = y_ref[...]
        ms = jnp.mean(y * y, axis=-1, keepdims=True)
        o_ref[rows, :] = (y * lax.rsqrt(ms + EPS) * go_ref[...]).astype(BF16)


def _attention(q, kv, bias, sinks, gq_t, gk_t, g_out):
    t = q.shape[0]
    tile = ATT_R * BLOCK
    head_of_lane = jnp.arange(D_ATTN) // HEAD_DIM
    hq = (head_of_lane[:, None] == jnp.arange(LANES)[None, :]).astype(BF16)
    hk = _head_indicator(LANES)
    grid_spec = pltpu.PrefetchScalarGridSpec(
        num_scalar_prefetch=0,
        grid=(t // tile,),
        in_specs=[
            pl.BlockSpec(memory_space=pltpu.SMEM),
            pl.BlockSpec((tile, D_ATTN), lambda i: (i, 0)),
            pl.BlockSpec((BLOCK, 2 * D_KV), lambda i: (jnp.maximum(i * ATT_R - 1, 0), 0)),
            pl.BlockSpec((tile, 2 * D_KV), lambda i: (i, 0)),
            pl.BlockSpec((N_Q_HEADS, BLOCK, 2 * BLOCK), lambda i: (0, 0, 0)),
            pl.BlockSpec((1, D_ATTN), lambda i: (0, 0)),
            pl.BlockSpec((1, LANES), lambda i: (0, 0)),
            pl.BlockSpec((1, D_ATTN), lambda i: (0, 0)),
            pl.BlockSpec((D_ATTN, LANES), lambda i: (0, 0)),
            pl.BlockSpec((LANES, D_ATTN), lambda i: (0, 0)),
            pl.BlockSpec((LANES, LANES), lambda i: (0, 0)),
        ],
        out_specs=pl.BlockSpec((tile, D_ATTN), lambda i: (i, 0)),
        scratch_shapes=[pltpu.VMEM((BLOCK, D_ATTN), F32)],
    )
    return pl.pallas_call(
        _attn_kernel,
        grid_spec=grid_spec,
        out_shape=jax.ShapeDtypeStruct((t, D_ATTN), BF16),
        compiler_params=_params(("parallel",)),
        name="attn",
    )(sinks, q, kv, kv, bias, gq_t, gk_t, g_out, hq, hq.T, hk)


def _t5_bucket(dist):
    max_exact = N_BUCKETS // 2
    d = jnp.maximum(dist, 0)
    log_ratio = jnp.log(jnp.maximum(d, max_exact).astype(F32) / max_exact)
    large = max_exact + (log_ratio / math.log(MAX_DISTANCE / max_exact)
                         * (N_BUCKETS - max_exact)).astype(jnp.int32)
    large = jnp.minimum(large, N_BUCKETS - 1)
    return jnp.where(d < max_exact, d, large)


def _bias_table(rel_bias):
    q_local = jnp.arange(BLOCK, dtype=jnp.int32) + BLOCK
    k_local = jnp.arange(2 * BLOCK, dtype=jnp.int32)
    dist = q_local[:, None] - k_local[None, :]
    band = (dist >= 0) & (dist < WINDOW)
    bucket = _t5_bucket(dist)
    hit = (jnp.arange(N_BUCKETS, dtype=jnp.int32)[:, None] == bucket.reshape(1, -1)).astype(F32)
    bias = jnp.dot(rel_bias.astype(F32).T, hit, precision=lax.Precision.HIGHEST)
    return jnp.where(band[None], bias.reshape(N_Q_HEADS, BLOCK, 2 * BLOCK), NEG_INF)


def _head_indicator(n):
    i = jnp.arange(n) // HEAD_DIM
    return (i[:, None] == i[None, :]).astype(BF16)


CONV_TT = 512
CONV_RC = 64
CONV_CC = 256


def _conv_kernel(u_ref, halo_ref, w_ref, b_ref, lg_ref, lb_ref, go_ref, o_ref, h_ref, y_ref):
    step = pl.program_id(0)

    def glu(u):
        return u[:, :D_CONV] * jax.nn.sigmoid(u[:, D_CONV:])

    hh = glu(halo_ref[...])
    h_ref[0, 0:HALO, :] = jnp.where(step == 0, jnp.zeros_like(hh), hh)
    h_ref[0, HALO:, :] = glu(u_ref[...])

    n_rows = CONV_TT + HALO
    for s in range(1, SUBLANES):
        for c0 in range(0, D_CONV, CONV_CC):
            cs = slice(c0, c0 + CONV_CC)
            h_ref[s, :, cs] = pltpu.roll(h_ref[0, :, cs], n_rows - s, axis=0)

    off = HALO - (CONV_WIDTH - 1)
    for r0 in range(0, CONV_TT, CONV_RC):
        for c0 in range(0, D_CONV, CONV_CC):
            cs = slice(c0, c0 + CONV_CC)
            acc = jnp.broadcast_to(b_ref[:, cs], (CONV_RC, CONV_CC))
            for j in range(CONV_WIDTH):
                s = (r0 + off + j) % SUBLANES
                a = r0 + off + j - s
                acc = acc + w_ref[j:j + 1, cs] * h_ref[s, a:a + CONV_RC, cs]
            y_ref[r0:r0 + CONV_RC, cs] = acc

    y = y_ref[...]
    mu = jnp.mean(y, axis=-1, keepdims=True)
    yc = y - mu
    var = jnp.mean(yc * yc, axis=-1, keepdims=True)
    z = yc * lax.rsqrt(var + EPS) * lg_ref[...] + lb_ref[...]
    s = z * jax.nn.sigmoid(z)
    ms = jnp.mean(s * s, axis=-1, keepdims=True)
    o_ref[...] = (s * lax.rsqrt(ms + EPS) * go_ref[...]).astype(BF16)


def _conv(uc, w_dw, b_dw, ln_g, ln_b, g_out):
    t = uc.shape[0]
    vec = lambda: pl.BlockSpec((1, D_CONV), lambda i: (0, 0))
    per = CONV_TT // HALO
    return pl.pallas_call(
        _conv_kernel,
        grid=(t // CONV_TT,),
        in_specs=[
            pl.BlockSpec((CONV_TT, 2 * D_CONV), lambda i: (i, 0)),
            pl.BlockSpec((HALO, 2 * D_CONV), lambda i: (jnp.maximum(i * per - 1, 0), 0)),
            pl.BlockSpec((HALO, D_CONV), lambda i: (0, 0)),
            vec(), vec(), vec(), vec(),
        ],
        out_specs=pl.BlockSpec((CONV_TT, D_CONV), lambda i: (i, 0)),
        out_shape=jax.ShapeDtypeStruct((t, D_CONV), BF16),
        scratch_shapes=[pltpu.VMEM((SUBLANES, CONV_TT + HALO, D_CONV), F32),
                        pltpu.VMEM((CONV_TT, D_CONV), F32)],
        compiler_params=_params(("parallel",)),
        name="conv",
    )(uc, uc, w_dw, b_dw, ln_g, ln_b, g_out)


OUT_TM = 512
D_PACK = D_MODEL // 2
HI_MASK = 0xFFFF0000


def _pack_bf16_pairs(h):
    lo = lax.bitcast_convert_type(h[:, :D_PACK].astype(BF16).astype(F32), jnp.uint32)
    hi = lax.bitcast_convert_type(h[:, D_PACK:].astype(BF16).astype(F32), jnp.uint32)
    return (lo >> 16) | (hi & jnp.uint32(HI_MASK))


def _unpack_pairs_f32(w):
    lo = lax.bitcast_convert_type(w << 16, F32)
    hi = lax.bitcast_convert_type(w & jnp.uint32(HI_MASK), F32)
    return lo, hi


def _unpack_bf16_pairs(w):
    lo, hi = _unpack_pairs_f32(w)
    return lo.astype(BF16), hi.astype(BF16)


def _out_kernel(ma_ref, mc_ref, x_ref, w_ref, bo_ref, g1_ref, g2_ref, sc_ref, sh_ref, wr_ref, br_ref,
                tri_ref, xm_ref, h2_ref, idx_ref, rank_ref, wt_ref, cnt_ref, carry_ref):
    step = pl.program_id(0)

    @pl.when(step == 0)
    def _():
        carry_ref[...] = jnp.zeros_like(carry_ref)

    y = (jnp.dot(ma_ref[...], w_ref[0:D_ATTN, :], preferred_element_type=F32)
         + jnp.dot(mc_ref[...], w_ref[D_ATTN:, :], preferred_element_type=F32) + bo_ref[...])
    xm = x_ref[...] + g1_ref[...] * y
    xm_ref[...] = xm
    h2 = _modulated_rms(xm, g2_ref[...], sc_ref[...], sh_ref[...])
    h2_ref[...] = _pack_bf16_pairs(h2)

    both = jnp.dot(h2.astype(BF16), wr_ref[...], preferred_element_type=F32)
    logits = both[:, :N_EXPERTS] + both[:, N_EXPERTS:] + br_ref[...]
    tm = logits.shape[0]
    lane = lax.broadcasted_iota(jnp.int32, (tm, N_EXPERTS), 1).astype(F32)
    vals, idxs = [], []
    l = logits
    for _ in range(TOP_K):
        m = jnp.max(l, axis=-1, keepdims=True)
        i = jnp.min(jnp.where(l == m, lane, float(N_EXPERTS)), axis=-1, keepdims=True)
        vals.append(m)
        idxs.append(i)
        l = jnp.where(lane == i, -jnp.inf, l)
    es = [jnp.exp(v - vals[0]) for v in vals]
    tot = es[0] + es[1] + es[2] + es[3]
    ws = [e / tot for e in es]

    hot = [(lane == i).astype(F32) for i in idxs]
    hot_all = hot[0] + hot[1] + hot[2] + hot[3]
    before = jnp.dot(tri_ref[...], hot_all.astype(BF16), preferred_element_type=F32) + carry_ref[...]
    ranks = [jnp.sum(h * before, axis=-1, keepdims=True) for h in hot]
    carry_ref[...] = carry_ref[...] + jnp.sum(hot_all, axis=0, keepdims=True)
    cnt_ref[...] = carry_ref[...].astype(jnp.int32)

    slot = lax.broadcasted_iota(jnp.int32, (tm, LANES), 1)

    def pack(cols):
        out = jnp.zeros((tm, LANES), F32)
        for k in range(TOP_K):
            out = jnp.where(slot == k, cols[k], out)
        return out

    idx_ref[...] = pack(idxs).astype(jnp.int32)
    rank_ref[...] = pack(ranks).astype(jnp.int32)
    wt_ref[...] = pack(ws)


def _out_proj(ma, mc, x2, w_out_bf, b_out, gate1, g2, scale2, shift2, w_router, b_router):
    t = x2.shape[0]
    vec = lambda n: pl.BlockSpec((1, n), lambda i: (0, 0))
    tri = jnp.tril(jnp.ones((OUT_TM, OUT_TM), F32), -1).astype(BF16)
    wr_hi = w_router.astype(BF16)
    wr_lo = (w_router - wr_hi.astype(F32)).astype(BF16)
    w_router = jnp.concatenate([wr_hi, wr_lo], axis=1)
    return pl.pallas_call(
        _out_kernel,
        grid=(t // OUT_TM,),
        in_specs=[
            pl.BlockSpec((OUT_TM, D_ATTN), lambda i: (i, 0)),
            pl.BlockSpec((OUT_TM, D_CONV), lambda i: (i, 0)),
            pl.BlockSpec((OUT_TM, D_MODEL), lambda i: (i, 0)),
            pl.BlockSpec((D_MODEL, D_MODEL), lambda i: (0, 0)),
            vec(D_MODEL), vec(D_MODEL), vec(D_MODEL), vec(D_MODEL), vec(D_MODEL),
            pl.BlockSpec((D_MODEL, 2 * N_EXPERTS), lambda i: (0, 0)),
            vec(N_EXPERTS),
            pl.BlockSpec((OUT_TM, OUT_TM), lambda i: (0, 0)),
        ],
        out_specs=[
            pl.BlockSpec((OUT_TM, D_MODEL), lambda i: (i, 0)),
            pl.BlockSpec((OUT_TM, D_PACK), lambda i: (i, 0)),
            pl.BlockSpec((OUT_TM, LANES), lambda i: (i, 0)),
            pl.BlockSpec((OUT_TM, LANES), lambda i: (i, 0)),
            pl.BlockSpec((OUT_TM, LANES), lambda i: (i, 0)),
            pl.BlockSpec((1, N_EXPERTS), lambda i: (0, 0)),
        ],
        out_shape=[
            jax.ShapeDtypeStruct((t, D_MODEL), F32),
            jax.ShapeDtypeStruct((t, D_PACK), jnp.uint32),
            jax.ShapeDtypeStruct((t, LANES), jnp.int32),
            jax.ShapeDtypeStruct((t, LANES), jnp.int32),
            jax.ShapeDtypeStruct((t, LANES), F32),
            jax.ShapeDtypeStruct((1, N_EXPERTS), jnp.int32),
        ],
        scratch_shapes=[pltpu.VMEM((1, N_EXPERTS), F32)],
        compiler_params=_params(("arbitrary",)),
        name="out_proj",
    )(ma, mc, x2, w_out_bf, b_out, gate1, g2, scale2, shift2, w_router, b_router, tri)


DISP_TD = 256
DISP_NBUF = 3


class _CopyGroup:
    def __init__(self, copies):
        self.copies = copies

    def start(self):
        for cp in self.copies:
            cp.start()

    def wait(self):
        for cp in self.copies:
            cp.wait()

ZERO_RUN = 64


def _dispatch_kernel(dest_ref, pad_lo_ref, pad_hi_ref, nu_ref, h2_ref, xs_ref,
                     stage_ref, zero_ref, in_sems, out_sems, zsem):
    step = pl.program_id(0)
    last = pl.num_programs(0) - 1
    slot = lax.rem(step, DISP_NBUF)
    rows_per_chunk = DISP_TD * TOP_K

    def load(chunk, s):
        rows = pl.ds(pl.multiple_of(chunk * DISP_TD, DISP_TD), DISP_TD)
        return _CopyGroup([pltpu.make_async_copy(h2_ref.at[rows, pl.ds(j * LANES, LANES)], stage_ref.at[s, :, j, :],
                                                 in_sems.at[s]) for j in range(stage_ref.shape[2])])

    def wait_chunk(s):
        whole = xs_ref.at[pl.ds(0, rows_per_chunk)]
        pltpu.make_async_copy(whole, whole, out_sems.at[s]).wait()

    @pl.when(step == 0)
    def _():
        load(0, 0).start()

        @pl.when(last >= 1)
        def _():
            load(1, 1).start()

    load(step, slot).wait()

    def issue(r, carry):
        for k in range(TOP_K):
            d = dest_ref[(step * DISP_TD + r) * TOP_K + k]
            pltpu.make_async_copy(stage_ref.at[slot, r], xs_ref.at[d], out_sems.at[slot]).start(priority=k % 2)
        return carry

    lax.fori_loop(0, DISP_TD, issue, 0, unroll=2)

    @pl.when(step == 0)
    def _():
        zero_ref[...] = jnp.zeros_like(zero_ref)

        def zero_rows(d, n):
            d = d if n == 1 else pl.multiple_of(d, n)
            return pltpu.make_async_copy(zero_ref.at[pl.ds(0, n)], xs_ref.at[pl.ds(d, n)], zsem)

        def ranges(e):
            lo, hi = pad_lo_ref[e], pad_hi_ref[e]
            a = jnp.minimum((lo + SUBLANES - 1) // SUBLANES * SUBLANES, hi)
            b = jnp.minimum((lo + ZERO_RUN - 1) // ZERO_RUN * ZERO_RUN, hi)
            return ((lo, a - lo, 1), (a, (b - a) // SUBLANES, SUBLANES), (b, (hi - b) // ZERO_RUN, ZERO_RUN))

        def for_all_runs(act):
            def per_expert(e, carry):
                for start, count, n in ranges(e):
                    lax.fori_loop(0, count, lambda i, c, s=start, n=n: (act(zero_rows(s + i * n, n)), c)[1], 0)
                return carry

            lax.fori_loop(0, N_EXPERTS, per_expert, 0)
            n_blocks = xs_ref.shape[0] // EXPERT_TM
            lax.fori_loop(nu_ref[0], n_blocks,
                          lambda blk, c: (act(zero_rows(blk * EXPERT_TM, EXPERT_TM)), c)[1], 0)

        for_all_runs(lambda copy: copy.start())
        for_all_runs(lambda copy: copy.wait())

    @pl.when(step > 0)
    def _():
        wait_chunk(lax.rem(step + DISP_NBUF - 1, DISP_NBUF))

    @pl.when(step + 2 <= last)
    def _():
        load(step + 2, lax.rem(step + 2, DISP_NBUF)).start()

    @pl.when(step == last)
    def _():
        wait_chunk(slot)


def _dispatch(h2p, dest_flat, pad_lo, pad_hi, n_used, n_rows):
    t = h2p.shape[0]
    grid_spec = pltpu.PrefetchScalarGridSpec(
        num_scalar_prefetch=4,
        grid=(t // DISP_TD,),
        in_specs=[pl.BlockSpec(memory_space=pl.ANY)],
        out_specs=pl.BlockSpec(memory_space=pl.ANY),
        scratch_shapes=[pltpu.VMEM((DISP_NBUF, DISP_TD, D_PACK // LANES, LANES), jnp.uint32),
                        pltpu.VMEM((EXPERT_TM, D_PACK // LANES, LANES), jnp.uint32),
                        pltpu.SemaphoreType.DMA((DISP_NBUF,)), pltpu.SemaphoreType.DMA((DISP_NBUF,)),
                        pltpu.SemaphoreType.DMA(())],
    )
    return pl.pallas_call(
        _dispatch_kernel,
        grid_spec=grid_spec,
        out_shape=jax.ShapeDtypeStruct((n_rows, D_PACK // LANES, LANES), jnp.uint32),
        compiler_params=_params(("arbitrary",)),
        name="dispatch",
    )(dest_flat, pad_lo, pad_hi, n_used, h2p)


UP_TF = 1024
DOWN_TN = 2048


def _weight_index_map(n_chunks):
    def index_map(c, b, be, nu, first, nxt, full):
        del nu, full
        in_last = nxt[b] < 0
        wrap = jnp.logical_and(in_last, c + 1 < n_chunks)
        e_next = jnp.where(in_last, jnp.where(wrap, be[0], be[b]), nxt[b])
        c_next = jnp.where(wrap, c + 1, c)
        is_first = first[b] == 1
        return jnp.where(is_first, be[b], e_next), 0, jnp.where(is_first, c, c_next)

    return index_map


def _row_cases(b, nu_ref, full_ref, o_ref, compute):
    half = EXPERT_TM // 2
    used = b < nu_ref[0]

    @pl.when(jnp.logical_and(used, full_ref[b] == 1))
    def _():
        compute(EXPERT_TM)

    @pl.when(jnp.logical_and(used, full_ref[b] == 0))
    def _():
        compute(half)
        o_ref[half:, :] = jnp.zeros((half, o_ref.shape[1]), o_ref.dtype)

    @pl.when(jnp.logical_not(used))
    def _():
        o_ref[...] = jnp.zeros_like(o_ref)


def _up_kernel(be_ref, nu_ref, first_ref, nxt_ref, full_ref, xs_ref, wg_ref, wu_ref, bg_ref, bu_ref, h_ref,
               wg_bf, wu_bf, xbuf_ref, xsems):
    f = pl.program_id(0)
    b = pl.program_id(1)
    nb = pl.num_programs(1)
    step = f * nb + b
    slot = step % 2
    n_sub = xs_ref.shape[1]

    def fetch(blk, s):
        rows = pl.ds(pl.multiple_of(jnp.minimum(blk, nu_ref[0] - 1) * EXPERT_TM, EXPERT_TM), EXPERT_TM)
        return [pltpu.make_async_copy(xs_ref.at[rows, j, :], xbuf_ref.at[s, :, pl.ds(j * LANES, LANES)],
                                      xsems.at[s]) for j in range(n_sub)]

    @pl.when(step == 0)
    def _():
        for cp in fetch(b, slot):
            cp.start()

    @pl.when(step + 1 < pl.num_programs(0) * nb)
    def _():
        for cp in fetch(jnp.where(b + 1 < nb, b + 1, 0), 1 - slot):
            cp.start()

    for cp in fetch(b, slot):
        cp.wait()
    x_ref = xbuf_ref.at[slot]

    @pl.when(first_ref[b] == 1)
    def _():
        wg_bf[...] = wg_ref[0].astype(BF16)
        wu_bf[...] = wu_ref[0].astype(BF16)

    def compute(m):
        x_lo, x_hi = _unpack_bf16_pairs(x_ref[0:m, :])

        def proj(w_bf, bias_ref):
            return (jnp.dot(x_lo, w_bf[0:D_PACK, :], preferred_element_type=F32)
                    + jnp.dot(x_hi, w_bf[D_PACK:, :], preferred_element_type=F32) + bias_ref[0])

        g = proj(wg_bf, bg_ref)
        lin = proj(wu_bf, bu_ref)
        g = jnp.minimum(g, SWIGLU_LIMIT)
        lin = jnp.clip(lin, -SWIGLU_LIMIT, SWIGLU_LIMIT)
        act = g * jax.nn.sigmoid(SWIGLU_ALPHA * g) * (lin + 1.0)
        h_ref[0:m, :] = act.astype(BF16)

    _row_cases(b, nu_ref, full_ref, h_ref, compute)


def _up(xs, tables, w_gate, b_gate, w_up, b_up):
    n_rows = xs.shape[0]
    nb = n_rows // EXPERT_TM
    n_chunks = D_FF // UP_TF
    wsel = _weight_index_map(n_chunks)
    bsel = lambda f, b, be, nu, first, nxt, full: (be[b], 0, f)
    grid_spec = pltpu.PrefetchScalarGridSpec(
        num_scalar_prefetch=5,
        grid=(n_chunks, nb),
        in_specs=[
            pl.BlockSpec(memory_space=pl.ANY),
            pl.BlockSpec((1, D_MODEL, UP_TF), wsel),
            pl.BlockSpec((1, D_MODEL, UP_TF), wsel),
            pl.BlockSpec((1, 1, UP_TF), bsel),
            pl.BlockSpec((1, 1, UP_TF), bsel),
        ],
        out_specs=pl.BlockSpec((EXPERT_TM, UP_TF), lambda f, b, be, nu, first, nxt, full: (b, f)),
        scratch_shapes=[pltpu.VMEM((D_MODEL, UP_TF), BF16), pltpu.VMEM((D_MODEL, UP_TF), BF16),
                        pltpu.VMEM((2, EXPERT_TM, D_PACK), jnp.uint32), pltpu.SemaphoreType.DMA((2,))],
    )
    return pl.pallas_call(
        _up_kernel,
        grid_spec=grid_spec,
        out_shape=jax.ShapeDtypeStruct((n_rows, D_FF), BF16),
        compiler_params=_params(("arbitrary", "arbitrary"), EXPERT_VMEM_LIMIT),
        name="expert_up",
    )(*tables, xs, w_gate, w_up,
      b_gate.reshape(N_EXPERTS, 1, D_FF), b_up.reshape(N_EXPERTS, 1, D_FF))


def _down_kernel(be_ref, nu_ref, first_ref, nxt_ref, full_ref, h_ref, wd_ref, bd_ref, y_ref, wd_bf):
    b = pl.program_id(1)

    @pl.when(first_ref[b] == 1)
    def _():
        wd_bf[...] = wd_ref[0].astype(BF16)

    def compute(m):
        y = jnp.dot(h_ref[0:m, :], wd_bf[...], preferred_element_type=F32) + bd_ref[0]
        y_ref[0:m, :] = _pack_bf16_pairs(y)

    _row_cases(b, nu_ref, full_ref, y_ref, compute)


def _down(hs, tables, w_down, b_down):
    n_rows = hs.shape[0]
    nb = n_rows // EXPERT_TM
    n_chunks = D_MODEL // DOWN_TN
    assert n_chunks == 1
    row = lambda n, b, be, nu, first, nxt, full: (jnp.minimum(b, nu[0] - 1), 0)
    grid_spec = pltpu.PrefetchScalarGridSpec(
        num_scalar_prefetch=5,
        grid=(n_chunks, nb),
        in_specs=[
            pl.BlockSpec((EXPERT_TM, D_FF), row),
            pl.BlockSpec((1, D_FF, DOWN_TN), _weight_index_map(n_chunks)),
            pl.BlockSpec((1, 1, DOWN_TN), lambda n, b, be, nu, first, nxt, full: (be[b], 0, n)),
        ],
        out_specs=pl.BlockSpec((EXPERT_TM, D_PACK), lambda n, b, be, nu, first, nxt, full: (b, 0)),
        scratch_shapes=[pltpu.VMEM((D_FF, DOWN_TN), BF16)],
    )
    return pl.pallas_call(
        _down_kernel,
        grid_spec=grid_spec,
        out_shape=jax.ShapeDtypeStruct((n_rows, D_PACK), jnp.uint32),
        compiler_params=_params(("arbitrary", "arbitrary"), EXPERT_VMEM_LIMIT),
        name="expert_down",
    )(*tables, hs, w_down, b_down.reshape(N_EXPERTS, 1, D_MODEL))


COMB_TC = 256


def _combine_kernel(dest_ref, xm_ref, wt_ref, g2_ref, ys_ref, o_ref, buf_ref, sems):
    step = pl.program_id(0)
    last = pl.num_programs(0) - 1
    slot = step % 2

    def gather_tile(tile, s):
        def issue(r, carry):
            for k in range(TOP_K):
                d = dest_ref[(tile * COMB_TC + r) * TOP_K + k]
                pltpu.make_async_copy(ys_ref.at[pl.ds(d, 1), :], buf_ref.at[s, pl.ds(k * COMB_TC + r, 1), :],
                                      sems.at[s]).start(priority=k % 2)
            return carry

        lax.fori_loop(0, COMB_TC, issue, 0, unroll=2)

    @pl.when(step == 0)
    def _():
        gather_tile(0, 0)

    @pl.when(step < last)
    def _():
        gather_tile(step + 1, 1 - slot)

    pltpu.make_async_copy(ys_ref.at[pl.ds(0, TOP_K * COMB_TC), :], buf_ref.at[slot], sems.at[slot]).wait()

    wt = wt_ref[...]
    acc_lo = acc_hi = None
    for k in range(TOP_K):
        lo, hi = _unpack_pairs_f32(buf_ref[slot, k * COMB_TC:(k + 1) * COMB_TC, :])
        w = wt[:, k:k + 1]
        acc_lo = w * lo if acc_lo is None else acc_lo + w * lo
        acc_hi = w * hi if acc_hi is None else acc_hi + w * hi
    o_ref[:, :D_PACK] = xm_ref[:, :D_PACK] + g2_ref[:, :D_PACK] * acc_lo
    o_ref[:, D_PACK:] = xm_ref[:, D_PACK:] + g2_ref[:, D_PACK:] * acc_hi


def _combine(ys, dest_flat, xm, wts, gate2):
    t = xm.shape[0]
    grid_spec = pltpu.PrefetchScalarGridSpec(
        num_scalar_prefetch=1,
        grid=(t // COMB_TC,),
        in_specs=[
            pl.BlockSpec((COMB_TC, D_MODEL), lambda i, d: (i, 0)),
            pl.BlockSpec((COMB_TC, LANES), lambda i, d: (i, 0)),
            pl.BlockSpec((1, D_MODEL), lambda i, d: (0, 0)),
            pl.BlockSpec(memory_space=pl.ANY),
        ],
        out_specs=pl.BlockSpec((COMB_TC, D_MODEL), lambda i, d: (i, 0)),
        scratch_shapes=[pltpu.VMEM((2, TOP_K * COMB_TC, D_PACK), jnp.uint32), pltpu.SemaphoreType.DMA((2,))],
    )
    return pl.pallas_call(
        _combine_kernel,
        grid_spec=grid_spec,
        out_shape=jax.ShapeDtypeStruct((t, D_MODEL), F32),
        compiler_params=_params(("arbitrary",)),
        name="combine",
    )(dest_flat, xm, wts, gate2, ys)


def _routing_tables(idx, rank, counts, n_tok):
    padded = ((counts + EXPERT_TM - 1) // EXPERT_TM) * EXPERT_TM
    pend = jnp.cumsum(padded)
    pstart = pend - padded
    experts = jnp.arange(N_EXPERTS, dtype=jnp.int32)
    start_of = jnp.sum(jnp.where(idx[..., None] == experts, pstart, 0), axis=-1)
    dest = (start_of + rank).astype(jnp.int32).reshape(-1)
    n_rows = n_tok * TOP_K + N_EXPERTS * EXPERT_TM
    nb = n_rows // EXPERT_TM
    block_start = jnp.arange(nb, dtype=jnp.int32) * EXPERT_TM
    block_expert = jnp.minimum(jnp.sum(pend[None, :] <= block_start[:, None], axis=1),
                               N_EXPERTS - 1).astype(jnp.int32)
    n_used = (pend[-1:] // EXPERT_TM).astype(jnp.int32)
    pad_lo = (pstart + counts).astype(jnp.int32)
    pad_hi = pend.astype(jnp.int32)

    blocks = jnp.arange(nb, dtype=jnp.int32)
    block_expert = jnp.where(blocks < n_used[0], block_expert, block_expert[jnp.maximum(n_used[0] - 1, 0)])
    prev = jnp.concatenate([block_expert[:1], block_expert[:-1]])
    first = jnp.logical_or(blocks == 0, block_expert != prev)
    first_pos = jnp.where(first, blocks, nb)
    next_first = jnp.flip(lax.cummin(jnp.flip(jnp.concatenate([first_pos[1:], jnp.full((1,), nb, jnp.int32)]))))
    nxt = jnp.where(next_first < nb, block_expert[jnp.minimum(next_first, nb - 1)], -1).astype(jnp.int32)
    rows_end = jnp.sum(jnp.where(block_expert[:, None] == experts, pad_lo, 0), axis=-1)
    full = (rows_end > block_start + EXPERT_TM // 2).astype(jnp.int32)
    tables = (block_expert, n_used, first.astype(jnp.int32), nxt, full)
    return dest, tables, pad_lo, pad_hi, n_rows


def _layer(x2, c, w_ada, b_ada, g_norm1, w_in, b_in, g_q, g_k, sinks, rel_bias, w_dw, b_dw, ln_g, ln_b,
           g_out_attn, g_out_conv, w_out, b_out, g_norm2, w_router, b_router,
           w_gate, b_gate, w_up, b_up, w_down, b_down):
    t = x2.shape[0]
    row = lambda v: v.reshape(1, -1)
    mod = _ada(c, w_ada, b_ada)
    shift1, scale1, gate1, shift2, scale2, gate2 = [mod[:, i * D_MODEL:(i + 1) * D_MODEL] for i in range(6)]

    q, kv, uc = _in_proj(x2, row(g_norm1), scale1, shift1, w_in.astype(BF16), row(b_in))

    gq_t = row(jnp.tile(g_q, N_Q_HEADS)) * (HEAD_DIM ** -0.5)
    gk_t = row(jnp.tile(g_k, N_KV_HEADS))
    ma = _attention(q, kv, _bias_table(rel_bias), sinks, gq_t, gk_t, row(g_out_attn))
    w_dw_p = jnp.concatenate([w_dw, jnp.zeros((HALO - CONV_WIDTH, D_CONV), w_dw.dtype)], axis=0)
    mc = _conv(uc, w_dw_p, row(b_dw), row(ln_g), row(ln_b), row(g_out_conv))

    xm, h2, idx, rank, wts, counts = _out_proj(
        ma, mc, x2, w_out.astype(BF16), row(b_out), gate1, row(g_norm2), scale2, shift2,
        w_router, row(b_router))

    dest, tables, pad_lo, pad_hi, n_rows = _routing_tables(idx[:, :TOP_K], rank[:, :TOP_K], counts[0], t)
    xs = _dispatch(h2, dest, pad_lo, pad_hi, tables[1], n_rows)
    hs = _up(xs, tables, w_gate, b_gate, w_up, b_up)
    ys = _down(hs, tables, w_down, b_down)
    return _combine(ys, dest, xm, wts, gate2)


def kernel(x, c, w_ada, b_ada, g_norm1, w_in, b_in, g_q, g_k, sinks, rel_bias, w_dw, b_dw, ln_g, ln_b,
           g_out_attn, g_out_conv, w_out, b_out, g_norm2, w_router, b_router,
           w_gate, b_gate, w_up, b_up, w_down, b_down):
    b, t, d = x.shape
    assert b == 1 and d == D_MODEL and w_ada.shape[0] == 1
    out = _layer(x.reshape(t, d), c, w_ada[0], b_ada[0], g_norm1[0], w_in[0], b_in[0], g_q[0], g_k[0],
                 sinks[0], rel_bias, w_dw[0], b_dw[0], ln_g[0], ln_b[0], g_out_attn[0], g_out_conv[0],
                 w_out[0], b_out[0], g_norm2[0], w_router[0], b_router[0],
                 w_gate[0], b_gate[0], w_up[0], b_up[0], w_down[0], b_down[0])
    return out.reshape(b, t, d)
```

```python
import functools
import math

import jax
import jax.numpy as jnp
from jax import lax
from jax.experimental import pallas as pl
from jax.experimental.pallas import tpu as pltpu

D_MODEL = 2048
HEAD_DIM = 64
N_Q_HEADS = 16
N_KV_HEADS = 2
D_ATTN = N_Q_HEADS * HEAD_DIM
D_KV = N_KV_HEADS * HEAD_DIM
D_CONV = D_MODEL - D_ATTN
D_IN = D_ATTN + 2 * D_KV + 2 * D_CONV
WINDOW = 128
BLOCK = 128
CONV_WIDTH = 31
N_BUCKETS = 32
MAX_DISTANCE = 128
N_EXPERTS = 32
TOP_K = 4
D_FF = D_MODEL
SWIGLU_LIMIT = 7.0
SWIGLU_ALPHA = 1.702
EPS = 1e-6
NEG_INF = -1e30

LANES = 128
SUBLANES = 8
VMEM_LIMIT = 56 * 1024 * 1024

HALO = 32
EXPERT_TM = 512
EXPERT_VMEM_LIMIT = 60 * 1024 * 1024
F32 = jnp.float32
BF16 = jnp.bfloat16


def _params(sem, vmem_limit=VMEM_LIMIT):
    return pltpu.CompilerParams(dimension_semantics=sem, vmem_limit_bytes=vmem_limit)


ADA_TN = 1024
ADA_RC = 256


def _ada_kernel(c_ref, w_ref, b_ref, o_ref):
    tn = w_ref.shape[1]

    def body(i, acc):
        r = pl.multiple_of(i * ADA_RC, ADA_RC)
        c = c_ref[pl.ds(r, ADA_RC), :]
        sc = c * jax.nn.sigmoid(c)
        prod = w_ref[pl.ds(r, ADA_RC), :] * sc
        return acc + jnp.sum(prod.reshape(ADA_RC // 8, 8, tn), axis=0)

    acc = lax.fori_loop(0, D_MODEL // ADA_RC, body, jnp.zeros((8, tn), F32))
    o_ref[...] = jnp.sum(acc, axis=0, keepdims=True) + b_ref[...]


def _ada(c, w_ada, b_ada):
    n = w_ada.shape[1]
    return pl.pallas_call(
        _ada_kernel,
        grid=(n // ADA_TN,),
        in_specs=[
            pl.BlockSpec((D_MODEL, 1), lambda j: (0, 0)),
            pl.BlockSpec((D_MODEL, ADA_TN), lambda j: (0, j)),
            pl.BlockSpec((1, ADA_TN), lambda j: (0, j)),
        ],
        out_specs=pl.BlockSpec((1, ADA_TN), lambda j: (0, j)),
        out_shape=jax.ShapeDtypeStruct((1, n), F32),
        compiler_params=_params(("parallel",)),
        name="ada",
    )(c.reshape(D_MODEL, 1), w_ada, b_ada.reshape(1, n))


IN_TM = 512
IN_NC = 256


def _modulated_rms(x, g, scale, shift):
    ms = jnp.mean(x * x, axis=-1, keepdims=True)
    return (x * lax.rsqrt(ms + EPS) * g) * (1.0 + scale) + shift


def _in_kernel(x_ref, g_ref, sc_ref, sh_ref, w_ref, b_ref, q_ref, kv_ref, uc_ref):
    h = _modulated_rms(x_ref[...], g_ref[...], sc_ref[...], sh_ref[...]).astype(BF16)

    def proj(lo, n, o_ref):
        for c in range(0, n, IN_NC):
            u = jnp.dot(h, w_ref[:, lo + c:lo + c + IN_NC], preferred_element_type=F32)
            o_ref[:, c:c + IN_NC] = u + b_ref[:, lo + c:lo + c + IN_NC]

    proj(0, D_ATTN, q_ref)
    proj(D_ATTN, 2 * D_KV, kv_ref)
    proj(D_ATTN + 2 * D_KV, 2 * D_CONV, uc_ref)


def _in_proj(x2, g1, scale1, shift1, w_in_bf, b_in):
    t = x2.shape[0]
    vec = lambda n: pl.BlockSpec((1, n), lambda i: (0, 0))
    return pl.pallas_call(
        _in_kernel,
        grid=(t // IN_TM,),
        in_specs=[
            pl.BlockSpec((IN_TM, D_MODEL), lambda i: (i, 0)),
            vec(D_MODEL), vec(D_MODEL), vec(D_MODEL),
            pl.BlockSpec((D_MODEL, D_IN), lambda i: (0, 0)),
            vec(D_IN),
        ],
        out_specs=[
            pl.BlockSpec((IN_TM, D_ATTN), lambda i: (i, 0)),
            pl.BlockSpec((IN_TM, 2 * D_KV), lambda i: (i, 0)),
            pl.BlockSpec((IN_TM, 2 * D_CONV), lambda i: (i, 0)),
        ],
        out_shape=[
            jax.ShapeDtypeStruct((t, D_ATTN), F32),
            jax.ShapeDtypeStruct((t, 2 * D_KV), F32),
            jax.ShapeDtypeStruct((t, 2 * D_CONV), F32),
        ],
        compiler_params=_params(("parallel",)),
        name="in_proj",
    )(x2, g1, scale1, shift1, w_in_bf, b_in)


ATT_R = 4


def _split_dot(a, b_bf):
    hi = a.astype(BF16)
    lo = (a - hi.astype(F32)).astype(BF16)
    return (jnp.dot(hi, b_bf, preferred_element_type=F32)
            + jnp.dot(lo, b_bf, preferred_element_type=F32))


def _attn_kernel(sinks_ref, q_ref, kvp_ref, kvc_ref, bias_ref, gq_ref, gk_ref, go_ref,
                 hq_ref, hqt_ref, hk_ref, o_ref, y_ref):
    step = pl.program_id(0)
    lane = lax.broadcasted_iota(jnp.int32, (2 * BLOCK, LANES), 1)
    low = lane < HEAD_DIM
    col = lax.broadcasted_iota(jnp.int32, (BLOCK, 2 * BLOCK), 1)
    first_lo = jnp.where(step == 0, BLOCK, 0)

    for r in range(ATT_R):
        rows = slice(r * BLOCK, (r + 1) * BLOCK)
        q = q_ref[rows, :]
        ssq = _split_dot(_split_dot(q * q, hq_ref[...]), hqt_ref[...])
        qn = (q * lax.rsqrt(ssq * (1.0 / HEAD_DIM) + EPS) * gq_ref[...]).astype(BF16)

        if r == 0:
            kv_prev = kvp_ref[...]
        else:
            kv_prev = kvc_ref[(r - 1) * BLOCK:r * BLOCK, :]
        kv = jnp.concatenate([kv_prev, kvc_ref[rows, :]], axis=0)
        k = kv[:, :LANES]
        v = kv[:, LANES:]
        kss = _split_dot(k * k, hk_ref[...])
        kn = k * lax.rsqrt(kss * (1.0 / HEAD_DIM) + EPS) * gk_ref[...]
        kn_sw = pltpu.roll(kn, HEAD_DIM, axis=1)
        v_sw = pltpu.roll(v, HEAD_DIM, axis=1)
        zero = jnp.zeros_like(kn)
        k_lo = [jnp.where(low, kn, zero).astype(BF16), jnp.where(low, kn_sw, zero).astype(BF16)]
        k_hi = [jnp.where(low, zero, kn_sw).astype(BF16), jnp.where(low, zero, kn).astype(BF16)]
        v_lo = [jnp.where(low, v, zero).astype(BF16), jnp.where(low, v_sw, zero).astype(BF16)]
        v_hi = [jnp.where(low, zero, v_sw).astype(BF16), jnp.where(low, zero, v).astype(BF16)]

        for p in range(N_Q_HEADS // 2):
            g = (2 * p) // (N_Q_HEADS // N_KV_HEADS)
            qp = qn[:, p * LANES:(p + 1) * LANES]
            acc = None
            for half, (kz, vz) in enumerate(((k_lo[g], v_lo[g]), (k_hi[g], v_hi[g]))):
                h = 2 * p + half
                s = lax.dot_general(qp, kz, (((1,), (1,)), ((), ())), preferred_element_type=F32)
                s = s + bias_ref[h]
                if r == 0:
                    s = jnp.where(col >= first_lo, s, NEG_INF)
                sink = sinks_ref[h]
                m = jnp.maximum(jnp.max(s, axis=-1, keepdims=True), sink)
                e = jnp.exp(s - m)
                denom = jnp.sum(e, axis=-1, keepdims=True) + jnp.exp(sink - m)
                pv = jnp.dot(e.astype(BF16), vz, preferred_element_type=F32)
                pv = pv * (1.0 / denom)
                acc = pv if acc is None else acc + pv
            y_ref[:, p * LANES:(p + 1) * LANES] = acc

        y = y_ref[...]
        ms = jnp.mean(y * y, axis=-1, keepdims=True)
        o_ref[rows, :] = (y * lax.rsqrt(ms + EPS) * go_ref[...]).astype(BF16)


def _attention(q, kv, bias, sinks, gq_t, gk_t, g_out):
    t = q.shape[0]
    tile = ATT_R * BLOCK
    head_of_lane = jnp.arange(D_ATTN) // HEAD_DIM
    hq = (head_of_lane[:, None] == jnp.arange(LANES)[None, :]).astype(BF16)
    hk = _head_indicator(LANES)
    grid_spec = pltpu.PrefetchScalarGridSpec(
        num_scalar_prefetch=0,
        grid=(t // tile,),
        in_specs=[
            pl.BlockSpec(memory_space=pltpu.SMEM),
            pl.BlockSpec((tile, D_ATTN), lambda i: (i, 0)),
            pl.BlockSpec((BLOCK, 2 * D_KV), lambda i: (jnp.maximum(i * ATT_R - 1, 0), 0)),
            pl.BlockSpec((tile, 2 * D_KV), lambda i: (i, 0)),
            pl.BlockSpec((N_Q_HEADS, BLOCK, 2 * BLOCK), lambda i: (0, 0, 0)),
            pl.BlockSpec((1, D_ATTN), lambda i: (0, 0)),
            pl.BlockSpec((1, LANES), lambda i: (0, 0)),
            pl.BlockSpec((1, D_ATTN), lambda i: (0, 0)),
            pl.BlockSpec((D_ATTN, LANES), lambda i: (0, 0)),
            pl.BlockSpec((LANES, D_ATTN), lambda i: (0, 0)),
            pl.BlockSpec((LANES, LANES), lambda i: (0, 0)),
        ],
        out_specs=pl.BlockSpec((tile, D_ATTN), lambda i: (i, 0)),
        scratch_shapes=[pltpu.VMEM((BLOCK, D_ATTN), F32)],
    )
    return pl.pallas_call(
        _attn_kernel,
        grid_spec=grid_spec,
        out_shape=jax.ShapeDtypeStruct((t, D_ATTN), BF16),
        compiler_params=_params(("parallel",)),
        name="attn",
    )(sinks, q, kv, kv, bias, gq_t, gk_t, g_out, hq, hq.T, hk)


def _t5_bucket(dist):
    max_exact = N_BUCKETS // 2
    d = jnp.maximum(dist, 0)
    log_ratio = jnp.log(jnp.maximum(d, max_exact).astype(F32) / max_exact)
    large = max_exact + (log_ratio / math.log(MAX_DISTANCE / max_exact)
                         * (N_BUCKETS - max_exact)).astype(jnp.int32)
    large = jnp.minimum(large, N_BUCKETS - 1)
    return jnp.where(d < max_exact, d, large)


def _bias_table(rel_bias):
    q_local = jnp.arange(BLOCK, dtype=jnp.int32) + BLOCK
    k_local = jnp.arange(2 * BLOCK, dtype=jnp.int32)
    dist = q_local[:, None] - k_local[None, :]
    band = (dist >= 0) & (dist < WINDOW)
    bucket = _t5_bucket(dist)
    hit = (jnp.arange(N_BUCKETS, dtype=jnp.int32)[:, None] == bucket.reshape(1, -1)).astype(F32)
    bias = jnp.dot(rel_bias.astype(F32).T, hit, precision=lax.Precision.HIGHEST)
    return jnp.where(band[None], bias.reshape(N_Q_HEADS, BLOCK, 2 * BLOCK), NEG_INF)


def _head_indicator(n):
    i = jnp.arange(n) // HEAD_DIM
    return (i[:, None] == i[None, :]).astype(BF16)


CONV_TT = 512
CONV_RC = 64
CONV_CC = 256


def _conv_kernel(u_ref, halo_ref, w_ref, b_ref, lg_ref, lb_ref, go_ref, o_ref, h_ref, y_ref):
    step = pl.program_id(0)

    def glu(u):
        return u[:, :D_CONV] * jax.nn.sigmoid(u[:, D_CONV:])

    hh = glu(halo_ref[...])
    h_ref[0, 0:HALO, :] = jnp.where(step == 0, jnp.zeros_like(hh), hh)
    h_ref[0, HALO:, :] = glu(u_ref[...])

    n_rows = CONV_TT + HALO
    for s in range(1, SUBLANES):
        for c0 in range(0, D_CONV, CONV_CC):
            cs = slice(c0, c0 + CONV_CC)
            h_ref[s, :, cs] = pltpu.roll(h_ref[0, :, cs], n_rows - s, axis=0)

    off = HALO - (CONV_WIDTH - 1)
    for r0 in range(0, CONV_TT, CONV_RC):
        for c0 in range(0, D_CONV, CONV_CC):
            cs = slice(c0, c0 + CONV_CC)
            acc = jnp.broadcast_to(b_ref[:, cs], (CONV_RC, CONV_CC))
            for j in range(CONV_WIDTH):
                s = (r0 + off + j) % SUBLANES
                a = r0 + off + j - s
                acc = acc + w_ref[j:j + 1, cs] * h_ref[s, a:a + CONV_RC, cs]
            y_ref[r0:r0 + CONV_RC, cs] = acc

    y = y_ref[...]
    mu = jnp.mean(y, axis=-1, keepdims=True)
    yc = y - mu
    var = jnp.mean(yc * yc, axis=-1, keepdims=True)
    z = yc * lax.rsqrt(var + EPS) * lg_ref[...] + lb_ref[...]
    s = z * jax.nn.sigmoid(z)
    ms = jnp.mean(s * s, axis=-1, keepdims=True)
    o_ref[...] = (s * lax.rsqrt(ms + EPS) * go_ref[...]).astype(BF16)


def _conv(uc, w_dw, b_dw, ln_g, ln_b, g_out):
    t = uc.shape[0]
    vec = lambda: pl.BlockSpec((1, D_CONV), lambda i: (0, 0))
    per = CONV_TT // HALO
    return pl.pallas_call(
        _conv_kernel,
        grid=(t // CONV_TT,),
        in_specs=[
            pl.BlockSpec((CONV_TT, 2 * D_CONV), lambda i: (i, 0)),
            pl.BlockSpec((HALO, 2 * D_CONV), lambda i: (jnp.maximum(i * per - 1, 0), 0)),
            pl.BlockSpec((HALO, D_CONV), lambda i: (0, 0)),
            vec(), vec(), vec(), vec(),
        ],
        out_specs=pl.BlockSpec((CONV_TT, D_CONV), lambda i: (i, 0)),
        out_shape=jax.ShapeDtypeStruct((t, D_CONV), BF16),
        scratch_shapes=[pltpu.VMEM((SUBLANES, CONV_TT + HALO, D_CONV), F32),
                        pltpu.VMEM((CONV_TT, D_CONV), F32)],
        compiler_params=_params(("parallel",)),
        name="conv",
    )(uc, uc, w_dw, b_dw, ln_g, ln_b, g_out)


OUT_TM = 512
D_PACK = D_MODEL // 2
HI_MASK = 0xFFFF0000


def _pack_bf16_pairs(h):
    lo = lax.bitcast_convert_type(h[:, :D_PACK].astype(BF16).astype(F32), jnp.uint32)
    hi = lax.bitcast_convert_type(h[:, D_PACK:].astype(BF16).astype(F32), jnp.uint32)
    return (lo >> 16) | (hi & jnp.uint32(HI_MASK))


def _unpack_pairs_f32(w):
    lo = lax.bitcast_convert_type(w << 16, F32)
    hi = lax.bitcast_convert_type(w & jnp.uint32(HI_MASK), F32)
    return lo, hi


def _unpack_bf16_pairs(w):
    lo, hi = _unpack_pairs_f32(w)
    return lo.astype(BF16), hi.astype(BF16)


def _out_kernel(ma_ref, mc_ref, x_ref, w_ref, bo_ref, g1_ref, g2_ref, sc_ref, sh_ref, wr_ref, br_ref,
                tri_ref, xm_ref, h2_ref, idx_ref, rank_ref, wt_ref, cnt_ref, carry_ref):
    step = pl.program_id(0)

    @pl.when(step == 0)
    def _():
        carry_ref[...] = jnp.zeros_like(carry_ref)

    y = (jnp.dot(ma_ref[...], w_ref[0:D_ATTN, :], preferred_element_type=F32)
         + jnp.dot(mc_ref[...], w_ref[D_ATTN:, :], preferred_element_type=F32) + bo_ref[...])
    xm = x_ref[...] + g1_ref[...] * y
    xm_ref[...] = xm
    h2 = _modulated_rms(xm, g2_ref[...], sc_ref[...], sh_ref[...])
    h2_ref[...] = _pack_bf16_pairs(h2)

    both = jnp.dot(h2.astype(BF16), wr_ref[...], preferred_element_type=F32)
    logits = both[:, :N_EXPERTS] + both[:, N_EXPERTS:] + br_ref[...]
    tm = logits.shape[0]
    lane = lax.broadcasted_iota(jnp.int32, (tm, N_EXPERTS), 1).astype(F32)
    vals, idxs = [], []
    l = logits
    for _ in range(TOP_K):
        m = jnp.max(l, axis=-1, keepdims=True)
        i = jnp.min(jnp.where(l == m, lane, float(N_EXPERTS)), axis=-1, keepdims=True)
        vals.append(m)
        idxs.append(i)
        l = jnp.where(lane == i, -jnp.inf, l)
    es = [jnp.exp(v - vals[0]) for v in vals]
    tot = es[0] + es[1] + es[2] + es[3]
    ws = [e / tot for e in es]

    hot = [(lane == i).astype(F32) for i in idxs]
    hot_all = hot[0] + hot[1] + hot[2] + hot[3]
    before = jnp.dot(tri_ref[...], hot_all.astype(BF16), preferred_element_type=F32) + carry_ref[...]
    ranks = [jnp.sum(h * before, axis=-1, keepdims=True) for h in hot]
    carry_ref[...] = carry_ref[...] + jnp.sum(hot_all, axis=0, keepdims=True)
    cnt_ref[...] = carry_ref[...].astype(jnp.int32)

    slot = lax.broadcasted_iota(jnp.int32, (tm, LANES), 1)

    def pack(cols):
        out = jnp.zeros((tm, LANES), F32)
        for k in range(TOP_K):
            out = jnp.where(slot == k, cols[k], out)
        return out

    idx_ref[...] = pack(idxs).astype(jnp.int32)
    rank_ref[...] = pack(ranks).astype(jnp.int32)
    wt_ref[...] = pack(ws)


def _out_proj(ma, mc, x2, w_out_bf, b_out, gate1, g2, scale2, shift2, w_router, b_router):
    t = x2.shape[0]
    vec = lambda n: pl.BlockSpec((1, n), lambda i: (0, 0))
    tri = jnp.tril(jnp.ones((OUT_TM, OUT_TM), F32), -1).astype(BF16)
    wr_hi = w_router.astype(BF16)
    wr_lo = (w_router - wr_hi.astype(F32)).astype(BF16)
    w_router = jnp.concatenate([wr_hi, wr_lo], axis=1)
    return pl.pallas_call(
        _out_kernel,
        grid=(t // OUT_TM,),
        in_specs=[
            pl.BlockSpec((OUT_TM, D_ATTN), lambda i: (i, 0)),
            pl.BlockSpec((OUT_TM, D_CONV), lambda i: (i, 0)),
            pl.BlockSpec((OUT_TM, D_MODEL), lambda i: (i, 0)),
            pl.BlockSpec((D_MODEL, D_MODEL), lambda i: (0, 0)),
            vec(D_MODEL), vec(D_MODEL), vec(D_MODEL), vec(D_MODEL), vec(D_MODEL),
            pl.BlockSpec((D_MODEL, 2 * N_EXPERTS), lambda i: (0, 0)),
            vec(N_EXPERTS),
            pl.BlockSpec((OUT_TM, OUT_TM), lambda i: (0, 0)),
        ],
        out_specs=[
            pl.BlockSpec((OUT_TM, D_MODEL), lambda i: (i, 0)),
            pl.BlockSpec((OUT_TM, D_PACK), lambda i: (i, 0)),
            pl.BlockSpec((OUT_TM, LANES), lambda i: (i, 0)),
            pl.BlockSpec((OUT_TM, LANES), lambda i: (i, 0)),
            pl.BlockSpec((OUT_TM, LANES), lambda i: (i, 0)),
            pl.BlockSpec((1, N_EXPERTS), lambda i: (0, 0)),
        ],
        out_shape=[
            jax.ShapeDtypeStruct((t, D_MODEL), F32),
            jax.ShapeDtypeStruct((t, D_PACK), jnp.uint32),
            jax.ShapeDtypeStruct((t, LANES), jnp.int32),
            jax.ShapeDtypeStruct((t, LANES), jnp.int32),
            jax.ShapeDtypeStruct((t, LANES), F32),
            jax.ShapeDtypeStruct((1, N_EXPERTS), jnp.int32),
        ],
        scratch_shapes=[pltpu.VMEM((1, N_EXPERTS), F32)],
        compiler_params=_params(("arbitrary",)),
        name="out_proj",
    )(ma, mc, x2, w_out_bf, b_out, gate1, g2, scale2, shift2, w_router, b_router, tri)


DISP_TD = 512
DISP_NBUF = 3


class _CopyGroup:
    def __init__(self, copies):
        self.copies = copies

    def start(self):
        for cp in self.copies:
            cp.start()

    def wait(self):
        for cp in self.copies:
            cp.wait()

ZERO_RUN = 64


def _dispatch_kernel(dest_ref, pad_lo_ref, pad_hi_ref, nu_ref, h2_ref, xs_ref,
                     stage_ref, zero_ref, in_sems, out_sems, zsem):
    step = pl.program_id(0)
    last = pl.num_programs(0) - 1
    slot = lax.rem(step, DISP_NBUF)
    rows_per_chunk = DISP_TD * TOP_K

    def load(chunk, s):
        rows = pl.ds(pl.multiple_of(chunk * DISP_TD, DISP_TD), DISP_TD)
        return _CopyGroup([pltpu.make_async_copy(h2_ref.at[rows, pl.ds(j * LANES, LANES)], stage_ref.at[s, :, j, :],
                                                 in_sems.at[s]) for j in range(stage_ref.shape[2])])

    def wait_chunk(s):
        whole = xs_ref.at[pl.ds(0, rows_per_chunk)]
        pltpu.make_async_copy(whole, whole, out_sems.at[s]).wait()

    @pl.when(step == 0)
    def _():
        load(0, 0).start()

        @pl.when(last >= 1)
        def _():
            load(1, 1).start()

    load(step, slot).wait()

    def issue(r, carry):
        for k in range(TOP_K):
            d = dest_ref[(step * DISP_TD + r) * TOP_K + k]
            pltpu.make_async_copy(stage_ref.at[slot, r], xs_ref.at[d], out_sems.at[slot]).start(priority=k % 2)
        return carry

    lax.fori_loop(0, DISP_TD, issue, 0, unroll=2)

    @pl.when(step == 0)
    def _():
        zero_ref[...] = jnp.zeros_like(zero_ref)

        def zero_rows(d, n):
            d = d if n == 1 else pl.multiple_of(d, n)
            return pltpu.make_async_copy(zero_ref.at[pl.ds(0, n)], xs_ref.at[pl.ds(d, n)], zsem)

        def ranges(e):
            lo, hi = pad_lo_ref[e], pad_hi_ref[e]
            a = jnp.minimum((lo + SUBLANES - 1) // SUBLANES * SUBLANES, hi)
            b = jnp.minimum((lo + ZERO_RUN - 1) // ZERO_RUN * ZERO_RUN, hi)
            return ((lo, a - lo, 1), (a, (b - a) // SUBLANES, SUBLANES), (b, (hi - b) // ZERO_RUN, ZERO_RUN))

        def for_all_runs(act):
            def per_expert(e, carry):
                for start, count, n in ranges(e):
                    lax.fori_loop(0, count, lambda i, c, s=start, n=n: (act(zero_rows(s + i * n, n)), c)[1], 0)
                return carry

            lax.fori_loop(0, N_EXPERTS, per_expert, 0)
            n_blocks = xs_ref.shape[0] // EXPERT_TM
            lax.fori_loop(nu_ref[0], n_blocks,
                          lambda blk, c: (act(zero_rows(blk * EXPERT_TM, EXPERT_TM)), c)[1], 0)

        for_all_runs(lambda copy: copy.start())
        for_all_runs(lambda copy: copy.wait())

    @pl.when(step > 0)
    def _():
        wait_chunk(lax.rem(step + DISP_NBUF - 1, DISP_NBUF))

    @pl.when(step + 2 <= last)
    def _():
        load(step + 2, lax.rem(step + 2, DISP_NBUF)).start()

    @pl.when(step == last)
    def _():
        wait_chunk(slot)


def _dispatch(h2p, dest_flat, pad_lo, pad_hi, n_used, n_rows):
    t = h2p.shape[0]
    grid_spec = pltpu.PrefetchScalarGridSpec(
        num_scalar_prefetch=4,
        grid=(t // DISP_TD,),
        in_specs=[pl.BlockSpec(memory_space=pl.ANY)],
        out_specs=pl.BlockSpec(memory_space=pl.ANY),
        scratch_shapes=[pltpu.VMEM((DISP_NBUF, DISP_TD, D_PACK // LANES, LANES), jnp.uint32),
                        pltpu.VMEM((EXPERT_TM, D_PACK // LANES, LANES), jnp.uint32),
                        pltpu.SemaphoreType.DMA((DISP_NBUF,)), pltpu.SemaphoreType.DMA((DISP_NBUF,)),
                        pltpu.SemaphoreType.DMA(())],
    )
    return pl.pallas_call(
        _dispatch_kernel,
        grid_spec=grid_spec,
        out_shape=jax.ShapeDtypeStruct((n_rows, D_PACK // LANES, LANES), jnp.uint32),
        compiler_params=_params(("arbitrary",)),
        name="dispatch",
    )(dest_flat, pad_lo, pad_hi, n_used, h2p)


UP_TF = 1024
DOWN_TN = 2048
UP_AHEAD = 2


def _weight_index_map(n_chunks):
    def index_map(c, b, be, nu, first, nxt, full):
        del nu, full
        in_last = nxt[b] < 0
        wrap = jnp.logical_and(in_last, c + 1 < n_chunks)
        e_next = jnp.where(in_last, jnp.where(wrap, be[0], be[b]), nxt[b])
        c_next = jnp.where(wrap, c + 1, c)
        is_first = first[b] == 1
        return jnp.where(is_first, be[b], e_next), 0, jnp.where(is_first, c, c_next)

    return index_map


def _row_cases(b, nu_ref, full_ref, o_ref, compute):
    half = EXPERT_TM // 2
    used = b < nu_ref[0]

    @pl.when(jnp.logical_and(used, full_ref[b] == 1))
    def _():
        compute(EXPERT_TM)

    @pl.when(jnp.logical_and(used, full_ref[b] == 0))
    def _():
        compute(half)
        o_ref[half:, :] = jnp.zeros((half, o_ref.shape[1]), o_ref.dtype)

    @pl.when(jnp.logical_not(used))
    def _():
        o_ref[...] = jnp.zeros_like(o_ref)


def _up_kernel(be_ref, nu_ref, first_ref, nxt_ref, full_ref, xs_ref, wg_ref, wu_ref, bg_ref, bu_ref, h_ref,
               wg_bf, wu_bf, xbuf_ref, xsems):
    f = pl.program_id(0)
    b = pl.program_id(1)
    nb = pl.num_programs(1)
    step = f * nb + b
    n_steps = pl.num_programs(0) * nb
    n_buf = UP_AHEAD + 1
    slot = lax.rem(step, n_buf)
    n_sub = xs_ref.shape[1]

    def fetch(ahead):
        blk = lax.rem(b + ahead, nb)
        rows = pl.ds(pl.multiple_of(jnp.minimum(blk, nu_ref[0] - 1) * EXPERT_TM, EXPERT_TM), EXPERT_TM)
        s = lax.rem(step + ahead, n_buf)
        return _CopyGroup([pltpu.make_async_copy(xs_ref.at[rows, j, :], xbuf_ref.at[s, :, pl.ds(j * LANES, LANES)],
                                                 xsems.at[s]) for j in range(n_sub)])

    @pl.when(step == 0)
    def _():
        for ahead in range(UP_AHEAD):
            @pl.when(ahead < n_steps)
            def _():
                fetch(ahead).start()

    @pl.when(step + UP_AHEAD < n_steps)
    def _():
        fetch(UP_AHEAD).start()

    fetch(0).wait()
    x_ref = xbuf_ref.at[slot]

    @pl.when(first_ref[b] == 1)
    def _():
        wg_bf[...] = wg_ref[0].astype(BF16)
        wu_bf[...] = wu_ref[0].astype(BF16)

    def compute(m):
        x_lo, x_hi = _unpack_bf16_pairs(x_ref[0:m, :])

        def proj(w_bf, bias_ref):
            return (jnp.dot(x_lo, w_bf[0:D_PACK, :], preferred_element_type=F32)
                    + jnp.dot(x_hi, w_bf[D_PACK:, :], preferred_element_type=F32) + bias_ref[0])

        g = proj(wg_bf, bg_ref)
        lin = proj(wu_bf, bu_ref)
        g = jnp.minimum(g, SWIGLU_LIMIT)
        lin = jnp.clip(lin, -SWIGLU_LIMIT, SWIGLU_LIMIT)
        act = g * jax.nn.sigmoid(SWIGLU_ALPHA * g) * (lin + 1.0)
        h_ref[0:m, :] = act.astype(BF16)

    _row_cases(b, nu_ref, full_ref, h_ref, compute)


def _up(xs, tables, w_gate, b_gate, w_up, b_up):
    n_rows = xs.shape[0]
    nb = n_rows // EXPERT_TM
    n_chunks = D_FF // UP_TF
    wsel = _weight_index_map(n_chunks)
    bsel = lambda f, b, be, nu, first, nxt, full: (be[b], 0, f)
    grid_spec = pltpu.PrefetchScalarGridSpec(
        num_scalar_prefetch=5,
        grid=(n_chunks, nb),
        in_specs=[
            pl.BlockSpec(memory_space=pl.ANY),
            pl.BlockSpec((1, D_MODEL, UP_TF), wsel),
            pl.BlockSpec((1, D_MODEL, UP_TF), wsel),
            pl.BlockSpec((1, 1, UP_TF), bsel),
            pl.BlockSpec((1, 1, UP_TF), bsel),
        ],
        out_specs=pl.BlockSpec((EXPERT_TM, UP_TF), lambda f, b, be, nu, first, nxt, full: (b, f)),
        scratch_shapes=[pltpu.VMEM((D_MODEL, UP_TF), BF16), pltpu.VMEM((D_MODEL, UP_TF), BF16),
                        pltpu.VMEM((UP_AHEAD + 1, EXPERT_TM, D_PACK), jnp.uint32),
                        pltpu.SemaphoreType.DMA((UP_AHEAD + 1,))],
    )
    return pl.pallas_call(
        _up_kernel,
        grid_spec=grid_spec,
        out_shape=jax.ShapeDtypeStruct((n_rows, D_FF), BF16),
        compiler_params=_params(("arbitrary", "arbitrary"), EXPERT_VMEM_LIMIT),
        name="expert_up",
    )(*tables, xs, w_gate, w_up,
      b_gate.reshape(N_EXPERTS, 1, D_FF), b_up.reshape(N_EXPERTS, 1, D_FF))


def _down_kernel(be_ref, nu_ref, first_ref, nxt_ref, full_ref, h_ref, wd_ref, bd_ref, y_ref, wd_bf):
    b = pl.program_id(1)

    @pl.when(first_ref[b] == 1)
    def _():
        wd_bf[...] = wd_ref[0].astype(BF16)

    def compute(m):
        y = jnp.dot(h_ref[0:m, :], wd_bf[...], preferred_element_type=F32) + bd_ref[0]
        y_ref[0:m, :] = _pack_bf16_pairs(y)

    _row_cases(b, nu_ref, full_ref, y_ref, compute)


def _down(hs, tables, w_down, b_down):
    n_rows = hs.shape[0]
    nb = n_rows // EXPERT_TM
    n_chunks = D_MODEL // DOWN_TN
    assert n_chunks == 1
    row = lambda n, b, be, nu, first, nxt, full: (jnp.minimum(b, nu[0] - 1), 0)
    grid_spec = pltpu.PrefetchScalarGridSpec(
        num_scalar_prefetch=5,
        grid=(n_chunks, nb),
        in_specs=[
            pl.BlockSpec((EXPERT_TM, D_FF), row),
            pl.BlockSpec((1, D_FF, DOWN_TN), _weight_index_map(n_chunks)),
            pl.BlockSpec((1, 1, DOWN_TN), lambda n, b, be, nu, first, nxt, full: (be[b], 0, n)),
        ],
        out_specs=pl.BlockSpec((EXPERT_TM, D_PACK), lambda n, b, be, nu, first, nxt, full: (b, 0)),
        scratch_shapes=[pltpu.VMEM((D_FF, DOWN_TN), BF16)],
    )
    return pl.pallas_call(
        _down_kernel,
        grid_spec=grid_spec,
        out_shape=jax.ShapeDtypeStruct((n_rows, D_PACK), jnp.uint32),
        compiler_params=_params(("arbitrary", "arbitrary"), EXPERT_VMEM_LIMIT),
        name="expert_down",
    )(*tables, hs, w_down, b_down.reshape(N_EXPERTS, 1, D_MODEL))


COMB_TC = 256


def _combine_kernel(dest_ref, xm_ref, wt_ref, g2_ref, ys_ref, o_ref, buf_ref, sems):
    step = pl.program_id(0)
    last = pl.num_programs(0) - 1
    slot = step % 2

    def gather_tile(tile, s):
        def issue(r, carry):
            for k in range(TOP_K):
                d = dest_ref[(tile * COMB_TC + r) * TOP_K + k]
                pltpu.make_async_copy(ys_ref.at[pl.ds(d, 1), :], buf_ref.at[s, pl.ds(k * COMB_TC + r, 1), :],
                                      sems.at[s]).start(priority=k % 2)
            return carry

        lax.fori_loop(0, COMB_TC, issue, 0, unroll=2)

    @pl.when(step == 0)
    def _():
        gather_tile(0, 0)

    @pl.when(step < last)
    def _():
        gather_tile(step + 1, 1 - slot)

    pltpu.make_async_copy(ys_ref.at[pl.ds(0, TOP_K * COMB_TC), :], buf_ref.at[slot], sems.at[slot]).wait()

    wt = wt_ref[...]
    acc_lo = acc_hi = None
    for k in range(TOP_K):
        lo, hi = _unpack_pairs_f32(buf_ref[slot, k * COMB_TC:(k + 1) * COMB_TC, :])
        w = wt[:, k:k + 1]
        acc_lo = w * lo if acc_lo is None else acc_lo + w * lo
        acc_hi = w * hi if acc_hi is None else acc_hi + w * hi
    o_ref[:, :D_PACK] = xm_ref[:, :D_PACK] + g2_ref[:, :D_PACK] * acc_lo
    o_ref[:, D_PACK:] = xm_ref[:, D_PACK:] + g2_ref[:, D_PACK:] * acc_hi


def _combine(ys, dest_flat, xm, wts, gate2):
    t = xm.shape[0]
    grid_spec = pltpu.PrefetchScalarGridSpec(
        num_scalar_prefetch=1,
        grid=(t // COMB_TC,),
        in_specs=[
            pl.BlockSpec((COMB_TC, D_MODEL), lambda i, d: (i, 0)),
            pl.BlockSpec((COMB_TC, LANES), lambda i, d: (i, 0)),
            pl.BlockSpec((1, D_MODEL), lambda i, d: (0, 0)),
            pl.BlockSpec(memory_space=pl.ANY),
        ],
        out_specs=pl.BlockSpec((COMB_TC, D_MODEL), lambda i, d: (i, 0)),
        scratch_shapes=[pltpu.VMEM((2, TOP_K * COMB_TC, D_PACK), jnp.uint32), pltpu.SemaphoreType.DMA((2,))],
    )
    return pl.pallas_call(
        _combine_kernel,
        grid_spec=grid_spec,
        out_shape=jax.ShapeDtypeStruct((t, D_MODEL), F32),
        compiler_params=_params(("arbitrary",)),
        name="combine",
    )(dest_flat, xm, wts, gate2, ys)


def _routing_tables(idx, rank, counts, n_tok):
    padded = ((counts + EXPERT_TM - 1) // EXPERT_TM) * EXPERT_TM
    pend = jnp.cumsum(padded)
    pstart = pend - padded
    experts = jnp.arange(N_EXPERTS, dtype=jnp.int32)
    start_of = jnp.sum(jnp.where(idx[..., None] == experts, pstart, 0), axis=-1)
    dest = (start_of + rank).astype(jnp.int32).reshape(-1)
    n_rows = n_tok * TOP_K + N_EXPERTS * EXPERT_TM
    nb = n_rows // EXPERT_TM
    block_start = jnp.arange(nb, dtype=jnp.int32) * EXPERT_TM
    block_expert = jnp.minimum(jnp.sum(pend[None, :] <= block_start[:, None], axis=1),
                               N_EXPERTS - 1).astype(jnp.int32)
    n_used = (pend[-1:] // EXPERT_TM).astype(jnp.int32)
    pad_lo = (pstart + counts).astype(jnp.int32)
    pad_hi = pend.astype(jnp.int32)

    blocks = jnp.arange(nb, dtype=jnp.int32)
    block_expert = jnp.where(blocks < n_used[0], block_expert, block_expert[jnp.maximum(n_used[0] - 1, 0)])
    prev = jnp.concatenate([block_expert[:1], block_expert[:-1]])
    first = jnp.logical_or(blocks == 0, block_expert != prev)
    first_pos = jnp.where(first, blocks, nb)
    next_first = jnp.flip(lax.cummin(jnp.flip(jnp.concatenate([first_pos[1:], jnp.full((1,), nb, jnp.int32)]))))
    nxt = jnp.where(next_first < nb, block_expert[jnp.minimum(next_first, nb - 1)], -1).astype(jnp.int32)
    rows_end = jnp.sum(jnp.where(block_expert[:, None] == experts, pad_lo, 0), axis=-1)
    full = (rows_end > block_start + EXPERT_TM // 2).astype(jnp.int32)
    tables = (block_expert, n_used, first.astype(jnp.int32), nxt, full)
    return dest, tables, pad_lo, pad_hi, n_rows


def _layer(x2, c, w_ada, b_ada, g_norm1, w_in, b_in, g_q, g_k, sinks, rel_bias, w_dw, b_dw, ln_g, ln_b,
           g_out_attn, g_out_conv, w_out, b_out, g_norm2, w_router, b_router,
           w_gate, b_gate, w_up, b_up, w_down, b_down):
    t = x2.shape[0]
    row = lambda v: v.reshape(1, -1)
    mod = _ada(c, w_ada, b_ada)
    shift1, scale1, gate1, shift2, scale2, gate2 = [mod[:, i * D_MODEL:(i + 1) * D_MODEL] for i in range(6)]

    q, kv, uc = _in_proj(x2, row(g_norm1), scale1, shift1, w_in.astype(BF16), row(b_in))

    gq_t = row(jnp.tile(g_q, N_Q_HEADS)) * (HEAD_DIM ** -0.5)
    gk_t = row(jnp.tile(g_k, N_KV_HEADS))
    ma = _attention(q, kv, _bias_table(rel_bias), sinks, gq_t, gk_t, row(g_out_attn))
    w_dw_p = jnp.concatenate([w_dw, jnp.zeros((HALO - CONV_WIDTH, D_CONV), w_dw.dtype)], axis=0)
    mc = _conv(uc, w_dw_p, row(b_dw), row(ln_g), row(ln_b), row(g_out_conv))

    xm, h2, idx, rank, wts, counts = _out_proj(
        ma, mc, x2, w_out.astype(BF16), row(b_out), gate1, row(g_norm2), scale2, shift2,
        w_router, row(b_router))

    dest, tables, pad_lo, pad_hi, n_rows = _routing_tables(idx[:, :TOP_K], rank[:, :TOP_K], counts[0], t)
    xs = _dispatch(h2, dest, pad_lo, pad_hi, tables[1], n_rows)
    hs = _up(xs, tables, w_gate, b_gate, w_up, b_up)
    ys = _down(hs, tables, w_down, b_down)
    return _combine(ys, dest, xm, wts, gate2)


def kernel(x, c, w_ada, b_ada, g_norm1, w_in, b_in, g_q, g_k, sinks, rel_bias, w_dw, b_dw, ln_g, ln_b,
           g_out_attn, g_out_conv, w_out, b_out, g_norm2, w_router, b_router,
           w_gate, b_gate, w_up, b_up, w_down, b_down):
    b, t, d = x.shape
    assert b == 1 and d == D_MODEL and w_ada.shape[0] == 1
    out = _layer(x.reshape(t, d), c, w_ada[0], b_ada[0], g_norm1[0], w_in[0], b_in[0], g_q[0], g_k[0],
                 sinks[0], rel_bias, w_dw[0], b_dw[0], ln_g[0], ln_b[0], g_out_attn[0], g_out_conv[0],
                 w_out[0], b_out[0], g_norm2[0], w_router[0], b_router[0],
                 w_gate[0], b_gate[0], w_up[0], b_up[0], w_down[0], b_down[0])
    return out.reshape(b, t, d)
```

```python
import functools
import math

import jax
import jax.numpy as jnp
from jax import lax
from jax.experimental import pallas as pl
from jax.experimental.pallas import tpu as pltpu

D_MODEL = 2048
HEAD_DIM = 64
N_Q_HEADS = 16
N_KV_HEADS = 2
D_ATTN = N_Q_HEADS * HEAD_DIM
D_KV = N_KV_HEADS * HEAD_DIM
D_CONV = D_MODEL - D_ATTN
D_IN = D_ATTN + 2 * D_KV + 2 * D_CONV
WINDOW = 128
BLOCK = 128
CONV_WIDTH = 31
N_BUCKETS = 32
MAX_DISTANCE = 128
N_EXPERTS = 32
TOP_K = 4
D_FF = D_MODEL
SWIGLU_LIMIT = 7.0
SWIGLU_ALPHA = 1.702
EPS = 1e-6
NEG_INF = -1e30

LANES = 128
SUBLANES = 8
VMEM_LIMIT = 56 * 1024 * 1024

HALO = 32
EXPERT_TM = 512
EXPERT_VMEM_LIMIT = 60 * 1024 * 1024
F32 = jnp.float32
BF16 = jnp.bfloat16


def _params(sem, vmem_limit=VMEM_LIMIT):
    return pltpu.CompilerParams(dimension_semantics=sem, vmem_limit_bytes=vmem_limit)


ADA_TN = 1024
ADA_RC = 256


def _ada_kernel(c_ref, w_ref, b_ref, o_ref):
    tn = w_ref.shape[1]

    def body(i, acc):
        r = pl.multiple_of(i * ADA_RC, ADA_RC)
        c = c_ref[pl.ds(r, ADA_RC), :]
        sc = c * jax.nn.sigmoid(c)
        prod = w_ref[pl.ds(r, ADA_RC), :] * sc
        return acc + jnp.sum(prod.reshape(ADA_RC // 8, 8, tn), axis=0)

    acc = lax.fori_loop(0, D_MODEL // ADA_RC, body, jnp.zeros((8, tn), F32))
    o_ref[...] = jnp.sum(acc, axis=0, keepdims=True) + b_ref[...]


def _ada(c, w_ada, b_ada):
    n = w_ada.shape[1]
    return pl.pallas_call(
        _ada_kernel,
        grid=(n // ADA_TN,),
        in_specs=[
            pl.BlockSpec((D_MODEL, 1), lambda j: (0, 0)),
            pl.BlockSpec((D_MODEL, ADA_TN), lambda j: (0, j)),
            pl.BlockSpec((1, ADA_TN), lambda j: (0, j)),
        ],
        out_specs=pl.BlockSpec((1, ADA_TN), lambda j: (0, j)),
        out_shape=jax.ShapeDtypeStruct((1, n), F32),
        compiler_params=_params(("parallel",)),
        name="ada",
    )(c.reshape(D_MODEL, 1), w_ada, b_ada.reshape(1, n))


IN_TM = 512
IN_NC = 256


def _modulated_rms(x, g, scale, shift):
    ms = jnp.mean(x * x, axis=-1, keepdims=True)
    return (x * lax.rsqrt(ms + EPS) * g) * (1.0 + scale) + shift


def _in_kernel(x_ref, g_ref, sc_ref, sh_ref, w_ref, b_ref, q_ref, kv_ref, uc_ref):
    h = _modulated_rms(x_ref[...], g_ref[...], sc_ref[...], sh_ref[...]).astype(BF16)

    def proj(lo, n, o_ref):
        for c in range(0, n, IN_NC):
            u = jnp.dot(h, w_ref[:, lo + c:lo + c + IN_NC], preferred_element_type=F32)
            o_ref[:, c:c + IN_NC] = u + b_ref[:, lo + c:lo + c + IN_NC]

    proj(0, D_ATTN, q_ref)
    proj(D_ATTN, 2 * D_KV, kv_ref)
    proj(D_ATTN + 2 * D_KV, 2 * D_CONV, uc_ref)


def _in_proj(x2, g1, scale1, shift1, w_in_bf, b_in):
    t = x2.shape[0]
    vec = lambda n: pl.BlockSpec((1, n), lambda i: (0, 0))
    return pl.pallas_call(
        _in_kernel,
        grid=(t // IN_TM,),
        in_specs=[
            pl.BlockSpec((IN_TM, D_MODEL), lambda i: (i, 0)),
            vec(D_MODEL), vec(D_MODEL), vec(D_MODEL),
            pl.BlockSpec((D_MODEL, D_IN), lambda i: (0, 0)),
            vec(D_IN),
        ],
        out_specs=[
            pl.BlockSpec((IN_TM, D_ATTN), lambda i: (i, 0)),
            pl.BlockSpec((IN_TM, 2 * D_KV), lambda i: (i, 0)),
            pl.BlockSpec((IN_TM, 2 * D_CONV), lambda i: (i, 0)),
        ],
        out_shape=[
            jax.ShapeDtypeStruct((t, D_ATTN), F32),
            jax.ShapeDtypeStruct((t, 2 * D_KV), F32),
            jax.ShapeDtypeStruct((t, 2 * D_CONV), F32),
        ],
        compiler_params=_params(("parallel",)),
        name="in_proj",
    )(x2, g1, scale1, shift1, w_in_bf, b_in)


ATT_R = 4


def _split_dot(a, b_bf):
    hi = a.astype(BF16)
    lo = (a - hi.astype(F32)).astype(BF16)
    return (jnp.dot(hi, b_bf, preferred_element_type=F32)
            + jnp.dot(lo, b_bf, preferred_element_type=F32))


def _attn_kernel(sinks_ref, q_ref, kvp_ref, kvc_ref, bias_ref, gq_ref, gk_ref, go_ref,
                 hq_ref, hqt_ref, hk_ref, o_ref, y_ref):
    step = pl.program_id(0)
    lane = lax.broadcasted_iota(jnp.int32, (2 * BLOCK, LANES), 1)
    low = lane < HEAD_DIM
    col = lax.broadcasted_iota(jnp.int32, (BLOCK, 2 * BLOCK), 1)
    first_lo = jnp.where(step == 0, BLOCK, 0)

    for r in range(ATT_R):
        rows = slice(r * BLOCK, (r + 1) * BLOCK)
        q = q_ref[rows, :]
        ssq = _split_dot(_split_dot(q * q, hq_ref[...]), hqt_ref[...])
        qn = (q * lax.rsqrt(ssq * (1.0 / HEAD_DIM) + EPS) * gq_ref[...]).astype(BF16)

        if r == 0:
            kv_prev = kvp_ref[...]
        else:
            kv_prev = kvc_ref[(r - 1) * BLOCK:r * BLOCK, :]
        kv = jnp.concatenate([kv_prev, kvc_ref[rows, :]], axis=0)
        k = kv[:, :LANES]
        v = kv[:, LANES:]
        kss = _split_dot(k * k, hk_ref[...])
        kn = k * lax.rsqrt(kss * (1.0 / HEAD_DIM) + EPS) * gk_ref[...]
        kn_sw = pltpu.roll(kn, HEAD_DIM, axis=1)
        v_sw = pltpu.roll(v, HEAD_DIM, axis=1)
        zero = jnp.zeros_like(kn)
        k_lo = [jnp.where(low, kn, zero).astype(BF16), jnp.where(low, kn_sw, zero).astype(BF16)]
        k_hi = [jnp.where(low, zero, kn_sw).astype(BF16), jnp.where(low, zero, kn).astype(BF16)]
        v_lo = [jnp.where(low, v, zero).astype(BF16), jnp.where(low, v_sw, zero).astype(BF16)]
        v_hi = [jnp.where(low, zero, v_sw).astype(BF16), jnp.where(low, zero, v).astype(BF16)]

        for p in range(N_Q_HEADS // 2):
            g = (2 * p) // (N_Q_HEADS // N_KV_HEADS)
            qp = qn[:, p * LANES:(p + 1) * LANES]
            acc = None
            for half, (kz, vz) in enumerate(((k_lo[g], v_lo[g]), (k_hi[g], v_hi[g]))):
                h = 2 * p + half
                s = lax.dot_general(qp, kz, (((1,), (1,)), ((), ())), preferred_element_type=F32)
                s = s + bias_ref[h]
                if r == 0:
                    s = jnp.where(col >= first_lo, s, NEG_INF)
                sink = sinks_ref[h]
                m = jnp.maximum(jnp.max(s, axis=-1, keepdims=True), sink)
                e = jnp.exp(s - m)
                denom = jnp.sum(e, axis=-1, keepdims=True) + jnp.exp(sink - m)
                pv = jnp.dot(e.astype(BF16), vz, preferred_element_type=F32)
                pv = pv * (1.0 / denom)
                acc = pv if acc is None else acc + pv
            y_ref[:, p * LANES:(p + 1) * LANES] = acc

        y = y_ref[...]
        ms = jnp.mean(y * y, axis=-1, keepdims=True)
        o_ref[rows, :] = (y * lax.rsqrt(ms + EPS) * go_ref[...]).astype(BF16)


def _attention(q, kv, bias, sinks, gq_t, gk_t, g_out):
    t = q.shape[0]
    tile = ATT_R * BLOCK
    head_of_lane = jnp.arange(D_ATTN) // HEAD_DIM
    hq = (head_of_lane[:, None] == jnp.arange(LANES)[None, :]).astype(BF16)
    hk = _head_indicator(LANES)
    grid_spec = pltpu.PrefetchScalarGridSpec(
        num_scalar_prefetch=0,
        grid=(t // tile,),
        in_specs=[
            pl.BlockSpec(memory_space=pltpu.SMEM),
            pl.BlockSpec((tile, D_ATTN), lambda i: (i, 0)),
            pl.BlockSpec((BLOCK, 2 * D_KV), lambda i: (jnp.maximum(i * ATT_R - 1, 0), 0)),
            pl.BlockSpec((tile, 2 * D_KV), lambda i: (i, 0)),
            pl.BlockSpec((N_Q_HEADS, BLOCK, 2 * BLOCK), lambda i: (0, 0, 0)),
            pl.BlockSpec((1, D_ATTN), lambda i: (0, 0)),
            pl.BlockSpec((1, LANES), lambda i: (0, 0)),
            pl.BlockSpec((1, D_ATTN), lambda i: (0, 0)),
            pl.BlockSpec((D_ATTN, LANES), lambda i: (0, 0)),
            pl.BlockSpec((LANES, D_ATTN), lambda i: (0, 0)),
            pl.BlockSpec((LANES, LANES), lambda i: (0, 0)),
        ],
        out_specs=pl.BlockSpec((tile, D_ATTN), lambda i: (i, 0)),
        scratch_shapes=[pltpu.VMEM((BLOCK, D_ATTN), F32)],
    )
    return pl.pallas_call(
        _attn_kernel,
        grid_spec=grid_spec,
        out_shape=jax.ShapeDtypeStruct((t, D_ATTN), BF16),
        compiler_params=_params(("parallel",)),
        name="attn",
    )(sinks, q, kv, kv, bias, gq_t, gk_t, g_out, hq, hq.T, hk)


def _t5_bucket(dist):
    max_exact = N_BUCKETS // 2
    d = jnp.maximum(dist, 0)
    log_ratio = jnp.log(jnp.maximum(d, max_exact).astype(F32) / max_exact)
    large = max_exact + (log_ratio / math.log(MAX_DISTANCE / max_exact)
                         * (N_BUCKETS - max_exact)).astype(jnp.int32)
    large = jnp.minimum(large, N_BUCKETS - 1)
    return jnp.where(d < max_exact, d, large)


def _bias_table(rel_bias):
    q_local = jnp.arange(BLOCK, dtype=jnp.int32) + BLOCK
    k_local = jnp.arange(2 * BLOCK, dtype=jnp.int32)
    dist = q_local[:, None] - k_local[None, :]
    band = (dist >= 0) & (dist < WINDOW)
    bucket = _t5_bucket(dist)
    hit = (jnp.arange(N_BUCKETS, dtype=jnp.int32)[:, None] == bucket.reshape(1, -1)).astype(F32)
    bias = jnp.dot(rel_bias.astype(F32).T, hit, precision=lax.Precision.HIGHEST)
    return jnp.where(band[None], bias.reshape(N_Q_HEADS, BLOCK, 2 * BLOCK), NEG_INF)


def _head_indicator(n):
    i = jnp.arange(n) // HEAD_DIM
    return (i[:, None] == i[None, :]).astype(BF16)


CONV_TT = 512
CONV_RC = 64
CONV_CC = 256


def _conv_kernel(u_ref, halo_ref, w_ref, b_ref, lg_ref, lb_ref, go_ref, o_ref, h_ref, y_ref):
    step = pl.program_id(0)

    def glu(u):
        return u[:, :D_CONV] * jax.nn.sigmoid(u[:, D_CONV:])

    hh = glu(halo_ref[...])
    h_ref[0, 0:HALO, :] = jnp.where(step == 0, jnp.zeros_like(hh), hh)
    h_ref[0, HALO:, :] = glu(u_ref[...])

    n_rows = CONV_TT + HALO
    for s in range(1, SUBLANES):
        for c0 in range(0, D_CONV, CONV_CC):
            cs = slice(c0, c0 + CONV_CC)
            h_ref[s, :, cs] = pltpu.roll(h_ref[0, :, cs], n_rows - s, axis=0)

    off = HALO - (CONV_WIDTH - 1)
    for r0 in range(0, CONV_TT, CONV_RC):
        for c0 in range(0, D_CONV, CONV_CC):
            cs = slice(c0, c0 + CONV_CC)
            acc = jnp.broadcast_to(b_ref[:, cs], (CONV_RC, CONV_CC))
            for j in range(CONV_WIDTH):
                s = (r0 + off + j) % SUBLANES
                a = r0 + off + j - s
                acc = acc + w_ref[j:j + 1, cs] * h_ref[s, a:a + CONV_RC, cs]
            y_ref[r0:r0 + CONV_RC, cs] = acc

    y = y_ref[...]
    mu = jnp.mean(y, axis=-1, keepdims=True)
    yc = y - mu
    var = jnp.mean(yc * yc, axis=-1, keepdims=True)
    z = yc * lax.rsqrt(var + EPS) * lg_ref[...] + lb_ref[...]
    s = z * jax.nn.sigmoid(z)
    ms = jnp.mean(s * s, axis=-1, keepdims=True)
    o_ref[...] = (s * lax.rsqrt(ms + EPS) * go_ref[...]).astype(BF16)


def _conv(uc, w_dw, b_dw, ln_g, ln_b, g_out):
    t = uc.shape[0]
    vec = lambda: pl.BlockSpec((1, D_CONV), lambda i: (0, 0))
    per = CONV_TT // HALO
    return pl.pallas_call(
        _conv_kernel,
        grid=(t // CONV_TT,),
        in_specs=[
            pl.BlockSpec((CONV_TT, 2 * D_CONV), lambda i: (i, 0)),
            pl.BlockSpec((HALO, 2 * D_CONV), lambda i: (jnp.maximum(i * per - 1, 0), 0)),
            pl.BlockSpec((HALO, D_CONV), lambda i: (0, 0)),
            vec(), vec(), vec(), vec(),
        ],
        out_specs=pl.BlockSpec((CONV_TT, D_CONV), lambda i: (i, 0)),
        out_shape=jax.ShapeDtypeStruct((t, D_CONV), BF16),
        scratch_shapes=[pltpu.VMEM((SUBLANES, CONV_TT + HALO, D_CONV), F32),
                        pltpu.VMEM((CONV_TT, D_CONV), F32)],
        compiler_params=_params(("parallel",)),
        name="conv",
    )(uc, uc, w_dw, b_dw, ln_g, ln_b, g_out)


OUT_TM = 512
D_PACK = D_MODEL // 2
HI_MASK = 0xFFFF0000


def _pack_bf16_pairs(h):
    lo = lax.bitcast_convert_type(h[:, :D_PACK].astype(BF16).astype(F32), jnp.uint32)
    hi = lax.bitcast_convert_type(h[:, D_PACK:].astype(BF16).astype(F32), jnp.uint32)
    return (lo >> 16) | (hi & jnp.uint32(HI_MASK))


def _unpack_pairs_f32(w):
    lo = lax.bitcast_convert_type(w << 16, F32)
    hi = lax.bitcast_convert_type(w & jnp.uint32(HI_MASK), F32)
    return lo, hi


def _unpack_bf16_pairs(w):
    lo, hi = _unpack_pairs_f32(w)
    return lo.astype(BF16), hi.astype(BF16)


def _out_kernel(ma_ref, mc_ref, x_ref, w_ref, bo_ref, g1_ref, g2_ref, sc_ref, sh_ref, wr_ref, br_ref,
                tri_ref, xm_ref, h2_ref, idx_ref, rank_ref, wt_ref, cnt_ref, carry_ref):
    step = pl.program_id(0)

    @pl.when(step == 0)
    def _():
        carry_ref[...] = jnp.zeros_like(carry_ref)

    y = (jnp.dot(ma_ref[...], w_ref[0:D_ATTN, :], preferred_element_type=F32)
         + jnp.dot(mc_ref[...], w_ref[D_ATTN:, :], preferred_element_type=F32) + bo_ref[...])
    xm = x_ref[...] + g1_ref[...] * y
    xm_ref[...] = xm
    h2 = _modulated_rms(xm, g2_ref[...], sc_ref[...], sh_ref[...])
    h2_ref[...] = _pack_bf16_pairs(h2)

    both = jnp.dot(h2.astype(BF16), wr_ref[...], preferred_element_type=F32)
    logits = both[:, :N_EXPERTS] + both[:, N_EXPERTS:] + br_ref[...]
    tm = logits.shape[0]
    lane = lax.broadcasted_iota(jnp.int32, (tm, N_EXPERTS), 1).astype(F32)
    vals, idxs = [], []
    l = logits
    for _ in range(TOP_K):
        m = jnp.max(l, axis=-1, keepdims=True)
        i = jnp.min(jnp.where(l == m, lane, float(N_EXPERTS)), axis=-1, keepdims=True)
        vals.append(m)
        idxs.append(i)
        l = jnp.where(lane == i, -jnp.inf, l)
    es = [jnp.exp(v - vals[0]) for v in vals]
    tot = es[0] + es[1] + es[2] + es[3]
    ws = [e / tot for e in es]

    hot = [(lane == i).astype(F32) for i in idxs]
    hot_all = hot[0] + hot[1] + hot[2] + hot[3]
    before = jnp.dot(tri_ref[...], hot_all.astype(BF16), preferred_element_type=F32) + carry_ref[...]
    ranks = [jnp.sum(h * before, axis=-1, keepdims=True) for h in hot]
    carry_ref[...] = carry_ref[...] + jnp.sum(hot_all, axis=0, keepdims=True)
    cnt_ref[...] = carry_ref[...].astype(jnp.int32)

    slot = lax.broadcasted_iota(jnp.int32, (tm, LANES), 1)

    def pack(cols):
        out = jnp.zeros((tm, LANES), F32)
        for k in range(TOP_K):
            out = jnp.where(slot == k, cols[k], out)
        return out

    idx_ref[...] = pack(idxs).astype(jnp.int32)
    rank_ref[...] = pack(ranks).astype(jnp.int32)
    wt_ref[...] = pack(ws)


def _out_proj(ma, mc, x2, w_out_bf, b_out, gate1, g2, scale2, shift2, w_router, b_router):
    t = x2.shape[0]
    vec = lambda n: pl.BlockSpec((1, n), lambda i: (0, 0))
    tri = jnp.tril(jnp.ones((OUT_TM, OUT_TM), F32), -1).astype(BF16)
    wr_hi = w_router.astype(BF16)
    wr_lo = (w_router - wr_hi.astype(F32)).astype(BF16)
    w_router = jnp.concatenate([wr_hi, wr_lo], axis=1)
    return pl.pallas_call(
        _out_kernel,
        grid=(t // OUT_TM,),
        in_specs=[
            pl.BlockSpec((OUT_TM, D_ATTN), lambda i: (i, 0)),
            pl.BlockSpec((OUT_TM, D_CONV), lambda i: (i, 0)),
            pl.BlockSpec((OUT_TM, D_MODEL), lambda i: (i, 0)),
            pl.BlockSpec((D_MODEL, D_MODEL), lambda i: (0, 0)),
            vec(D_MODEL), vec(D_MODEL), vec(D_MODEL), vec(D_MODEL), vec(D_MODEL),
            pl.BlockSpec((D_MODEL, 2 * N_EXPERTS), lambda i: (0, 0)),
            vec(N_EXPERTS),
            pl.BlockSpec((OUT_TM, OUT_TM), lambda i: (0, 0)),
        ],
        out_specs=[
            pl.BlockSpec((OUT_TM, D_MODEL), lambda i: (i, 0)),
            pl.BlockSpec((OUT_TM, D_PACK), lambda i: (i, 0)),
            pl.BlockSpec((OUT_TM, LANES), lambda i: (i, 0)),
            pl.BlockSpec((OUT_TM, LANES), lambda i: (i, 0)),
            pl.BlockSpec((OUT_TM, LANES), lambda i: (i, 0)),
            pl.BlockSpec((1, N_EXPERTS), lambda i: (0, 0)),
        ],
        out_shape=[
            jax.ShapeDtypeStruct((t, D_MODEL), F32),
            jax.ShapeDtypeStruct((t, D_PACK), jnp.uint32),
            jax.ShapeDtypeStruct((t, LANES), jnp.int32),
            jax.ShapeDtypeStruct((t, LANES), jnp.int32),
            jax.ShapeDtypeStruct((t, LANES), F32),
            jax.ShapeDtypeStruct((1, N_EXPERTS), jnp.int32),
        ],
        scratch_shapes=[pltpu.VMEM((1, N_EXPERTS), F32)],
        compiler_params=_params(("arbitrary",)),
        name="out_proj",
    )(ma, mc, x2, w_out_bf, b_out, gate1, g2, scale2, shift2, w_router, b_router, tri)


DISP_TD = 512
DISP_NBUF = 3


class _CopyGroup:
    def __init__(self, copies):
        self.copies = copies

    def start(self):
        for cp in self.copies:
            cp.start()

    def wait(self):
        for cp in self.copies:
            cp.wait()

ZERO_RUN = 64


def _dispatch_kernel(dest_ref, pad_lo_ref, pad_hi_ref, nu_ref, h2_ref, xs_ref,
                     stage_ref, zero_ref, in_sems, out_sems, zsem):
    step = pl.program_id(0)
    last = pl.num_programs(0) - 1
    slot = lax.rem(step, DISP_NBUF)
    rows_per_chunk = DISP_TD * TOP_K

    def load(chunk, s):
        rows = pl.ds(pl.multiple_of(chunk * DISP_TD, DISP_TD), DISP_TD)
        return _CopyGroup([pltpu.make_async_copy(h2_ref.at[rows, pl.ds(j * LANES, LANES)], stage_ref.at[s, :, j, :],
                                                 in_sems.at[s]) for j in range(stage_ref.shape[2])])

    def wait_chunk(s):
        whole = xs_ref.at[pl.ds(0, rows_per_chunk)]
        pltpu.make_async_copy(whole, whole, out_sems.at[s]).wait()

    @pl.when(step == 0)
    def _():
        load(0, 0).start()

        @pl.when(last >= 1)
        def _():
            load(1, 1).start()

    load(step, slot).wait()

    def issue(r, carry):
        for k in range(TOP_K):
            d = dest_ref[(step * DISP_TD + r) * TOP_K + k]
            pltpu.make_async_copy(stage_ref.at[slot, r], xs_ref.at[d], out_sems.at[slot]).start(priority=k % 2)
        return carry

    lax.fori_loop(0, DISP_TD, issue, 0, unroll=2)

    @pl.when(step == 0)
    def _():
        zero_ref[...] = jnp.zeros_like(zero_ref)

        def zero_rows(d, n):
            d = d if n == 1 else pl.multiple_of(d, n)
            return pltpu.make_async_copy(zero_ref.at[pl.ds(0, n)], xs_ref.at[pl.ds(d, n)], zsem)

        def ranges(e):
            lo, hi = pad_lo_ref[e], pad_hi_ref[e]
            a = jnp.minimum((lo + SUBLANES - 1) // SUBLANES * SUBLANES, hi)
            b = jnp.minimum((lo + ZERO_RUN - 1) // ZERO_RUN * ZERO_RUN, hi)
            return ((lo, a - lo, 1), (a, (b - a) // SUBLANES, SUBLANES), (b, (hi - b) // ZERO_RUN, ZERO_RUN))

        def for_all_runs(act):
            def per_expert(e, carry):
                for start, count, n in ranges(e):
                    lax.fori_loop(0, count, lambda i, c, s=start, n=n: (act(zero_rows(s + i * n, n)), c)[1], 0)
                return carry

            lax.fori_loop(0, N_EXPERTS, per_expert, 0)
            n_blocks = xs_ref.shape[0] // EXPERT_TM
            lax.fori_loop(nu_ref[0], n_blocks,
                          lambda blk, c: (act(zero_rows(blk * EXPERT_TM, EXPERT_TM)), c)[1], 0)

        for_all_runs(lambda copy: copy.start())
        for_all_runs(lambda copy: copy.wait())

    @pl.when(step > 0)
    def _():
        wait_chunk(lax.rem(step + DISP_NBUF - 1, DISP_NBUF))

    @pl.when(step + 2 <= last)
    def _():
        load(step + 2, lax.rem(step + 2, DISP_NBUF)).start()

    @pl.when(step == last)
    def _():
        wait_chunk(slot)


def _dispatch(h2p, dest_flat, pad_lo, pad_hi, n_used, n_rows):
    t = h2p.shape[0]
    grid_spec = pltpu.PrefetchScalarGridSpec(
        num_scalar_prefetch=4,
        grid=(t // DISP_TD,),
        in_specs=[pl.BlockSpec(memory_space=pl.ANY)],
        out_specs=pl.BlockSpec(memory_space=pl.ANY),
        scratch_shapes=[pltpu.VMEM((DISP_NBUF, DISP_TD, D_PACK // LANES, LANES), jnp.uint32),
                        pltpu.VMEM((EXPERT_TM, D_PACK // LANES, LANES), jnp.uint32),
                        pltpu.SemaphoreType.DMA((DISP_NBUF,)), pltpu.SemaphoreType.DMA((DISP_NBUF,)),
                        pltpu.SemaphoreType.DMA(())],
    )
    return pl.pallas_call(
        _dispatch_kernel,
        grid_spec=grid_spec,
        out_shape=jax.ShapeDtypeStruct((n_rows, D_PACK // LANES, LANES), jnp.uint32),
        compiler_params=_params(("arbitrary",)),
        name="dispatch",
    )(dest_flat, pad_lo, pad_hi, n_used, h2p)


UP_TF = 1024
DOWN_TN = 2048
UP_AHEAD = 2


def _weight_index_map(n_chunks):
    def index_map(c, b, be, nu, first, nxt, full):
        del nu, full
        in_last = nxt[b] < 0
        wrap = jnp.logical_and(in_last, c + 1 < n_chunks)
        e_next = jnp.where(in_last, jnp.where(wrap, be[0], be[b]), nxt[b])
        c_next = jnp.where(wrap, c + 1, c)
        is_first = first[b] == 1
        return jnp.where(is_first, be[b], e_next), 0, jnp.where(is_first, c, c_next)

    return index_map


def _row_cases(b, nu_ref, full_ref, o_ref, compute):
    half = EXPERT_TM // 2
    used = b < nu_ref[0]

    @pl.when(jnp.logical_and(used, full_ref[b] == 1))
    def _():
        compute(EXPERT_TM)

    @pl.when(jnp.logical_and(used, full_ref[b] == 0))
    def _():
        compute(half)
        o_ref[half:, :] = jnp.zeros((half, o_ref.shape[1]), o_ref.dtype)

    @pl.when(jnp.logical_not(used))
    def _():
        o_ref[...] = jnp.zeros(o_ref.shape, o_ref.dtype)


def _up_kernel(be_ref, nu_ref, first_ref, nxt_ref, full_ref, xs_ref, wg_ref, wu_ref, bg_ref, bu_ref, h_ref,
               wg_bf, wu_bf, xbuf_ref, xsems):
    f = pl.program_id(0)
    b = pl.program_id(1)
    nb = pl.num_programs(1)
    step = f * nb + b
    n_steps = pl.num_programs(0) * nb
    n_buf = UP_AHEAD + 1
    slot = lax.rem(step, n_buf)
    n_sub = xs_ref.shape[1]

    def fetch(ahead):
        blk = lax.rem(b + ahead, nb)
        rows = pl.ds(pl.multiple_of(jnp.minimum(blk, nu_ref[0] - 1) * EXPERT_TM, EXPERT_TM), EXPERT_TM)
        s = lax.rem(step + ahead, n_buf)
        return _CopyGroup([pltpu.make_async_copy(xs_ref.at[rows, j, :], xbuf_ref.at[s, :, pl.ds(j * LANES, LANES)],
                                                 xsems.at[s]) for j in range(n_sub)])

    @pl.when(step == 0)
    def _():
        for ahead in range(UP_AHEAD):
            @pl.when(ahead < n_steps)
            def _():
                fetch(ahead).start()

    @pl.when(step + UP_AHEAD < n_steps)
    def _():
        fetch(UP_AHEAD).start()

    fetch(0).wait()
    x_ref = xbuf_ref.at[slot]

    @pl.when(first_ref[b] == 1)
    def _():
        wg_bf[...] = wg_ref[0].astype(BF16)
        wu_bf[...] = wu_ref[0].astype(BF16)

    def compute(m):
        x_lo, x_hi = _unpack_bf16_pairs(x_ref[0:m, :])

        def proj(w_bf, bias_ref):
            return (jnp.dot(x_lo, w_bf[0:D_PACK, :], preferred_element_type=F32)
                    + jnp.dot(x_hi, w_bf[D_PACK:, :], preferred_element_type=F32) + bias_ref[0])

        g = proj(wg_bf, bg_ref)
        lin = proj(wu_bf, bu_ref)
        g = jnp.minimum(g, SWIGLU_LIMIT)
        lin = jnp.clip(lin, -SWIGLU_LIMIT, SWIGLU_LIMIT)
        act = g * jax.nn.sigmoid(SWIGLU_ALPHA * g) * (lin + 1.0)
        h_ref[0:m, :] = act.astype(BF16)

    _row_cases(b, nu_ref, full_ref, h_ref, compute)


def _up(xs, tables, w_gate, b_gate, w_up, b_up):
    n_rows = xs.shape[0]
    nb = n_rows // EXPERT_TM
    n_chunks = D_FF // UP_TF
    wsel = _weight_index_map(n_chunks)
    bsel = lambda f, b, be, nu, first, nxt, full: (be[b], 0, f)
    grid_spec = pltpu.PrefetchScalarGridSpec(
        num_scalar_prefetch=5,
        grid=(n_chunks, nb),
        in_specs=[
            pl.BlockSpec(memory_space=pl.ANY),
            pl.BlockSpec((1, D_MODEL, UP_TF), wsel),
            pl.BlockSpec((1, D_MODEL, UP_TF), wsel),
            pl.BlockSpec((1, 1, UP_TF), bsel),
            pl.BlockSpec((1, 1, UP_TF), bsel),
        ],
        out_specs=pl.BlockSpec((EXPERT_TM, UP_TF), lambda f, b, be, nu, first, nxt, full: (b, f)),
        scratch_shapes=[pltpu.VMEM((D_MODEL, UP_TF), BF16), pltpu.VMEM((D_MODEL, UP_TF), BF16),
                        pltpu.VMEM((UP_AHEAD + 1, EXPERT_TM, D_PACK), jnp.uint32),
                        pltpu.SemaphoreType.DMA((UP_AHEAD + 1,))],
    )
    return pl.pallas_call(
        _up_kernel,
        grid_spec=grid_spec,
        out_shape=jax.ShapeDtypeStruct((n_rows, D_FF), BF16),
        compiler_params=_params(("arbitrary", "arbitrary"), EXPERT_VMEM_LIMIT),
        name="expert_up",
    )(*tables, xs, w_gate, w_up,
      b_gate.reshape(N_EXPERTS, 1, D_FF), b_up.reshape(N_EXPERTS, 1, D_FF))


def _down_kernel(be_ref, nu_ref, first_ref, nxt_ref, full_ref, h_ref, wd_ref, bd_ref, ys_ref, wd_bf,
                 ybuf_ref, ysems):
    b = pl.program_id(1)
    last = pl.num_programs(1) - 1
    slot = b % 2

    def write_back(blk, s):
        rows = pl.ds(pl.multiple_of(blk * EXPERT_TM, EXPERT_TM), EXPERT_TM)
        return _CopyGroup([pltpu.make_async_copy(ybuf_ref.at[s, :, pl.ds(j * LANES, LANES)], ys_ref.at[rows, j, :],
                                                 ysems.at[s]) for j in range(ys_ref.shape[1])])

    @pl.when(b >= 2)
    def _():
        write_back(b - 2, slot).wait()

    @pl.when(first_ref[b] == 1)
    def _():
        wd_bf[...] = wd_ref[0].astype(BF16)

    y_ref = ybuf_ref.at[slot]

    def compute(m):
        y = jnp.dot(h_ref[0:m, :], wd_bf[...], preferred_element_type=F32) + bd_ref[0]
        y_ref[0:m, :] = _pack_bf16_pairs(y)

    _row_cases(b, nu_ref, full_ref, y_ref, compute)
    write_back(b, slot).start()

    @pl.when(b == last)
    def _():
        @pl.when(b >= 1)
        def _():
            write_back(b - 1, 1 - slot).wait()

        write_back(b, slot).wait()


def _down(hs, tables, w_down, b_down):
    n_rows = hs.shape[0]
    nb = n_rows // EXPERT_TM
    n_chunks = D_MODEL // DOWN_TN
    assert n_chunks == 1
    row = lambda n, b, be, nu, first, nxt, full: (jnp.minimum(b, nu[0] - 1), 0)
    grid_spec = pltpu.PrefetchScalarGridSpec(
        num_scalar_prefetch=5,
        grid=(n_chunks, nb),
        in_specs=[
            pl.BlockSpec((EXPERT_TM, D_FF), row),
            pl.BlockSpec((1, D_FF, DOWN_TN), _weight_index_map(n_chunks)),
            pl.BlockSpec((1, 1, DOWN_TN), lambda n, b, be, nu, first, nxt, full: (be[b], 0, n)),
        ],
        out_specs=pl.BlockSpec(memory_space=pl.ANY),
        scratch_shapes=[pltpu.VMEM((D_FF, DOWN_TN), BF16),
                        pltpu.VMEM((2, EXPERT_TM, D_PACK), jnp.uint32), pltpu.SemaphoreType.DMA((2,))],
    )
    return pl.pallas_call(
        _down_kernel,
        grid_spec=grid_spec,
        out_shape=jax.ShapeDtypeStruct((n_rows, D_PACK // LANES, LANES), jnp.uint32),
        compiler_params=_params(("arbitrary", "arbitrary"), EXPERT_VMEM_LIMIT),
        name="expert_down",
    )(*tables, hs, w_down, b_down.reshape(N_EXPERTS, 1, D_MODEL))


COMB_TC = 256


def _combine_kernel(dest_ref, wsm_ref, xm_ref, g2_ref, ys_ref, o_ref, buf_ref, acc_ref, sems):
    step = pl.program_id(0)
    last = pl.num_programs(0) - 1
    slot = step % 2

    def gather_tile(tile, s):
        def issue(r, carry):
            for k in range(TOP_K):
                d = dest_ref[(tile * COMB_TC + r) * TOP_K + k]
                pltpu.make_async_copy(ys_ref.at[d], buf_ref.at[s, k * COMB_TC + r],
                                      sems.at[s]).start(priority=k % 2)
            return carry

        lax.fori_loop(0, COMB_TC, issue, 0, unroll=2)

    @pl.when(step == 0)
    def _():
        gather_tile(0, 0)

    @pl.when(step < last)
    def _():
        gather_tile(step + 1, 1 - slot)

    pltpu.make_async_copy(ys_ref.at[pl.ds(0, TOP_K * COMB_TC)], buf_ref.at[slot], sems.at[slot]).wait()

    def reduce_token(r, carry):
        acc_lo = acc_hi = None
        for k in range(TOP_K):
            w = wsm_ref[(step * COMB_TC + r) * TOP_K + k]
            lo, hi = _unpack_pairs_f32(buf_ref[slot, k * COMB_TC + r])
            acc_lo = w * lo if acc_lo is None else acc_lo + w * lo
            acc_hi = w * hi if acc_hi is None else acc_hi + w * hi
        acc_ref[0, r] = acc_lo
        acc_ref[1, r] = acc_hi
        return carry

    lax.fori_loop(0, COMB_TC, reduce_token, 0, unroll=4)

    for half in range(2):
        for j in range(D_PACK // LANES):
            cols = slice(half * D_PACK + j * LANES, half * D_PACK + (j + 1) * LANES)
            o_ref[:, cols] = xm_ref[:, cols] + g2_ref[:, cols] * acc_ref[half, :, j, :]


def _combine(ys, dest_flat, xm, wts_flat, gate2):
    t = xm.shape[0]
    n_sub = D_PACK // LANES
    grid_spec = pltpu.PrefetchScalarGridSpec(
        num_scalar_prefetch=2,
        grid=(t // COMB_TC,),
        in_specs=[
            pl.BlockSpec((COMB_TC, D_MODEL), lambda i, d, w: (i, 0)),
            pl.BlockSpec((1, D_MODEL), lambda i, d, w: (0, 0)),
            pl.BlockSpec(memory_space=pl.ANY),
        ],
        out_specs=pl.BlockSpec((COMB_TC, D_MODEL), lambda i, d, w: (i, 0)),
        scratch_shapes=[pltpu.VMEM((2, TOP_K * COMB_TC, n_sub, LANES), jnp.uint32),
                        pltpu.VMEM((2, COMB_TC, n_sub, LANES), F32),
                        pltpu.SemaphoreType.DMA((2,))],
    )
    return pl.pallas_call(
        _combine_kernel,
        grid_spec=grid_spec,
        out_shape=jax.ShapeDtypeStruct((t, D_MODEL), F32),
        compiler_params=_params(("arbitrary",)),
        name="combine",
    )(dest_flat, wts_flat, xm, gate2, ys)


def _routing_tables(idx, rank, counts, n_tok):
    padded = ((counts + EXPERT_TM - 1) // EXPERT_TM) * EXPERT_TM
    pend = jnp.cumsum(padded)
    pstart = pend - padded
    experts = jnp.arange(N_EXPERTS, dtype=jnp.int32)
    start_of = jnp.sum(jnp.where(idx[..., None] == experts, pstart, 0), axis=-1)
    dest = (start_of + rank).astype(jnp.int32).reshape(-1)
    n_rows = n_tok * TOP_K + N_EXPERTS * EXPERT_TM
    nb = n_rows // EXPERT_TM
    block_start = jnp.arange(nb, dtype=jnp.int32) * EXPERT_TM
    block_expert = jnp.minimum(jnp.sum(pend[None, :] <= block_start[:, None], axis=1),
                               N_EXPERTS - 1).astype(jnp.int32)
    n_used = (pend[-1:] // EXPERT_TM).astype(jnp.int32)
    pad_lo = (pstart + counts).astype(jnp.int32)
    pad_hi = pend.astype(jnp.int32)

    blocks = jnp.arange(nb, dtype=jnp.int32)
    block_expert = jnp.where(blocks < n_used[0], block_expert, block_expert[jnp.maximum(n_used[0] - 1, 0)])
    prev = jnp.concatenate([block_expert[:1], block_expert[:-1]])
    first = jnp.logical_or(blocks == 0, block_expert != prev)
    first_pos = jnp.where(first, blocks, nb)
    next_first = jnp.flip(lax.cummin(jnp.flip(jnp.concatenate([first_pos[1:], jnp.full((1,), nb, jnp.int32)]))))
    nxt = jnp.where(next_first < nb, block_expert[jnp.minimum(next_first, nb - 1)], -1).astype(jnp.int32)
    rows_end = jnp.sum(jnp.where(block_expert[:, None] == experts, pad_lo, 0), axis=-1)
    full = (rows_end > block_start + EXPERT_TM // 2).astype(jnp.int32)
    tables = (block_expert, n_used, first.astype(jnp.int32), nxt, full)
    return dest, tables, pad_lo, pad_hi, n_rows


def _layer(x2, c, w_ada, b_ada, g_norm1, w_in, b_in, g_q, g_k, sinks, rel_bias, w_dw, b_dw, ln_g, ln_b,
           g_out_attn, g_out_conv, w_out, b_out, g_norm2, w_router, b_router,
           w_gate, b_gate, w_up, b_up, w_down, b_down):
    t = x2.shape[0]
    row = lambda v: v.reshape(1, -1)
    mod = _ada(c, w_ada, b_ada)
    shift1, scale1, gate1, shift2, scale2, gate2 = [mod[:, i * D_MODEL:(i + 1) * D_MODEL] for i in range(6)]

    q, kv, uc = _in_proj(x2, row(g_norm1), scale1, shift1, w_in.astype(BF16), row(b_in))

    gq_t = row(jnp.tile(g_q, N_Q_HEADS)) * (HEAD_DIM ** -0.5)
    gk_t = row(jnp.tile(g_k, N_KV_HEADS))
    ma = _attention(q, kv, _bias_table(rel_bias), sinks, gq_t, gk_t, row(g_out_attn))
    w_dw_p = jnp.concatenate([w_dw, jnp.zeros((HALO - CONV_WIDTH, D_CONV), w_dw.dtype)], axis=0)
    mc = _conv(uc, w_dw_p, row(b_dw), row(ln_g), row(ln_b), row(g_out_conv))

    xm, h2, idx, rank, wts, counts = _out_proj(
        ma, mc, x2, w_out.astype(BF16), row(b_out), gate1, row(g_norm2), scale2, shift2,
        w_router, row(b_router))

    dest, tables, pad_lo, pad_hi, n_rows = _routing_tables(idx[:, :TOP_K], rank[:, :TOP_K], counts[0], t)
    xs = _dispatch(h2, dest, pad_lo, pad_hi, tables[1], n_rows)
    hs = _up(xs, tables, w_gate, b_gate, w_up, b_up)
    ys = _down(hs, tables, w_down, b_down)
    return _combine(ys, dest, xm, wts[:, :TOP_K].reshape(-1), gate2)


def kernel(x, c, w_ada, b_ada, g_norm1, w_in, b_in, g_q, g_k, sinks, rel_bias, w_dw, b_dw, ln_g, ln_b,
           g_out_attn, g_out_conv, w_out, b_out, g_norm2, w_router, b_router,
           w_gate, b_gate, w_up, b_up, w_down, b_down):
    b, t, d = x.shape
    assert b == 1 and d == D_MODEL and w_ada.shape[0] == 1
    out = _layer(x.reshape(t, d), c, w_ada[0], b_ada[0], g_norm1[0], w_in[0], b_in[0], g_q[0], g_k[0],
                 sinks[0], rel_bias, w_dw[0], b_dw[0], ln_g[0], ln_b[0], g_out_attn[0], g_out_conv[0],
                 w_out[0], b_out[0], g_norm2[0], w_router[0], b_router[0],
                 w_gate[0], b_gate[0], w_up[0], b_up[0], w_down[0], b_down[0])
    return out.reshape(b, t, d)
```

```python
import functools
import math

import jax
import jax.numpy as jnp
from jax import lax
from jax.experimental import pallas as pl
from jax.experimental.pallas import tpu as pltpu

D_MODEL = 2048
HEAD_DIM = 64
N_Q_HEADS = 16
N_KV_HEADS = 2
D_ATTN = N_Q_HEADS * HEAD_DIM
D_KV = N_KV_HEADS * HEAD_DIM
D_CONV = D_MODEL - D_ATTN
D_IN = D_ATTN + 2 * D_KV + 2 * D_CONV
WINDOW = 128
BLOCK = 128
CONV_WIDTH = 31
N_BUCKETS = 32
MAX_DISTANCE = 128
N_EXPERTS = 32
TOP_K = 4
D_FF = D_MODEL
SWIGLU_LIMIT = 7.0
SWIGLU_ALPHA = 1.702
EPS = 1e-6
NEG_INF = -1e30

LANES = 128
SUBLANES = 8
VMEM_LIMIT = 56 * 1024 * 1024

HALO = 32
EXPERT_TM = 512
EXPERT_VMEM_LIMIT = 60 * 1024 * 1024
F32 = jnp.float32
BF16 = jnp.bfloat16


def _params(sem, vmem_limit=VMEM_LIMIT):
    return pltpu.CompilerParams(dimension_semantics=sem, vmem_limit_bytes=vmem_limit)


ADA_TN = 1024
ADA_RC = 256


def _ada_kernel(c_ref, w_ref, b_ref, o_ref):
    tn = w_ref.shape[1]

    def body(i, acc):
        r = pl.multiple_of(i * ADA_RC, ADA_RC)
        c = c_ref[pl.ds(r, ADA_RC), :]
        sc = c * jax.nn.sigmoid(c)
        prod = w_ref[pl.ds(r, ADA_RC), :] * sc
        return acc + jnp.sum(prod.reshape(ADA_RC // 8, 8, tn), axis=0)

    acc = lax.fori_loop(0, D_MODEL // ADA_RC, body, jnp.zeros((8, tn), F32))
    o_ref[...] = jnp.sum(acc, axis=0, keepdims=True) + b_ref[...]


def _ada(c, w_ada, b_ada):
    n = w_ada.shape[1]
    return pl.pallas_call(
        _ada_kernel,
        grid=(n // ADA_TN,),
        in_specs=[
            pl.BlockSpec((D_MODEL, 1), lambda j: (0, 0)),
            pl.BlockSpec((D_MODEL, ADA_TN), lambda j: (0, j)),
            pl.BlockSpec((1, ADA_TN), lambda j: (0, j)),
        ],
        out_specs=pl.BlockSpec((1, ADA_TN), lambda j: (0, j)),
        out_shape=jax.ShapeDtypeStruct((1, n), F32),
        compiler_params=_params(("parallel",)),
        name="ada",
    )(c.reshape(D_MODEL, 1), w_ada, b_ada.reshape(1, n))


IN_TM = 512
IN_NC = 256


def _modulated_rms(x, g, scale, shift):
    ms = jnp.mean(x * x, axis=-1, keepdims=True)
    return (x * lax.rsqrt(ms + EPS) * g) * (1.0 + scale) + shift


def _in_kernel(x_ref, g_ref, sc_ref, sh_ref, w_ref, b_ref, q_ref, kv_ref, uc_ref):
    h = _modulated_rms(x_ref[...], g_ref[...], sc_ref[...], sh_ref[...]).astype(BF16)

    def proj(lo, n, o_ref):
        for c in range(0, n, IN_NC):
            u = jnp.dot(h, w_ref[:, lo + c:lo + c + IN_NC], preferred_element_type=F32)
            o_ref[:, c:c + IN_NC] = u + b_ref[:, lo + c:lo + c + IN_NC]

    proj(0, D_ATTN, q_ref)
    proj(D_ATTN, 2 * D_KV, kv_ref)
    proj(D_ATTN + 2 * D_KV, 2 * D_CONV, uc_ref)


def _in_proj(x2, g1, scale1, shift1, w_in_bf, b_in):
    t = x2.shape[0]
    vec = lambda n: pl.BlockSpec((1, n), lambda i: (0, 0))
    return pl.pallas_call(
        _in_kernel,
        grid=(t // IN_TM,),
        in_specs=[
            pl.BlockSpec((IN_TM, D_MODEL), lambda i: (i, 0)),
            vec(D_MODEL), vec(D_MODEL), vec(D_MODEL),
            pl.BlockSpec((D_MODEL, D_IN), lambda i: (0, 0)),
            vec(D_IN),
        ],
        out_specs=[
            pl.BlockSpec((IN_TM, D_ATTN), lambda i: (i, 0)),
            pl.BlockSpec((IN_TM, 2 * D_KV), lambda i: (i, 0)),
            pl.BlockSpec((IN_TM, 2 * D_CONV), lambda i: (i, 0)),
        ],
        out_shape=[
            jax.ShapeDtypeStruct((t, D_ATTN), F32),
            jax.ShapeDtypeStruct((t, 2 * D_KV), F32),
            jax.ShapeDtypeStruct((t, 2 * D_CONV), F32),
        ],
        compiler_params=_params(("parallel",)),
        name="in_proj",
    )(x2, g1, scale1, shift1, w_in_bf, b_in)


ATT_R = 4


def _split_dot(a, b_bf):
    hi = a.astype(BF16)
    lo = (a - hi.astype(F32)).astype(BF16)
    return (jnp.dot(hi, b_bf, preferred_element_type=F32)
            + jnp.dot(lo, b_bf, preferred_element_type=F32))


def _attn_kernel(sinks_ref, q_ref, kvp_ref, kvc_ref, bias_ref, gq_ref, gk_ref, go_ref,
                 hq_ref, hqt_ref, hk_ref, o_ref, y_ref):
    step = pl.program_id(0)
    lane = lax.broadcasted_iota(jnp.int32, (2 * BLOCK, LANES), 1)
    low = lane < HEAD_DIM
    col = lax.broadcasted_iota(jnp.int32, (BLOCK, 2 * BLOCK), 1)
    first_lo = jnp.where(step == 0, BLOCK, 0)

    for r in range(ATT_R):
        rows = slice(r * BLOCK, (r + 1) * BLOCK)
        q = q_ref[rows, :]
        ssq = _split_dot(_split_dot(q * q, hq_ref[...]), hqt_ref[...])
        qn = (q * lax.rsqrt(ssq * (1.0 / HEAD_DIM) + EPS) * gq_ref[...]).astype(BF16)

        if r == 0:
            kv_prev = kvp_ref[...]
        else:
            kv_prev = kvc_ref[(r - 1) * BLOCK:r * BLOCK, :]
        kv = jnp.concatenate([kv_prev, kvc_ref[rows, :]], axis=0)
        k = kv[:, :LANES]
        v = kv[:, LANES:]
        kss = _split_dot(k * k, hk_ref[...])
        kn = k * lax.rsqrt(kss * (1.0 / HEAD_DIM) + EPS) * gk_ref[...]
        kn_sw = pltpu.roll(kn, HEAD_DIM, axis=1)
        v_sw = pltpu.roll(v, HEAD_DIM, axis=1)
        zero = jnp.zeros_like(kn)
        k_lo = [jnp.where(low, kn, zero).astype(BF16), jnp.where(low, kn_sw, zero).astype(BF16)]
        k_hi = [jnp.where(low, zero, kn_sw).astype(BF16), jnp.where(low, zero, kn).astype(BF16)]
        v_lo = [jnp.where(low, v, zero).astype(BF16), jnp.where(low, v_sw, zero).astype(BF16)]
        v_hi = [jnp.where(low, zero, v_sw).astype(BF16), jnp.where(low, zero, v).astype(BF16)]

        for p in range(N_Q_HEADS // 2):
            g = (2 * p) // (N_Q_HEADS // N_KV_HEADS)
            qp = qn[:, p * LANES:(p + 1) * LANES]
            acc = None
            for half, (kz, vz) in enumerate(((k_lo[g], v_lo[g]), (k_hi[g], v_hi[g]))):
                h = 2 * p + half
                s = lax.dot_general(qp, kz, (((1,), (1,)), ((), ())), preferred_element_type=F32)
                s = s + bias_ref[h]
                if r == 0:
                    s = jnp.where(col >= first_lo, s, NEG_INF)
                sink = sinks_ref[h]
                m = jnp.maximum(jnp.max(s, axis=-1, keepdims=True), sink)
                e = jnp.exp(s - m)
                denom = jnp.sum(e, axis=-1, keepdims=True) + jnp.exp(sink - m)
                pv = jnp.dot(e.astype(BF16), vz, preferred_element_type=F32)
                pv = pv * (1.0 / denom)
                acc = pv if acc is None else acc + pv
            y_ref[:, p * LANES:(p + 1) * LANES] = acc

        y = y_ref[...]
        ms = jnp.mean(y * y, axis=-1, keepdims=True)
        o_ref[rows, :] = (y * lax.rsqrt(ms + EPS) * go_ref[...]).astype(BF16)


def _attention(q, kv, bias, sinks, gq_t, gk_t, g_out):
    t = q.shape[0]
    tile = ATT_R * BLOCK
    head_of_lane = jnp.arange(D_ATTN) // HEAD_DIM
    hq = (head_of_lane[:, None] == jnp.arange(LANES)[None, :]).astype(BF16)
    hk = _head_indicator(LANES)
    grid_spec = pltpu.PrefetchScalarGridSpec(
        num_scalar_prefetch=0,
        grid=(t // tile,),
        in_specs=[
            pl.BlockSpec(memory_space=pltpu.SMEM),
            pl.BlockSpec((tile, D_ATTN), lambda i: (i, 0)),
            pl.BlockSpec((BLOCK, 2 * D_KV), lambda i: (jnp.maximum(i * ATT_R - 1, 0), 0)),
            pl.BlockSpec((tile, 2 * D_KV), lambda i: (i, 0)),
            pl.BlockSpec((N_Q_HEADS, BLOCK, 2 * BLOCK), lambda i: (0, 0, 0)),
            pl.BlockSpec((1, D_ATTN), lambda i: (0, 0)),
            pl.BlockSpec((1, LANES), lambda i: (0, 0)),
            pl.BlockSpec((1, D_ATTN), lambda i: (0, 0)),
            pl.BlockSpec((D_ATTN, LANES), lambda i: (0, 0)),
            pl.BlockSpec((LANES, D_ATTN), lambda i: (0, 0)),
            pl.BlockSpec((LANES, LANES), lambda i: (0, 0)),
        ],
        out_specs=pl.BlockSpec((tile, D_ATTN), lambda i: (i, 0)),
        scratch_shapes=[pltpu.VMEM((BLOCK, D_ATTN), F32)],
    )
    return pl.pallas_call(
        _attn_kernel,
        grid_spec=grid_spec,
        out_shape=jax.ShapeDtypeStruct((t, D_ATTN), BF16),
        compiler_params=_params(("parallel",)),
        name="attn",
    )(sinks, q, kv, kv, bias, gq_t, gk_t, g_out, hq, hq.T, hk)


def _t5_bucket(dist):
    max_exact = N_BUCKETS // 2
    d = jnp.maximum(dist, 0)
    log_ratio = jnp.log(jnp.maximum(d, max_exact).astype(F32) / max_exact)
    large = max_exact + (log_ratio / math.log(MAX_DISTANCE / max_exact)
                         * (N_BUCKETS - max_exact)).astype(jnp.int32)
    large = jnp.minimum(large, N_BUCKETS - 1)
    return jnp.where(d < max_exact, d, large)


def _bias_table(rel_bias):
    q_local = jnp.arange(BLOCK, dtype=jnp.int32) + BLOCK
    k_local = jnp.arange(2 * BLOCK, dtype=jnp.int32)
    dist = q_local[:, None] - k_local[None, :]
    band = (dist >= 0) & (dist < WINDOW)
    bucket = _t5_bucket(dist)
    hit = (jnp.arange(N_BUCKETS, dtype=jnp.int32)[:, None] == bucket.reshape(1, -1)).astype(F32)
    bias = jnp.dot(rel_bias.astype(F32).T, hit, precision=lax.Precision.HIGHEST)
    return jnp.where(band[None], bias.reshape(N_Q_HEADS, BLOCK, 2 * BLOCK), NEG_INF)


def _head_indicator(n):
    i = jnp.arange(n) // HEAD_DIM
    return (i[:, None] == i[None, :]).astype(BF16)


CONV_TT = 512
CONV_RC = 64
CONV_CC = 256


def _conv_kernel(u_ref, halo_ref, w_ref, b_ref, lg_ref, lb_ref, go_ref, o_ref, h_ref, y_ref):
    step = pl.program_id(0)

    def glu(u):
        return u[:, :D_CONV] * jax.nn.sigmoid(u[:, D_CONV:])

    hh = glu(halo_ref[...])
    h_ref[0, 0:HALO, :] = jnp.where(step == 0, jnp.zeros_like(hh), hh)
    h_ref[0, HALO:, :] = glu(u_ref[...])

    n_rows = CONV_TT + HALO
    for s in range(1, SUBLANES):
        for c0 in range(0, D_CONV, CONV_CC):
            cs = slice(c0, c0 + CONV_CC)
            h_ref[s, :, cs] = pltpu.roll(h_ref[0, :, cs], n_rows - s, axis=0)

    off = HALO - (CONV_WIDTH - 1)
    for r0 in range(0, CONV_TT, CONV_RC):
        for c0 in range(0, D_CONV, CONV_CC):
            cs = slice(c0, c0 + CONV_CC)
            acc = jnp.broadcast_to(b_ref[:, cs], (CONV_RC, CONV_CC))
            for j in range(CONV_WIDTH):
                s = (r0 + off + j) % SUBLANES
                a = r0 + off + j - s
                acc = acc + w_ref[j:j + 1, cs] * h_ref[s, a:a + CONV_RC, cs]
            y_ref[r0:r0 + CONV_RC, cs] = acc

    y = y_ref[...]
    mu = jnp.mean(y, axis=-1, keepdims=True)
    yc = y - mu
    var = jnp.mean(yc * yc, axis=-1, keepdims=True)
    z = yc * lax.rsqrt(var + EPS) * lg_ref[...] + lb_ref[...]
    s = z * jax.nn.sigmoid(z)
    ms = jnp.mean(s * s, axis=-1, keepdims=True)
    o_ref[...] = (s * lax.rsqrt(ms + EPS) * go_ref[...]).astype(BF16)


def _conv(uc, w_dw, b_dw, ln_g, ln_b, g_out):
    t = uc.shape[0]
    vec = lambda: pl.BlockSpec((1, D_CONV), lambda i: (0, 0))
    per = CONV_TT // HALO
    return pl.pallas_call(
        _conv_kernel,
        grid=(t // CONV_TT,),
        in_specs=[
            pl.BlockSpec((CONV_TT, 2 * D_CONV), lambda i: (i, 0)),
            pl.BlockSpec((HALO, 2 * D_CONV), lambda i: (jnp.maximum(i * per - 1, 0), 0)),
            pl.BlockSpec((HALO, D_CONV), lambda i: (0, 0)),
            vec(), vec(), vec(), vec(),
        ],
        out_specs=pl.BlockSpec((CONV_TT, D_CONV), lambda i: (i, 0)),
        out_shape=jax.ShapeDtypeStruct((t, D_CONV), BF16),
        scratch_shapes=[pltpu.VMEM((SUBLANES, CONV_TT + HALO, D_CONV), F32),
                        pltpu.VMEM((CONV_TT, D_CONV), F32)],
        compiler_params=_params(("parallel",)),
        name="conv",
    )(uc, uc, w_dw, b_dw, ln_g, ln_b, g_out)


OUT_TM = 512
D_PACK = D_MODEL // 2
HI_MASK = 0xFFFF0000


def _pack_bf16_pairs(h):
    lo = lax.bitcast_convert_type(h[:, :D_PACK].astype(BF16).astype(F32), jnp.uint32)
    hi = lax.bitcast_convert_type(h[:, D_PACK:].astype(BF16).astype(F32), jnp.uint32)
    return (lo >> 16) | (hi & jnp.uint32(HI_MASK))


def _unpack_pairs_f32(w):
    lo = lax.bitcast_convert_type(w << 16, F32)
    hi = lax.bitcast_convert_type(w & jnp.uint32(HI_MASK), F32)
    return lo, hi


def _unpack_bf16_pairs(w):
    lo, hi = _unpack_pairs_f32(w)
    return lo.astype(BF16), hi.astype(BF16)


def _out_kernel(ma_ref, mc_ref, x_ref, w_ref, bo_ref, g1_ref, g2_ref, sc_ref, sh_ref, wr_ref, br_ref,
                tri_ref, xm_ref, h2_ref, idx_ref, rank_ref, wt_ref, cnt_ref, carry_ref):
    step = pl.program_id(0)

    @pl.when(step == 0)
    def _():
        carry_ref[...] = jnp.zeros_like(carry_ref)

    y = (jnp.dot(ma_ref[...], w_ref[0:D_ATTN, :], preferred_element_type=F32)
         + jnp.dot(mc_ref[...], w_ref[D_ATTN:, :], preferred_element_type=F32) + bo_ref[...])
    xm = x_ref[...] + g1_ref[...] * y
    xm_ref[...] = xm
    h2 = _modulated_rms(xm, g2_ref[...], sc_ref[...], sh_ref[...])
    h2_ref[...] = _pack_bf16_pairs(h2)

    both = jnp.dot(h2.astype(BF16), wr_ref[...], preferred_element_type=F32)
    logits = both[:, :N_EXPERTS] + both[:, N_EXPERTS:] + br_ref[...]
    tm = logits.shape[0]
    lane = lax.broadcasted_iota(jnp.int32, (tm, N_EXPERTS), 1).astype(F32)
    vals, idxs = [], []
    l = logits
    for _ in range(TOP_K):
        m = jnp.max(l, axis=-1, keepdims=True)
        i = jnp.min(jnp.where(l == m, lane, float(N_EXPERTS)), axis=-1, keepdims=True)
        vals.append(m)
        idxs.append(i)
        l = jnp.where(lane == i, -jnp.inf, l)
    es = [jnp.exp(v - vals[0]) for v in vals]
    tot = es[0] + es[1] + es[2] + es[3]
    ws = [e / tot for e in es]

    hot = [(lane == i).astype(F32) for i in idxs]
    hot_all = hot[0] + hot[1] + hot[2] + hot[3]
    before = jnp.dot(tri_ref[...], hot_all.astype(BF16), preferred_element_type=F32) + carry_ref[...]
    ranks = [jnp.sum(h * before, axis=-1, keepdims=True) for h in hot]
    carry_ref[...] = carry_ref[...] + jnp.sum(hot_all, axis=0, keepdims=True)
    cnt_ref[...] = carry_ref[...].astype(jnp.int32)

    slot = lax.broadcasted_iota(jnp.int32, (tm, LANES), 1)

    def pack(cols):
        out = jnp.zeros((tm, LANES), F32)
        for k in range(TOP_K):
            out = jnp.where(slot == k, cols[k], out)
        return out

    idx_ref[...] = pack(idxs).astype(jnp.int32)
    rank_ref[...] = pack(ranks).astype(jnp.int32)
    wt_ref[...] = pack(ws)


def _out_proj(ma, mc, x2, w_out_bf, b_out, gate1, g2, scale2, shift2, w_router, b_router):
    t = x2.shape[0]
    vec = lambda n: pl.BlockSpec((1, n), lambda i: (0, 0))
    tri = jnp.tril(jnp.ones((OUT_TM, OUT_TM), F32), -1).astype(BF16)
    wr_hi = w_router.astype(BF16)
    wr_lo = (w_router - wr_hi.astype(F32)).astype(BF16)
    w_router = jnp.concatenate([wr_hi, wr_lo], axis=1)
    return pl.pallas_call(
        _out_kernel,
        grid=(t // OUT_TM,),
        in_specs=[
            pl.BlockSpec((OUT_TM, D_ATTN), lambda i: (i, 0)),
            pl.BlockSpec((OUT_TM, D_CONV), lambda i: (i, 0)),
            pl.BlockSpec((OUT_TM, D_MODEL), lambda i: (i, 0)),
            pl.BlockSpec((D_MODEL, D_MODEL), lambda i: (0, 0)),
            vec(D_MODEL), vec(D_MODEL), vec(D_MODEL), vec(D_MODEL), vec(D_MODEL),
            pl.BlockSpec((D_MODEL, 2 * N_EXPERTS), lambda i: (0, 0)),
            vec(N_EXPERTS),
            pl.BlockSpec((OUT_TM, OUT_TM), lambda i: (0, 0)),
        ],
        out_specs=[
            pl.BlockSpec((OUT_TM, D_MODEL), lambda i: (i, 0)),
            pl.BlockSpec((OUT_TM, D_PACK), lambda i: (i, 0)),
            pl.BlockSpec((OUT_TM, LANES), lambda i: (i, 0)),
            pl.BlockSpec((OUT_TM, LANES), lambda i: (i, 0)),
            pl.BlockSpec((OUT_TM, LANES), lambda i: (i, 0)),
            pl.BlockSpec((1, N_EXPERTS), lambda i: (0, 0)),
        ],
        out_shape=[
            jax.ShapeDtypeStruct((t, D_MODEL), F32),
            jax.ShapeDtypeStruct((t, D_PACK), jnp.uint32),
            jax.ShapeDtypeStruct((t, LANES), jnp.int32),
            jax.ShapeDtypeStruct((t, LANES), jnp.int32),
            jax.ShapeDtypeStruct((t, LANES), F32),
            jax.ShapeDtypeStruct((1, N_EXPERTS), jnp.int32),
        ],
        scratch_shapes=[pltpu.VMEM((1, N_EXPERTS), F32)],
        compiler_params=_params(("arbitrary",)),
        name="out_proj",
    )(ma, mc, x2, w_out_bf, b_out, gate1, g2, scale2, shift2, w_router, b_router, tri)


DISP_TD = 512
DISP_NBUF = 3


class _CopyGroup:
    def __init__(self, copies):
        self.copies = copies

    def start(self):
        for cp in self.copies:
            cp.start()

    def wait(self):
        for cp in self.copies:
            cp.wait()

ZERO_RUN = 64


def _dispatch_kernel(dest_ref, pad_lo_ref, pad_hi_ref, nu_ref, h2_ref, xs_ref,
                     stage_ref, zero_ref, in_sems, out_sems, zsem):
    step = pl.program_id(0)
    last = pl.num_programs(0) - 1
    slot = lax.rem(step, DISP_NBUF)
    rows_per_chunk = DISP_TD * TOP_K

    def load(chunk, s):
        rows = pl.ds(pl.multiple_of(chunk * DISP_TD, DISP_TD), DISP_TD)
        return _CopyGroup([pltpu.make_async_copy(h2_ref.at[rows, pl.ds(j * LANES, LANES)], stage_ref.at[s, :, j, :],
                                                 in_sems.at[s]) for j in range(stage_ref.shape[2])])

    def wait_chunk(s):
        whole = xs_ref.at[pl.ds(0, rows_per_chunk)]
        pltpu.make_async_copy(whole, whole, out_sems.at[s]).wait()

    @pl.when(step == 0)
    def _():
        load(0, 0).start()

        @pl.when(last >= 1)
        def _():
            load(1, 1).start()

    load(step, slot).wait()

    def issue(r, carry):
        for k in range(TOP_K):
            d = dest_ref[(step * DISP_TD + r) * TOP_K + k]
            pltpu.make_async_copy(stage_ref.at[slot, r], xs_ref.at[d], out_sems.at[slot]).start(priority=k % 2)
        return carry

    lax.fori_loop(0, DISP_TD, issue, 0, unroll=2)

    @pl.when(step == 0)
    def _():
        zero_ref[...] = jnp.zeros_like(zero_ref)

        def zero_rows(d, n):
            d = d if n == 1 else pl.multiple_of(d, n)
            return pltpu.make_async_copy(zero_ref.at[pl.ds(0, n)], xs_ref.at[pl.ds(d, n)], zsem)

        def ranges(e):
            lo, hi = pad_lo_ref[e], pad_hi_ref[e]
            a = jnp.minimum((lo + SUBLANES - 1) // SUBLANES * SUBLANES, hi)
            b = jnp.minimum((lo + ZERO_RUN - 1) // ZERO_RUN * ZERO_RUN, hi)
            return ((lo, a - lo, 1), (a, (b - a) // SUBLANES, SUBLANES), (b, (hi - b) // ZERO_RUN, ZERO_RUN))

        def for_all_runs(act):
            def per_expert(e, carry):
                for start, count, n in ranges(e):
                    lax.fori_loop(0, count, lambda i, c, s=start, n=n: (act(zero_rows(s + i * n, n)), c)[1], 0)
                return carry

            lax.fori_loop(0, N_EXPERTS, per_expert, 0)
            n_blocks = xs_ref.shape[0] // EXPERT_TM
            lax.fori_loop(nu_ref[0], n_blocks,
                          lambda blk, c: (act(zero_rows(blk * EXPERT_TM, EXPERT_TM)), c)[1], 0)

        for_all_runs(lambda copy: copy.start())
        for_all_runs(lambda copy: copy.wait())

    @pl.when(step > 0)
    def _():
        wait_chunk(lax.rem(step + DISP_NBUF - 1, DISP_NBUF))

    @pl.when(step + 2 <= last)
    def _():
        load(step + 2, lax.rem(step + 2, DISP_NBUF)).start()

    @pl.when(step == last)
    def _():
        wait_chunk(slot)


def _dispatch(h2p, dest_flat, pad_lo, pad_hi, n_used, n_rows):
    t = h2p.shape[0]
    grid_spec = pltpu.PrefetchScalarGridSpec(
        num_scalar_prefetch=4,
        grid=(t // DISP_TD,),
        in_specs=[pl.BlockSpec(memory_space=pl.ANY)],
        out_specs=pl.BlockSpec(memory_space=pl.ANY),
        scratch_shapes=[pltpu.VMEM((DISP_NBUF, DISP_TD, D_PACK // LANES, LANES), jnp.uint32),
                        pltpu.VMEM((EXPERT_TM, D_PACK // LANES, LANES), jnp.uint32),
                        pltpu.SemaphoreType.DMA((DISP_NBUF,)), pltpu.SemaphoreType.DMA((DISP_NBUF,)),
                        pltpu.SemaphoreType.DMA(())],
    )
    return pl.pallas_call(
        _dispatch_kernel,
        grid_spec=grid_spec,
        out_shape=jax.ShapeDtypeStruct((n_rows, D_PACK // LANES, LANES), jnp.uint32),
        compiler_params=_params(("arbitrary",)),
        name="dispatch",
    )(dest_flat, pad_lo, pad_hi, n_used, h2p)


UP_TF = 1024
DOWN_TN = 2048
UP_AHEAD = 2


def _weight_index_map(n_chunks):
    def index_map(c, b, be, nu, first, nxt, full):
        del nu, full
        in_last = nxt[b] < 0
        wrap = jnp.logical_and(in_last, c + 1 < n_chunks)
        e_next = jnp.where(in_last, jnp.where(wrap, be[0], be[b]), nxt[b])
        c_next = jnp.where(wrap, c + 1, c)
        is_first = first[b] == 1
        return jnp.where(is_first, be[b], e_next), 0, jnp.where(is_first, c, c_next)

    return index_map


def _row_cases(b, nu_ref, full_ref, o_ref, compute):
    half = EXPERT_TM // 2
    used = b < nu_ref[0]

    @pl.when(jnp.logical_and(used, full_ref[b] == 1))
    def _():
        compute(EXPERT_TM)

    @pl.when(jnp.logical_and(used, full_ref[b] == 0))
    def _():
        compute(half)
        o_ref[half:, :] = jnp.zeros((half, o_ref.shape[1]), o_ref.dtype)

    @pl.when(jnp.logical_not(used))
    def _():
        o_ref[...] = jnp.zeros(o_ref.shape, o_ref.dtype)


def _up_kernel(be_ref, nu_ref, first_ref, nxt_ref, full_ref, xs_ref, wg_ref, wu_ref, bg_ref, bu_ref, h_ref,
               wg_bf, wu_bf, xbuf_ref, xsems):
    f = pl.program_id(0)
    b = pl.program_id(1)
    nb = pl.num_programs(1)
    step = f * nb + b
    n_steps = pl.num_programs(0) * nb
    n_buf = UP_AHEAD + 1
    slot = lax.rem(step, n_buf)
    n_sub = xs_ref.shape[1]

    def fetch(ahead):
        blk = lax.rem(b + ahead, nb)
        rows = pl.ds(pl.multiple_of(jnp.minimum(blk, nu_ref[0] - 1) * EXPERT_TM, EXPERT_TM), EXPERT_TM)
        s = lax.rem(step + ahead, n_buf)
        return _CopyGroup([pltpu.make_async_copy(xs_ref.at[rows, j, :], xbuf_ref.at[s, :, pl.ds(j * LANES, LANES)],
                                                 xsems.at[s]) for j in range(n_sub)])

    @pl.when(step == 0)
    def _():
        for ahead in range(UP_AHEAD):
            @pl.when(ahead < n_steps)
            def _():
                fetch(ahead).start()

    @pl.when(step + UP_AHEAD < n_steps)
    def _():
        fetch(UP_AHEAD).start()

    fetch(0).wait()
    x_ref = xbuf_ref.at[slot]

    @pl.when(first_ref[b] == 1)
    def _():
        wg_bf[...] = wg_ref[0].astype(BF16)
        wu_bf[...] = wu_ref[0].astype(BF16)

    def compute(m):
        x_lo, x_hi = _unpack_bf16_pairs(x_ref[0:m, :])

        def proj(w_bf, bias_ref):
            return (jnp.dot(x_lo, w_bf[0:D_PACK, :], preferred_element_type=F32)
                    + jnp.dot(x_hi, w_bf[D_PACK:, :], preferred_element_type=F32) + bias_ref[0])

        g = proj(wg_bf, bg_ref)
        lin = proj(wu_bf, bu_ref)
        g = jnp.minimum(g, SWIGLU_LIMIT)
        lin = jnp.clip(lin, -SWIGLU_LIMIT, SWIGLU_LIMIT)
        act = g * jax.nn.sigmoid(SWIGLU_ALPHA * g) * (lin + 1.0)
        h_ref[0:m, :] = act.astype(BF16)

    _row_cases(b, nu_ref, full_ref, h_ref, compute)


def _up(xs, tables, w_gate, b_gate, w_up, b_up):
    n_rows = xs.shape[0]
    nb = n_rows // EXPERT_TM
    n_chunks = D_FF // UP_TF
    wsel = _weight_index_map(n_chunks)
    bsel = lambda f, b, be, nu, first, nxt, full: (be[b], 0, f)
    grid_spec = pltpu.PrefetchScalarGridSpec(
        num_scalar_prefetch=5,
        grid=(n_chunks, nb),
        in_specs=[
            pl.BlockSpec(memory_space=pl.ANY),
            pl.BlockSpec((1, D_MODEL, UP_TF), wsel),
            pl.BlockSpec((1, D_MODEL, UP_TF), wsel),
            pl.BlockSpec((1, 1, UP_TF), bsel),
            pl.BlockSpec((1, 1, UP_TF), bsel),
        ],
        out_specs=pl.BlockSpec((EXPERT_TM, UP_TF), lambda f, b, be, nu, first, nxt, full: (b, f)),
        scratch_shapes=[pltpu.VMEM((D_MODEL, UP_TF), BF16), pltpu.VMEM((D_MODEL, UP_TF), BF16),
                        pltpu.VMEM((UP_AHEAD + 1, EXPERT_TM, D_PACK), jnp.uint32),
                        pltpu.SemaphoreType.DMA((UP_AHEAD + 1,))],
    )
    return pl.pallas_call(
        _up_kernel,
        grid_spec=grid_spec,
        out_shape=jax.ShapeDtypeStruct((n_rows, D_FF), BF16),
        compiler_params=_params(("arbitrary", "arbitrary"), EXPERT_VMEM_LIMIT),
        name="expert_up",
    )(*tables, xs, w_gate, w_up,
      b_gate.reshape(N_EXPERTS, 1, D_FF), b_up.reshape(N_EXPERTS, 1, D_FF))


def _down_kernel(be_ref, nu_ref, first_ref, nxt_ref, full_ref, h_ref, wd_ref, bd_ref, ys_ref, wd_bf,
                 ybuf_ref, ysems):
    b = pl.program_id(1)
    last = pl.num_programs(1) - 1
    slot = b % 2

    def write_back(blk, s):
        rows = pl.ds(pl.multiple_of(blk * EXPERT_TM, EXPERT_TM), EXPERT_TM)
        return _CopyGroup([pltpu.make_async_copy(ybuf_ref.at[s, :, pl.ds(j * LANES, LANES)], ys_ref.at[rows, j, :],
                                                 ysems.at[s]) for j in range(ys_ref.shape[1])])

    @pl.when(b >= 2)
    def _():
        write_back(b - 2, slot).wait()

    @pl.when(first_ref[b] == 1)
    def _():
        wd_bf[...] = wd_ref[0].astype(BF16)

    y_ref = ybuf_ref.at[slot]

    def compute(m):
        y = jnp.dot(h_ref[0:m, :], wd_bf[...], preferred_element_type=F32) + bd_ref[0]
        y_ref[0:m, :] = _pack_bf16_pairs(y)

    _row_cases(b, nu_ref, full_ref, y_ref, compute)
    write_back(b, slot).start()

    @pl.when(b == last)
    def _():
        @pl.when(b >= 1)
        def _():
            write_back(b - 1, 1 - slot).wait()

        write_back(b, slot).wait()


def _down(hs, tables, w_down, b_down):
    n_rows = hs.shape[0]
    nb = n_rows // EXPERT_TM
    n_chunks = D_MODEL // DOWN_TN
    assert n_chunks == 1
    row = lambda n, b, be, nu, first, nxt, full: (jnp.minimum(b, nu[0] - 1), 0)
    grid_spec = pltpu.PrefetchScalarGridSpec(
        num_scalar_prefetch=5,
        grid=(n_chunks, nb),
        in_specs=[
            pl.BlockSpec((EXPERT_TM, D_FF), row),
            pl.BlockSpec((1, D_FF, DOWN_TN), _weight_index_map(n_chunks)),
            pl.BlockSpec((1, 1, DOWN_TN), lambda n, b, be, nu, first, nxt, full: (be[b], 0, n)),
        ],
        out_specs=pl.BlockSpec(memory_space=pl.ANY),
        scratch_shapes=[pltpu.VMEM((D_FF, DOWN_TN), BF16),
                        pltpu.VMEM((2, EXPERT_TM, D_PACK), jnp.uint32), pltpu.SemaphoreType.DMA((2,))],
    )
    return pl.pallas_call(
        _down_kernel,
        grid_spec=grid_spec,
        out_shape=jax.ShapeDtypeStruct((n_rows, D_PACK // LANES, LANES), jnp.uint32),
        compiler_params=_params(("arbitrary", "arbitrary"), EXPERT_VMEM_LIMIT),
        name="expert_down",
    )(*tables, hs, w_down, b_down.reshape(N_EXPERTS, 1, D_MODEL))


COMB_TC = 512


def _combine_kernel(dest_ref, wsm_ref, xm_ref, g2_ref, ys_ref, o_ref, buf_ref, acc_ref, sems):
    step = pl.program_id(0)
    last = pl.num_programs(0) - 1
    slot = step % 2

    def gather_tile(tile, s):
        def issue(r, carry):
            for k in range(TOP_K):
                d = dest_ref[(tile * COMB_TC + r) * TOP_K + k]
                pltpu.make_async_copy(ys_ref.at[d], buf_ref.at[s, k * COMB_TC + r],
                                      sems.at[s]).start(priority=k % 2)
            return carry

        lax.fori_loop(0, COMB_TC, issue, 0, unroll=4)

    @pl.when(step == 0)
    def _():
        gather_tile(0, 0)

    @pl.when(step < last)
    def _():
        gather_tile(step + 1, 1 - slot)

    pltpu.make_async_copy(ys_ref.at[pl.ds(0, TOP_K * COMB_TC)], buf_ref.at[slot], sems.at[slot]).wait()

    def reduce_token(r, carry):
        acc_lo = acc_hi = None
        for k in range(TOP_K):
            w = wsm_ref[(step * COMB_TC + r) * TOP_K + k]
            lo, hi = _unpack_pairs_f32(buf_ref[slot, k * COMB_TC + r])
            acc_lo = w * lo if acc_lo is None else acc_lo + w * lo
            acc_hi = w * hi if acc_hi is None else acc_hi + w * hi
        acc_ref[0, r] = acc_lo
        acc_ref[1, r] = acc_hi
        return carry

    lax.fori_loop(0, COMB_TC, reduce_token, 0, unroll=4)

    for half in range(2):
        for j in range(D_PACK // LANES):
            cols = slice(half * D_PACK + j * LANES, half * D_PACK + (j + 1) * LANES)
            o_ref[:, cols] = xm_ref[:, cols] + g2_ref[:, cols] * acc_ref[half, :, j, :]


def _combine(ys, dest_flat, xm, wts_flat, gate2):
    t = xm.shape[0]
    n_sub = D_PACK // LANES
    grid_spec = pltpu.PrefetchScalarGridSpec(
        num_scalar_prefetch=2,
        grid=(t // COMB_TC,),
        in_specs=[
            pl.BlockSpec((COMB_TC, D_MODEL), lambda i, d, w: (i, 0)),
            pl.BlockSpec((1, D_MODEL), lambda i, d, w: (0, 0)),
            pl.BlockSpec(memory_space=pl.ANY),
        ],
        out_specs=pl.BlockSpec((COMB_TC, D_MODEL), lambda i, d, w: (i, 0)),
        scratch_shapes=[pltpu.VMEM((2, TOP_K * COMB_TC, n_sub, LANES), jnp.uint32),
                        pltpu.VMEM((2, COMB_TC, n_sub, LANES), F32),
                        pltpu.SemaphoreType.DMA((2,))],
    )
    return pl.pallas_call(
        _combine_kernel,
        grid_spec=grid_spec,
        out_shape=jax.ShapeDtypeStruct((t, D_MODEL), F32),
        compiler_params=_params(("arbitrary",)),
        name="combine",
    )(dest_flat, wts_flat, xm, gate2, ys)


def _routing_tables(idx, rank, counts, n_tok):
    padded = ((counts + EXPERT_TM - 1) // EXPERT_TM) * EXPERT_TM
    pend = jnp.cumsum(padded)
    pstart = pend - padded
    experts = jnp.arange(N_EXPERTS, dtype=jnp.int32)
    start_of = jnp.sum(jnp.where(idx[..., None] == experts, pstart, 0), axis=-1)
    dest = (start_of + rank).astype(jnp.int32).reshape(-1)
    n_rows = n_tok * TOP_K + N_EXPERTS * EXPERT_TM
    nb = n_rows // EXPERT_TM
    block_start = jnp.arange(nb, dtype=jnp.int32) * EXPERT_TM
    block_expert = jnp.minimum(jnp.sum(pend[None, :] <= block_start[:, None], axis=1),
                               N_EXPERTS - 1).astype(jnp.int32)
    n_used = (pend[-1:] // EXPERT_TM).astype(jnp.int32)
    pad_lo = (pstart + counts).astype(jnp.int32)
    pad_hi = pend.astype(jnp.int32)

    blocks = jnp.arange(nb, dtype=jnp.int32)
    block_expert = jnp.where(blocks < n_used[0], block_expert, block_expert[jnp.maximum(n_used[0] - 1, 0)])
    prev = jnp.concatenate([block_expert[:1], block_expert[:-1]])
    first = jnp.logical_or(blocks == 0, block_expert != prev)
    first_pos = jnp.where(first, blocks, nb)
    next_first = jnp.flip(lax.cummin(jnp.flip(jnp.concatenate([first_pos[1:], jnp.full((1,), nb, jnp.int32)]))))
    nxt = jnp.where(next_first < nb, block_expert[jnp.minimum(next_first, nb - 1)], -1).astype(jnp.int32)
    rows_end = jnp.sum(jnp.where(block_expert[:, None] == experts, pad_lo, 0), axis=-1)
    full = (rows_end > block_start + EXPERT_TM // 2).astype(jnp.int32)
    tables = (block_expert, n_used, first.astype(jnp.int32), nxt, full)
    return dest, tables, pad_lo, pad_hi, n_rows


def _layer(x2, c, w_ada, b_ada, g_norm1, w_in, b_in, g_q, g_k, sinks, rel_bias, w_dw, b_dw, ln_g, ln_b,
           g_out_attn, g_out_conv, w_out, b_out, g_norm2, w_router, b_router,
           w_gate, b_gate, w_up, b_up, w_down, b_down):
    t = x2.shape[0]
    row = lambda v: v.reshape(1, -1)
    mod = _ada(c, w_ada, b_ada)
    shift1, scale1, gate1, shift2, scale2, gate2 = [mod[:, i * D_MODEL:(i + 1) * D_MODEL] for i in range(6)]

    q, kv, uc = _in_proj(x2, row(g_norm1), scale1, shift1, w_in.astype(BF16), row(b_in))

    gq_t = row(jnp.tile(g_q, N_Q_HEADS)) * (HEAD_DIM ** -0.5)
    gk_t = row(jnp.tile(g_k, N_KV_HEADS))
    ma = _attention(q, kv, _bias_table(rel_bias), sinks, gq_t, gk_t, row(g_out_attn))
    w_dw_p = jnp.concatenate([w_dw, jnp.zeros((HALO - CONV_WIDTH, D_CONV), w_dw.dtype)], axis=0)
    mc = _conv(uc, w_dw_p, row(b_dw), row(ln_g), row(ln_b), row(g_out_conv))

    xm, h2, idx, rank, wts, counts = _out_proj(
        ma, mc, x2, w_out.astype(BF16), row(b_out), gate1, row(g_norm2), scale2, shift2,
        w_router, row(b_router))

    dest, tables, pad_lo, pad_hi, n_rows = _routing_tables(idx[:, :TOP_K], rank[:, :TOP_K], counts[0], t)
    xs = _dispatch(h2, dest, pad_lo, pad_hi, tables[1], n_rows)
    hs = _up(xs, tables, w_gate, b_gate, w_up, b_up)
    ys = _down(hs, tables, w_down, b_down)
    return _combine(ys, dest, xm, wts[:, :TOP_K].reshape(-1), gate2)


def kernel(x, c, w_ada, b_ada, g_norm1, w_in, b_in, g_q, g_k, sinks, rel_bias, w_dw, b_dw, ln_g, ln_b,
           g_out_attn, g_out_conv, w_out, b_out, g_norm2, w_router, b_router,
           w_gate, b_gate, w_up, b_up, w_down, b_down):
    b, t, d = x.shape
    assert b == 1 and d == D_MODEL and w_ada.shape[0] == 1
    out = _layer(x.reshape(t, d), c, w_ada[0], b_ada[0], g_norm1[0], w_in[0], b_in[0], g_q[0], g_k[0],
                 sinks[0], rel_bias, w_dw[0], b_dw[0], ln_g[0], ln_b[0], g_out_attn[0], g_out_conv[0],
                 w_out[0], b_out[0], g_norm2[0], w_router[0], b_router[0],
                 w_gate[0], b_gate[0], w_up[0], b_up[0], w_down[0], b_down[0])
    return out.reshape(b, t, d)
```

```python
import math

import jax
import jax.numpy as jnp
from jax import lax
from jax.experimental import pallas as pl
from jax.experimental.pallas import tpu as pltpu

D_MODEL = 2048
HEAD_DIM = 64
N_Q_HEADS = 16
N_KV_HEADS = 2
D_ATTN = N_Q_HEADS * HEAD_DIM
D_KV = N_KV_HEADS * HEAD_DIM
D_CONV = D_MODEL - D_ATTN
D_IN = D_ATTN + 2 * D_KV + 2 * D_CONV
WINDOW = 128
BLOCK = 128
CONV_WIDTH = 31
N_BUCKETS = 32
MAX_DISTANCE = 128
N_EXPERTS = 32
TOP_K = 4
D_FF = D_MODEL
SWIGLU_LIMIT = 7.0
SWIGLU_ALPHA = 1.702
EPS = 1e-6
NEG_INF = -1e30

LANES = 128
SUBLANES = 8
VMEM_LIMIT = 56 * 1024 * 1024

HALO = 32
EXPERT_TM = 512
EXPERT_VMEM_LIMIT = 60 * 1024 * 1024
F32 = jnp.float32
BF16 = jnp.bfloat16


def _params(sem, vmem_limit=VMEM_LIMIT):
    return pltpu.CompilerParams(dimension_semantics=sem, vmem_limit_bytes=vmem_limit)


ADA_TN = 1024
ADA_RC = 256


def _ada_kernel(c_ref, w_ref, b_ref, o_ref):
    tn = w_ref.shape[1]

    def body(i, acc):
        r = pl.multiple_of(i * ADA_RC, ADA_RC)
        c = c_ref[pl.ds(r, ADA_RC), :]
        sc = c * jax.nn.sigmoid(c)
        prod = w_ref[pl.ds(r, ADA_RC), :] * sc
        return acc + jnp.sum(prod.reshape(ADA_RC // 8, 8, tn), axis=0)

    acc = lax.fori_loop(0, D_MODEL // ADA_RC, body, jnp.zeros((8, tn), F32))
    o_ref[...] = jnp.sum(acc, axis=0, keepdims=True) + b_ref[...]


def _ada(c, w_ada, b_ada):
    n = w_ada.shape[1]
    return pl.pallas_call(
        _ada_kernel,
        grid=(n // ADA_TN,),
        in_specs=[
            pl.BlockSpec((D_MODEL, 1), lambda j: (0, 0)),
            pl.BlockSpec((D_MODEL, ADA_TN), lambda j: (0, j)),
            pl.BlockSpec((1, ADA_TN), lambda j: (0, j)),
        ],
        out_specs=pl.BlockSpec((1, ADA_TN), lambda j: (0, j)),
        out_shape=jax.ShapeDtypeStruct((1, n), F32),
        compiler_params=_params(("parallel",)),
        name="ada",
    )(c.reshape(D_MODEL, 1), w_ada, b_ada.reshape(1, n))


IN_TM = 512
IN_NC = 256


def _modulated_rms(x, g, scale, shift):
    ms = jnp.mean(x * x, axis=-1, keepdims=True)
    return (x * lax.rsqrt(ms + EPS) * g) * (1.0 + scale) + shift


def _in_kernel(x_ref, g_ref, sc_ref, sh_ref, w_ref, b_ref, q_ref, kv_ref, uc_ref):
    h = _modulated_rms(x_ref[...], g_ref[...], sc_ref[...], sh_ref[...]).astype(BF16)

    def proj(lo, n, o_ref):
        for c in range(0, n, IN_NC):
            u = jnp.dot(h, w_ref[:, lo + c:lo + c + IN_NC], preferred_element_type=F32)
            o_ref[:, c:c + IN_NC] = u + b_ref[:, lo + c:lo + c + IN_NC]

    proj(0, D_ATTN, q_ref)
    proj(D_ATTN, 2 * D_KV, kv_ref)
    proj(D_ATTN + 2 * D_KV, 2 * D_CONV, uc_ref)


def _in_proj(x2, g1, scale1, shift1, w_in_bf, b_in):
    t = x2.shape[0]
    vec = lambda n: pl.BlockSpec((1, n), lambda i: (0, 0))
    return pl.pallas_call(
        _in_kernel,
        grid=(t // IN_TM,),
        in_specs=[
            pl.BlockSpec((IN_TM, D_MODEL), lambda i: (i, 0)),
            vec(D_MODEL), vec(D_MODEL), vec(D_MODEL),
            pl.BlockSpec((D_MODEL, D_IN), lambda i: (0, 0)),
            vec(D_IN),
        ],
        out_specs=[
            pl.BlockSpec((IN_TM, D_ATTN), lambda i: (i, 0)),
            pl.BlockSpec((IN_TM, 2 * D_KV), lambda i: (i, 0)),
            pl.BlockSpec((IN_TM, 2 * D_CONV), lambda i: (i, 0)),
        ],
        out_shape=[
            jax.ShapeDtypeStruct((t, D_ATTN), F32),
            jax.ShapeDtypeStruct((t, 2 * D_KV), F32),
            jax.ShapeDtypeStruct((t, 2 * D_CONV), F32),
        ],
        compiler_params=_params(("parallel",)),
        name="in_proj",
    )(x2, g1, scale1, shift1, w_in_bf, b_in)


ATT_R = 4


def _split_dot(a, b_bf):
    hi = a.astype(BF16)
    lo = (a - hi.astype(F32)).astype(BF16)
    return (jnp.dot(hi, b_bf, preferred_element_type=F32)
            + jnp.dot(lo, b_bf, preferred_element_type=F32))


def _attn_kernel(sinks_ref, q_ref, kvp_ref, kvc_ref, bias_ref, gq_ref, gk_ref, go_ref,
                 hq_ref, hqt_ref, hk_ref, o_ref, y_ref):
    step = pl.program_id(0)
    lane = lax.broadcasted_iota(jnp.int32, (2 * BLOCK, LANES), 1)
    low = lane < HEAD_DIM
    col = lax.broadcasted_iota(jnp.int32, (BLOCK, 2 * BLOCK), 1)
    first_lo = jnp.where(step == 0, BLOCK, 0)

    for r in range(ATT_R):
        rows = slice(r * BLOCK, (r + 1) * BLOCK)
        q = q_ref[rows, :]
        ssq = _split_dot(_split_dot(q * q, hq_ref[...]), hqt_ref[...])
        qn = (q * lax.rsqrt(ssq * (1.0 / HEAD_DIM) + EPS) * gq_ref[...]).astype(BF16)

        if r == 0:
            kv_prev = kvp_ref[...]
        else:
            kv_prev = kvc_ref[(r - 1) * BLOCK:r * BLOCK, :]
        kv = jnp.concatenate([kv_prev, kvc_ref[rows, :]], axis=0)
        k = kv[:, :LANES]
        v = kv[:, LANES:]
        kss = _split_dot(k * k, hk_ref[...])
        kn = k * lax.rsqrt(kss * (1.0 / HEAD_DIM) + EPS) * gk_ref[...]
        kn_sw = pltpu.roll(kn, HEAD_DIM, axis=1)
        v_sw = pltpu.roll(v, HEAD_DIM, axis=1)
        zero = jnp.zeros_like(kn)
        k_lo = [jnp.where(low, kn, zero).astype(BF16), jnp.where(low, kn_sw, zero).astype(BF16)]
        k_hi = [jnp.where(low, zero, kn_sw).astype(BF16), jnp.where(low, zero, kn).astype(BF16)]
        v_lo = [jnp.where(low, v, zero).astype(BF16), jnp.where(low, v_sw, zero).astype(BF16)]
        v_hi = [jnp.where(low, zero, v_sw).astype(BF16), jnp.where(low, zero, v).astype(BF16)]

        for p in range(N_Q_HEADS // 2):
            g = (2 * p) // (N_Q_HEADS // N_KV_HEADS)
            qp = qn[:, p * LANES:(p + 1) * LANES]
            acc = None
            for half, (kz, vz) in enumerate(((k_lo[g], v_lo[g]), (k_hi[g], v_hi[g]))):
                h = 2 * p + half
                s = lax.dot_general(qp, kz, (((1,), (1,)), ((), ())), preferred_element_type=F32)
                s = s + bias_ref[h]
                if r == 0:
                    s = jnp.where(col >= first_lo, s, NEG_INF)
                sink = sinks_ref[h]
                m = jnp.maximum(jnp.max(s, axis=-1, keepdims=True), sink)
                e = jnp.exp(s - m)
                denom = jnp.sum(e, axis=-1, keepdims=True) + jnp.exp(sink - m)
                pv = jnp.dot(e.astype(BF16), vz, preferred_element_type=F32)
                pv = pv * (1.0 / denom)
                acc = pv if acc is None else acc + pv
            y_ref[:, p * LANES:(p + 1) * LANES] = acc

        y = y_ref[...]
        ms = jnp.mean(y * y, axis=-1, keepdims=True)
        o_ref[rows, :] = (y * lax.rsqrt(ms + EPS) * go_ref[...]).astype(BF16)


def _attention(q, kv, bias, sinks, gq_t, gk_t, g_out):
    t = q.shape[0]
    tile = ATT_R * BLOCK
    head_of_lane = jnp.arange(D_ATTN) // HEAD_DIM
    hq = (head_of_lane[:, None] == jnp.arange(LANES)[None, :]).astype(BF16)
    hk = _head_indicator(LANES)
    grid_spec = pltpu.PrefetchScalarGridSpec(
        num_scalar_prefetch=0,
        grid=(t // tile,),
        in_specs=[
            pl.BlockSpec(memory_space=pltpu.SMEM),
            pl.BlockSpec((tile, D_ATTN), lambda i: (i, 0)),
            pl.BlockSpec((BLOCK, 2 * D_KV), lambda i: (jnp.maximum(i * ATT_R - 1, 0), 0)),
            pl.BlockSpec((tile, 2 * D_KV), lambda i: (i, 0)),
            pl.BlockSpec((N_Q_HEADS, BLOCK, 2 * BLOCK), lambda i: (0, 0, 0)),
            pl.BlockSpec((1, D_ATTN), lambda i: (0, 0)),
            pl.BlockSpec((1, LANES), lambda i: (0, 0)),
            pl.BlockSpec((1, D_ATTN), lambda i: (0, 0)),
            pl.BlockSpec((D_ATTN, LANES), lambda i: (0, 0)),
            pl.BlockSpec((LANES, D_ATTN), lambda i: (0, 0)),
            pl.BlockSpec((LANES, LANES), lambda i: (0, 0)),
        ],
        out_specs=pl.BlockSpec((tile, D_ATTN), lambda i: (i, 0)),
        scratch_shapes=[pltpu.VMEM((BLOCK, D_ATTN), F32)],
    )
    return pl.pallas_call(
        _attn_kernel,
        grid_spec=grid_spec,
        out_shape=jax.ShapeDtypeStruct((t, D_ATTN), BF16),
        compiler_params=_params(("parallel",)),
        name="attn",
    )(sinks, q, kv, kv, bias, gq_t, gk_t, g_out, hq, hq.T, hk)


def _t5_bucket(dist):
    max_exact = N_BUCKETS // 2
    d = jnp.maximum(dist, 0)
    log_ratio = jnp.log(jnp.maximum(d, max_exact).astype(F32) / max_exact)
    large = max_exact + (log_ratio / math.log(MAX_DISTANCE / max_exact)
                         * (N_BUCKETS - max_exact)).astype(jnp.int32)
    large = jnp.minimum(large, N_BUCKETS - 1)
    return jnp.where(d < max_exact, d, large)


def _bias_table(rel_bias):
    q_local = jnp.arange(BLOCK, dtype=jnp.int32) + BLOCK
    k_local = jnp.arange(2 * BLOCK, dtype=jnp.int32)
    dist = q_local[:, None] - k_local[None, :]
    band = (dist >= 0) & (dist < WINDOW)
    bucket = _t5_bucket(dist)
    hit = (jnp.arange(N_BUCKETS, dtype=jnp.int32)[:, None] == bucket.reshape(1, -1)).astype(F32)
    bias = jnp.dot(rel_bias.astype(F32).T, hit, precision=lax.Precision.HIGHEST)
    return jnp.where(band[None], bias.reshape(N_Q_HEADS, BLOCK, 2 * BLOCK), NEG_INF)


def _head_indicator(n):
    i = jnp.arange(n) // HEAD_DIM
    return (i[:, None] == i[None, :]).astype(BF16)


CONV_TT = 512
CONV_RC = 64
CONV_CC = 256


def _conv_kernel(u_ref, halo_ref, w_ref, b_ref, lg_ref, lb_ref, go_ref, o_ref, h_ref, y_ref):
    step = pl.program_id(0)

    def glu(u):
        return u[:, :D_CONV] * jax.nn.sigmoid(u[:, D_CONV:])

    hh = glu(halo_ref[...])
    h_ref[0, 0:HALO, :] = jnp.where(step == 0, jnp.zeros_like(hh), hh)
    h_ref[0, HALO:, :] = glu(u_ref[...])

    n_rows = CONV_TT + HALO
    for s in range(1, SUBLANES):
        for c0 in range(0, D_CONV, CONV_CC):
            cs = slice(c0, c0 + CONV_CC)
            h_ref[s, :, cs] = pltpu.roll(h_ref[0, :, cs], n_rows - s, axis=0)

    off = HALO - (CONV_WIDTH - 1)
    for r0 in range(0, CONV_TT, CONV_RC):
        for c0 in range(0, D_CONV, CONV_CC):
            cs = slice(c0, c0 + CONV_CC)
            acc = jnp.broadcast_to(b_ref[:, cs], (CONV_RC, CONV_CC))
            for j in range(CONV_WIDTH):
                s = (r0 + off + j) % SUBLANES
                a = r0 + off + j - s
                acc = acc + w_ref[j:j + 1, cs] * h_ref[s, a:a + CONV_RC, cs]
            y_ref[r0:r0 + CONV_RC, cs] = acc

    y = y_ref[...]
    mu = jnp.mean(y, axis=-1, keepdims=True)
    yc = y - mu
    var = jnp.mean(yc * yc, axis=-1, keepdims=True)
    z = yc * lax.rsqrt(var + EPS) * lg_ref[...] + lb_ref[...]
    s = z * jax.nn.sigmoid(z)
    ms = jnp.mean(s * s, axis=-1, keepdims=True)
    o_ref[...] = (s * lax.rsqrt(ms + EPS) * go_ref[...]).astype(BF16)


def _conv(uc, w_dw, b_dw, ln_g, ln_b, g_out):
    t = uc.shape[0]
    vec = lambda: pl.BlockSpec((1, D_CONV), lambda i: (0, 0))
    per = CONV_TT // HALO
    return pl.pallas_call(
        _conv_kernel,
        grid=(t // CONV_TT,),
        in_specs=[
            pl.BlockSpec((CONV_TT, 2 * D_CONV), lambda i: (i, 0)),
            pl.BlockSpec((HALO, 2 * D_CONV), lambda i: (jnp.maximum(i * per - 1, 0), 0)),
            pl.BlockSpec((HALO, D_CONV), lambda i: (0, 0)),
            vec(), vec(), vec(), vec(),
        ],
        out_specs=pl.BlockSpec((CONV_TT, D_CONV), lambda i: (i, 0)),
        out_shape=jax.ShapeDtypeStruct((t, D_CONV), BF16),
        scratch_shapes=[pltpu.VMEM((SUBLANES, CONV_TT + HALO, D_CONV), F32),
                        pltpu.VMEM((CONV_TT, D_CONV), F32)],
        compiler_params=_params(("parallel",)),
        name="conv",
    )(uc, uc, w_dw, b_dw, ln_g, ln_b, g_out)


OUT_TM = 512
D_PACK = D_MODEL // 2
HI_MASK = 0xFFFF0000


def _pack_bf16_pairs(h):
    lo = lax.bitcast_convert_type(h[:, :D_PACK].astype(BF16).astype(F32), jnp.uint32)
    hi = lax.bitcast_convert_type(h[:, D_PACK:].astype(BF16).astype(F32), jnp.uint32)
    return (lo >> 16) | (hi & jnp.uint32(HI_MASK))


def _unpack_pairs_f32(w):
    lo = lax.bitcast_convert_type(w << 16, F32)
    hi = lax.bitcast_convert_type(w & jnp.uint32(HI_MASK), F32)
    return lo, hi


def _unpack_bf16_pairs(w):
    lo, hi = _unpack_pairs_f32(w)
    return lo.astype(BF16), hi.astype(BF16)


def _out_kernel(ma_ref, mc_ref, x_ref, w_ref, bo_ref, g1_ref, g2_ref, sc_ref, sh_ref, wr_ref, br_ref,
                tri_ref, xm_ref, h2_ref, idx_ref, rank_ref, wt_ref, cnt_ref, carry_ref):
    step = pl.program_id(0)

    @pl.when(step == 0)
    def _():
        carry_ref[...] = jnp.zeros_like(carry_ref)

    y = (jnp.dot(ma_ref[...], w_ref[0:D_ATTN, :], preferred_element_type=F32)
         + jnp.dot(mc_ref[...], w_ref[D_ATTN:, :], preferred_element_type=F32) + bo_ref[...])
    xm = x_ref[...] + g1_ref[...] * y
    xm_ref[...] = xm
    h2 = _modulated_rms(xm, g2_ref[...], sc_ref[...], sh_ref[...])
    h2_ref[...] = _pack_bf16_pairs(h2)

    both = jnp.dot(h2.astype(BF16), wr_ref[...], preferred_element_type=F32)
    logits = both[:, :N_EXPERTS] + both[:, N_EXPERTS:] + br_ref[...]
    tm = logits.shape[0]
    lane = lax.broadcasted_iota(jnp.int32, (tm, N_EXPERTS), 1).astype(F32)
    vals, idxs = [], []
    l = logits
    for _ in range(TOP_K):
        m = jnp.max(l, axis=-1, keepdims=True)
        i = jnp.min(jnp.where(l == m, lane, float(N_EXPERTS)), axis=-1, keepdims=True)
        vals.append(m)
        idxs.append(i)
        l = jnp.where(lane == i, -jnp.inf, l)
    es = [jnp.exp(v - vals[0]) for v in vals]
    tot = es[0] + es[1] + es[2] + es[3]
    ws = [e / tot for e in es]

    hot = [(lane == i).astype(F32) for i in idxs]
    hot_all = hot[0] + hot[1] + hot[2] + hot[3]
    before = jnp.dot(tri_ref[...], hot_all.astype(BF16), preferred_element_type=F32) + carry_ref[...]
    ranks = [jnp.sum(h * before, axis=-1, keepdims=True) for h in hot]
    carry_ref[...] = carry_ref[...] + jnp.sum(hot_all, axis=0, keepdims=True)
    cnt_ref[...] = carry_ref[...].astype(jnp.int32)

    slot = lax.broadcasted_iota(jnp.int32, (tm, LANES), 1)

    def pack(cols):
        out = jnp.zeros((tm, LANES), F32)
        for k in range(TOP_K):
            out = jnp.where(slot == k, cols[k], out)
        return out

    idx_ref[...] = pack(idxs).astype(jnp.int32)
    rank_ref[...] = pack(ranks).astype(jnp.int32)
    wt_ref[...] = pack(ws)


def _out_proj(ma, mc, x2, w_out_bf, b_out, gate1, g2, scale2, shift2, w_router, b_router):
    t = x2.shape[0]
    vec = lambda n: pl.BlockSpec((1, n), lambda i: (0, 0))
    tri = jnp.tril(jnp.ones((OUT_TM, OUT_TM), F32), -1).astype(BF16)
    wr_hi = w_router.astype(BF16)
    wr_lo = (w_router - wr_hi.astype(F32)).astype(BF16)
    w_router = jnp.concatenate([wr_hi, wr_lo], axis=1)
    return pl.pallas_call(
        _out_kernel,
        grid=(t // OUT_TM,),
        in_specs=[
            pl.BlockSpec((OUT_TM, D_ATTN), lambda i: (i, 0)),
            pl.BlockSpec((OUT_TM, D_CONV), lambda i: (i, 0)),
            pl.BlockSpec((OUT_TM, D_MODEL), lambda i: (i, 0)),
            pl.BlockSpec((D_MODEL, D_MODEL), lambda i: (0, 0)),
            vec(D_MODEL), vec(D_MODEL), vec(D_MODEL), vec(D_MODEL), vec(D_MODEL),
            pl.BlockSpec((D_MODEL, 2 * N_EXPERTS), lambda i: (0, 0)),
            vec(N_EXPERTS),
            pl.BlockSpec((OUT_TM, OUT_TM), lambda i: (0, 0)),
        ],
        out_specs=[
            pl.BlockSpec((OUT_TM, D_MODEL), lambda i: (i, 0)),
            pl.BlockSpec((OUT_TM, D_PACK), lambda i: (i, 0)),
            pl.BlockSpec((OUT_TM, LANES), lambda i: (i, 0)),
            pl.BlockSpec((OUT_TM, LANES), lambda i: (i, 0)),
            pl.BlockSpec((OUT_TM, LANES), lambda i: (i, 0)),
            pl.BlockSpec((1, N_EXPERTS), lambda i: (0, 0)),
        ],
        out_shape=[
            jax.ShapeDtypeStruct((t, D_MODEL), F32),
            jax.ShapeDtypeStruct((t, D_PACK), jnp.uint32),
            jax.ShapeDtypeStruct((t, LANES), jnp.int32),
            jax.ShapeDtypeStruct((t, LANES), jnp.int32),
            jax.ShapeDtypeStruct((t, LANES), F32),
            jax.ShapeDtypeStruct((1, N_EXPERTS), jnp.int32),
        ],
        scratch_shapes=[pltpu.VMEM((1, N_EXPERTS), F32)],
        compiler_params=_params(("arbitrary",)),
        name="out_proj",
    )(ma, mc, x2, w_out_bf, b_out, gate1, g2, scale2, shift2, w_router, b_router, tri)


DISP_TD = 512
DISP_NBUF = 3


class _CopyGroup:
    def __init__(self, copies):
        self.copies = copies

    def start(self):
        for cp in self.copies:
            cp.start()

    def wait(self):
        for cp in self.copies:
            cp.wait()

ZERO_RUN = 64


def _dispatch_kernel(dest_ref, pad_lo_ref, pad_hi_ref, nu_ref, h2_ref, xs_ref,
                     stage_ref, zero_ref, in_sems, out_sems, zsem):
    step = pl.program_id(0)
    last = pl.num_programs(0) - 1
    slot = lax.rem(step, DISP_NBUF)
    rows_per_chunk = DISP_TD * TOP_K

    def load(chunk, s):
        rows = pl.ds(pl.multiple_of(chunk * DISP_TD, DISP_TD), DISP_TD)
        return _CopyGroup([pltpu.make_async_copy(h2_ref.at[rows, pl.ds(j * LANES, LANES)], stage_ref.at[s, :, j, :],
                                                 in_sems.at[s]) for j in range(stage_ref.shape[2])])

    def wait_chunk(s):
        whole = xs_ref.at[pl.ds(0, rows_per_chunk)]
        pltpu.make_async_copy(whole, whole, out_sems.at[s]).wait()

    @pl.when(step == 0)
    def _():
        load(0, 0).start()

        @pl.when(last >= 1)
        def _():
            load(1, 1).start()

    load(step, slot).wait()

    def issue(r, carry):
        for k in range(TOP_K):
            d = dest_ref[(step * DISP_TD + r) * TOP_K + k]
            pltpu.make_async_copy(stage_ref.at[slot, r], xs_ref.at[d], out_sems.at[slot]).start(priority=k % 2)
        return carry

    lax.fori_loop(0, DISP_TD, issue, 0, unroll=2)

    @pl.when(step == 0)
    def _():
        zero_ref[...] = jnp.zeros_like(zero_ref)

        def zero_rows(d, n):
            d = d if n == 1 else pl.multiple_of(d, n)
            return pltpu.make_async_copy(zero_ref.at[pl.ds(0, n)], xs_ref.at[pl.ds(d, n)], zsem)

        def ranges(e):
            lo, hi = pad_lo_ref[e], pad_hi_ref[e]
            a = jnp.minimum((lo + SUBLANES - 1) // SUBLANES * SUBLANES, hi)
            b = jnp.minimum((lo + ZERO_RUN - 1) // ZERO_RUN * ZERO_RUN, hi)
            return ((lo, a - lo, 1), (a, (b - a) // SUBLANES, SUBLANES), (b, (hi - b) // ZERO_RUN, ZERO_RUN))

        def for_all_runs(act):
            def per_expert(e, carry):
                for start, count, n in ranges(e):
                    lax.fori_loop(0, count, lambda i, c, s=start, n=n: (act(zero_rows(s + i * n, n)), c)[1], 0)
                return carry

            lax.fori_loop(0, N_EXPERTS, per_expert, 0)
            n_blocks = xs_ref.shape[0] // EXPERT_TM
            lax.fori_loop(nu_ref[0], n_blocks,
                          lambda blk, c: (act(zero_rows(blk * EXPERT_TM, EXPERT_TM)), c)[1], 0)

        for_all_runs(lambda copy: copy.start())
        for_all_runs(lambda copy: copy.wait())

    @pl.when(step > 0)
    def _():
        wait_chunk(lax.rem(step + DISP_NBUF - 1, DISP_NBUF))

    @pl.when(step + 2 <= last)
    def _():
        load(step + 2, lax.rem(step + 2, DISP_NBUF)).start()

    @pl.when(step == last)
    def _():
        wait_chunk(slot)


def _dispatch(h2p, dest_flat, pad_lo, pad_hi, n_used, n_rows):
    t = h2p.shape[0]
    grid_spec = pltpu.PrefetchScalarGridSpec(
        num_scalar_prefetch=4,
        grid=(t // DISP_TD,),
        in_specs=[pl.BlockSpec(memory_space=pl.ANY)],
        out_specs=pl.BlockSpec(memory_space=pl.ANY),
        scratch_shapes=[pltpu.VMEM((DISP_NBUF, DISP_TD, D_PACK // LANES, LANES), jnp.uint32),
                        pltpu.VMEM((EXPERT_TM, D_PACK // LANES, LANES), jnp.uint32),
                        pltpu.SemaphoreType.DMA((DISP_NBUF,)), pltpu.SemaphoreType.DMA((DISP_NBUF,)),
                        pltpu.SemaphoreType.DMA(())],
    )
    return pl.pallas_call(
        _dispatch_kernel,
        grid_spec=grid_spec,
        out_shape=jax.ShapeDtypeStruct((n_rows, D_PACK // LANES, LANES), jnp.uint32),
        compiler_params=_params(("arbitrary",)),
        name="dispatch",
    )(dest_flat, pad_lo, pad_hi, n_used, h2p)


UP_TF = 1024
DOWN_TN = 2048
UP_AHEAD = 2


def _weight_index_map(n_chunks):
    def index_map(c, b, be, nu, first, nxt, full):
        del nu, full
        in_last = nxt[b] < 0
        wrap = jnp.logical_and(in_last, c + 1 < n_chunks)
        e_next = jnp.where(in_last, jnp.where(wrap, be[0], be[b]), nxt[b])
        c_next = jnp.where(wrap, c + 1, c)
        is_first = first[b] == 1
        return jnp.where(is_first, be[b], e_next), 0, jnp.where(is_first, c, c_next)

    return index_map


def _row_cases(b, nu_ref, full_ref, o_ref, compute):
    half = EXPERT_TM // 2
    used = b < nu_ref[0]

    @pl.when(jnp.logical_and(used, full_ref[b] == 1))
    def _():
        compute(EXPERT_TM)

    @pl.when(jnp.logical_and(used, full_ref[b] == 0))
    def _():
        compute(half)
        o_ref[half:, :] = jnp.zeros((half, o_ref.shape[1]), o_ref.dtype)

    @pl.when(jnp.logical_not(used))
    def _():
        o_ref[...] = jnp.zeros(o_ref.shape, o_ref.dtype)


def _up_kernel(be_ref, nu_ref, first_ref, nxt_ref, full_ref, xs_ref, wg_ref, wu_ref, bg_ref, bu_ref, h_ref,
               wg_bf, wu_bf, xbuf_ref, xsems):
    f = pl.program_id(0)
    b = pl.program_id(1)
    nb = pl.num_programs(1)
    step = f * nb + b
    n_steps = pl.num_programs(0) * nb
    n_buf = UP_AHEAD + 1
    slot = lax.rem(step, n_buf)
    n_sub = xs_ref.shape[1]

    def fetch(ahead):
        blk = lax.rem(b + ahead, nb)
        rows = pl.ds(pl.multiple_of(jnp.minimum(blk, nu_ref[0] - 1) * EXPERT_TM, EXPERT_TM), EXPERT_TM)
        s = lax.rem(step + ahead, n_buf)
        return _CopyGroup([pltpu.make_async_copy(xs_ref.at[rows, j, :], xbuf_ref.at[s, :, pl.ds(j * LANES, LANES)],
                                                 xsems.at[s]) for j in range(n_sub)])

    @pl.when(step == 0)
    def _():
        for ahead in range(UP_AHEAD):
            @pl.when(ahead < n_steps)
            def _():
                fetch(ahead).start()

    @pl.when(step + UP_AHEAD < n_steps)
    def _():
        fetch(UP_AHEAD).start()

    fetch(0).wait()
    x_ref = xbuf_ref.at[slot]

    @pl.when(first_ref[b] == 1)
    def _():
        wg_bf[...] = wg_ref[0].astype(BF16)
        wu_bf[...] = wu_ref[0].astype(BF16)

    def compute(m):
        x_lo, x_hi = _unpack_bf16_pairs(x_ref[0:m, :])

        def proj(w_bf, bias_ref):
            return (jnp.dot(x_lo, w_bf[0:D_PACK, :], preferred_element_type=F32)
                    + jnp.dot(x_hi, w_bf[D_PACK:, :], preferred_element_type=F32) + bias_ref[0])

        g = proj(wg_bf, bg_ref)
        lin = proj(wu_bf, bu_ref)
        g = jnp.minimum(g, SWIGLU_LIMIT)
        lin = jnp.clip(lin, -SWIGLU_LIMIT, SWIGLU_LIMIT)
        act = g * jax.nn.sigmoid(SWIGLU_ALPHA * g) * (lin + 1.0)
        h_ref[0:m, :] = act.astype(BF16)

    _row_cases(b, nu_ref, full_ref, h_ref, compute)


def _up(xs, tables, w_gate, b_gate, w_up, b_up):
    n_rows = xs.shape[0]
    nb = n_rows // EXPERT_TM
    n_chunks = D_FF // UP_TF
    wsel = _weight_index_map(n_chunks)
    bsel = lambda f, b, be, nu, first, nxt, full: (be[b], 0, f)
    grid_spec = pltpu.PrefetchScalarGridSpec(
        num_scalar_prefetch=5,
        grid=(n_chunks, nb),
        in_specs=[
            pl.BlockSpec(memory_space=pl.ANY),
            pl.BlockSpec((1, D_MODEL, UP_TF), wsel),
            pl.BlockSpec((1, D_MODEL, UP_TF), wsel),
            pl.BlockSpec((1, 1, UP_TF), bsel),
            pl.BlockSpec((1, 1, UP_TF), bsel),
        ],
        out_specs=pl.BlockSpec((EXPERT_TM, UP_TF), lambda f, b, be, nu, first, nxt, full: (b, f)),
        scratch_shapes=[pltpu.VMEM((D_MODEL, UP_TF), BF16), pltpu.VMEM((D_MODEL, UP_TF), BF16),
                        pltpu.VMEM((UP_AHEAD + 1, EXPERT_TM, D_PACK), jnp.uint32),
                        pltpu.SemaphoreType.DMA((UP_AHEAD + 1,))],
    )
    return pl.pallas_call(
        _up_kernel,
        grid_spec=grid_spec,
        out_shape=jax.ShapeDtypeStruct((n_rows, D_FF), BF16),
        compiler_params=_params(("arbitrary", "arbitrary"), EXPERT_VMEM_LIMIT),
        name="expert_up",
    )(*tables, xs, w_gate, w_up,
      b_gate.reshape(N_EXPERTS, 1, D_FF), b_up.reshape(N_EXPERTS, 1, D_FF))


def _down_kernel(be_ref, nu_ref, first_ref, nxt_ref, full_ref, h_ref, wd_ref, bd_ref, ys_ref, wd_bf,
                 ybuf_ref, ysems):
    b = pl.program_id(1)
    last = pl.num_programs(1) - 1
    slot = b % 2

    def write_back(blk, s):
        rows = pl.ds(pl.multiple_of(blk * EXPERT_TM, EXPERT_TM), EXPERT_TM)
        return _CopyGroup([pltpu.make_async_copy(ybuf_ref.at[s, :, pl.ds(j * LANES, LANES)], ys_ref.at[rows, j, :],
                                                 ysems.at[s]) for j in range(ys_ref.shape[1])])

    @pl.when(b >= 2)
    def _():
        write_back(b - 2, slot).wait()

    @pl.when(first_ref[b] == 1)
    def _():
        wd_bf[...] = wd_ref[0].astype(BF16)

    y_ref = ybuf_ref.at[slot]

    def compute(m):
        y = jnp.dot(h_ref[0:m, :], wd_bf[...], preferred_element_type=F32) + bd_ref[0]
        y_ref[0:m, :] = _pack_bf16_pairs(y)

    _row_cases(b, nu_ref, full_ref, y_ref, compute)
    write_back(b, slot).start()

    @pl.when(b == last)
    def _():
        @pl.when(b >= 1)
        def _():
            write_back(b - 1, 1 - slot).wait()

        write_back(b, slot).wait()


def _down(hs, tables, w_down, b_down):
    n_rows = hs.shape[0]
    nb = n_rows // EXPERT_TM
    n_chunks = D_MODEL // DOWN_TN
    assert n_chunks == 1
    row = lambda n, b, be, nu, first, nxt, full: (jnp.minimum(b, nu[0] - 1), 0)
    grid_spec = pltpu.PrefetchScalarGridSpec(
        num_scalar_prefetch=5,
        grid=(n_chunks, nb),
        in_specs=[
            pl.BlockSpec((EXPERT_TM, D_FF), row),
            pl.BlockSpec((1, D_FF, DOWN_TN), _weight_index_map(n_chunks)),
            pl.BlockSpec((1, 1, DOWN_TN), lambda n, b, be, nu, first, nxt, full: (be[b], 0, n)),
        ],
        out_specs=pl.BlockSpec(memory_space=pl.ANY),
        scratch_shapes=[pltpu.VMEM((D_FF, DOWN_TN), BF16),
                        pltpu.VMEM((2, EXPERT_TM, D_PACK), jnp.uint32), pltpu.SemaphoreType.DMA((2,))],
    )
    return pl.pallas_call(
        _down_kernel,
        grid_spec=grid_spec,
        out_shape=jax.ShapeDtypeStruct((n_rows, D_PACK // LANES, LANES), jnp.uint32),
        compiler_params=_params(("arbitrary", "arbitrary"), EXPERT_VMEM_LIMIT),
        name="expert_down",
    )(*tables, hs, w_down, b_down.reshape(N_EXPERTS, 1, D_MODEL))


COMB_TC = 256


def _combine_kernel(dest_ref, wsm_ref, xm_ref, g2_ref, ys_ref, o_ref, buf_ref, acc_ref, sems):
    step = pl.program_id(0)
    last = pl.num_programs(0) - 1
    slot = step % 2

    def gather_tile(tile, s):
        def issue(r, carry):
            for k in range(TOP_K):
                d = dest_ref[(tile * COMB_TC + r) * TOP_K + k]
                pltpu.make_async_copy(ys_ref.at[d], buf_ref.at[s, k * COMB_TC + r],
                                      sems.at[s]).start(priority=k % 2)
            return carry

        lax.fori_loop(0, COMB_TC, issue, 0, unroll=2)

    @pl.when(step == 0)
    def _():
        gather_tile(0, 0)

    @pl.when(step < last)
    def _():
        gather_tile(step + 1, 1 - slot)

    pltpu.make_async_copy(ys_ref.at[pl.ds(0, TOP_K * COMB_TC)], buf_ref.at[slot], sems.at[slot]).wait()

    def reduce_token(r, carry):
        acc_lo = acc_hi = None
        for k in range(TOP_K):
            w = wsm_ref[(step * COMB_TC + r) * TOP_K + k]
            lo, hi = _unpack_pairs_f32(buf_ref[slot, k * COMB_TC + r])
            acc_lo = w * lo if acc_lo is None else acc_lo + w * lo
            acc_hi = w * hi if acc_hi is None else acc_hi + w * hi
        acc_ref[0, r] = acc_lo
        acc_ref[1, r] = acc_hi
        return carry

    lax.fori_loop(0, COMB_TC, reduce_token, 0, unroll=4)

    for half in range(2):
        for j in range(D_PACK // LANES):
            cols = slice(half * D_PACK + j * LANES, half * D_PACK + (j + 1) * LANES)
            o_ref[:, cols] = xm_ref[:, cols] + g2_ref[:, cols] * acc_ref[half, :, j, :]


def _combine(ys, dest_flat, xm, wts_flat, gate2):
    t = xm.shape[0]
    n_sub = D_PACK // LANES
    grid_spec = pltpu.PrefetchScalarGridSpec(
        num_scalar_prefetch=2,
        grid=(t // COMB_TC,),
        in_specs=[
            pl.BlockSpec((COMB_TC, D_MODEL), lambda i, d, w: (i, 0)),
            pl.BlockSpec((1, D_MODEL), lambda i, d, w: (0, 0)),
            pl.BlockSpec(memory_space=pl.ANY),
        ],
        out_specs=pl.BlockSpec((COMB_TC, D_MODEL), lambda i, d, w: (i, 0)),
        scratch_shapes=[pltpu.VMEM((2, TOP_K * COMB_TC, n_sub, LANES), jnp.uint32),
                        pltpu.VMEM((2, COMB_TC, n_sub, LANES), F32),
                        pltpu.SemaphoreType.DMA((2,))],
    )
    return pl.pallas_call(
        _combine_kernel,
        grid_spec=grid_spec,
        out_shape=jax.ShapeDtypeStruct((t, D_MODEL), F32),
        compiler_params=_params(("arbitrary",)),
        name="combine",
    )(dest_flat, wts_flat, xm, gate2, ys)


def _routing_tables(idx, rank, counts, n_tok):
    padded = ((counts + EXPERT_TM - 1) // EXPERT_TM) * EXPERT_TM
    pend = jnp.cumsum(padded)
    pstart = pend - padded
    experts = jnp.arange(N_EXPERTS, dtype=jnp.int32)
    start_of = jnp.sum(jnp.where(idx[..., None] == experts, pstart, 0), axis=-1)
    dest = (start_of + rank).astype(jnp.int32).reshape(-1)
    n_rows = n_tok * TOP_K + N_EXPERTS * EXPERT_TM
    nb = n_rows // EXPERT_TM
    block_start = jnp.arange(nb, dtype=jnp.int32) * EXPERT_TM
    block_expert = jnp.minimum(jnp.sum(pend[None, :] <= block_start[:, None], axis=1),
                               N_EXPERTS - 1).astype(jnp.int32)
    n_used = (pend[-1:] // EXPERT_TM).astype(jnp.int32)
    pad_lo = (pstart + counts).astype(jnp.int32)
    pad_hi = pend.astype(jnp.int32)

    blocks = jnp.arange(nb, dtype=jnp.int32)
    block_expert = jnp.where(blocks < n_used[0], block_expert, block_expert[jnp.maximum(n_used[0] - 1, 0)])
    prev = jnp.concatenate([block_expert[:1], block_expert[:-1]])
    first = jnp.logical_or(blocks == 0, block_expert != prev)
    first_pos = jnp.where(first, blocks, nb)
    next_first = jnp.flip(lax.cummin(jnp.flip(jnp.concatenate([first_pos[1:], jnp.full((1,), nb, jnp.int32)]))))
    nxt = jnp.where(next_first < nb, block_expert[jnp.minimum(next_first, nb - 1)], -1).astype(jnp.int32)
    rows_end = jnp.sum(jnp.where(block_expert[:, None] == experts, pad_lo, 0), axis=-1)
    full = (rows_end > block_start + EXPERT_TM // 2).astype(jnp.int32)
    tables = (block_expert, n_used, first.astype(jnp.int32), nxt, full)
    return dest, tables, pad_lo, pad_hi, n_rows


def _layer(x2, c, w_ada, b_ada, g_norm1, w_in, b_in, g_q, g_k, sinks, rel_bias, w_dw, b_dw, ln_g, ln_b,
           g_out_attn, g_out_conv, w_out, b_out, g_norm2, w_router, b_router,
           w_gate, b_gate, w_up, b_up, w_down, b_down):
    t = x2.shape[0]
    row = lambda v: v.reshape(1, -1)
    mod = _ada(c, w_ada, b_ada)
    shift1, scale1, gate1, shift2, scale2, gate2 = [mod[:, i * D_MODEL:(i + 1) * D_MODEL] for i in range(6)]

    q, kv, uc = _in_proj(x2, row(g_norm1), scale1, shift1, w_in.astype(BF16), row(b_in))

    gq_t = row(jnp.tile(g_q, N_Q_HEADS)) * (HEAD_DIM ** -0.5)
    gk_t = row(jnp.tile(g_k, N_KV_HEADS))
    ma = _attention(q, kv, _bias_table(rel_bias), sinks, gq_t, gk_t, row(g_out_attn))
    w_dw_p = jnp.concatenate([w_dw, jnp.zeros((HALO - CONV_WIDTH, D_CONV), w_dw.dtype)], axis=0)
    mc = _conv(uc, w_dw_p, row(b_dw), row(ln_g), row(ln_b), row(g_out_conv))

    xm, h2, idx, rank, wts, counts = _out_proj(
        ma, mc, x2, w_out.astype(BF16), row(b_out), gate1, row(g_norm2), scale2, shift2,
        w_router, row(b_router))

    dest, tables, pad_lo, pad_hi, n_rows = _routing_tables(idx[:, :TOP_K], rank[:, :TOP_K], counts[0], t)
    xs = _dispatch(h2, dest, pad_lo, pad_hi, tables[1], n_rows)
    hs = _up(xs, tables, w_gate, b_gate, w_up, b_up)
    ys = _down(hs, tables, w_down, b_down)
    return _combine(ys, dest, xm, wts[:, :TOP_K].reshape(-1), gate2)


def kernel(x, c, w_ada, b_ada, g_norm1, w_in, b_in, g_q, g_k, sinks, rel_bias, w_dw, b_dw, ln_g, ln_b,
           g_out_attn, g_out_conv, w_out, b_out, g_norm2, w_router, b_router,
           w_gate, b_gate, w_up, b_up, w_down, b_down):
    b, t, d = x.shape
    assert b == 1 and d == D_MODEL and w_ada.shape[0] == 1
    out = _layer(x.reshape(t, d), c, w_ada[0], b_ada[0], g_norm1[0], w_in[0], b_in[0], g_q[0], g_k[0],
                 sinks[0], rel_bias, w_dw[0], b_dw[0], ln_g[0], ln_b[0], g_out_attn[0], g_out_conv[0],
                 w_out[0], b_out[0], g_norm2[0], w_router[0], b_router[0],
                 w_gate[0], b_gate[0], w_up[0], b_up[0], w_down[0], b_down[0])
    return out.reshape(b, t, d)
```

```python
import math

import jax
import jax.numpy as jnp
from jax import lax
from jax.experimental import pallas as pl
from jax.experimental.pallas import tpu as pltpu

D_MODEL = 2048
HEAD_DIM = 64
N_Q_HEADS = 16
N_KV_HEADS = 2
D_ATTN = N_Q_HEADS * HEAD_DIM
D_KV = N_KV_HEADS * HEAD_DIM
D_CONV = D_MODEL - D_ATTN
D_IN = D_ATTN + 2 * D_KV + 2 * D_CONV
WINDOW = 128
BLOCK = 128
CONV_WIDTH = 31
N_BUCKETS = 32
MAX_DISTANCE = 128
N_EXPERTS = 32
TOP_K = 4
D_FF = D_MODEL
SWIGLU_LIMIT = 7.0
SWIGLU_ALPHA = 1.702
EPS = 1e-6
NEG_INF = -1e30

LANES = 128
SUBLANES = 8
VMEM_LIMIT = 56 * 1024 * 1024

HALO = 32
EXPERT_TM = 512
EXPERT_VMEM_LIMIT = 60 * 1024 * 1024
F32 = jnp.float32
BF16 = jnp.bfloat16


def _params(sem, vmem_limit=VMEM_LIMIT):
    return pltpu.CompilerParams(dimension_semantics=sem, vmem_limit_bytes=vmem_limit)


ADA_TN = 1024
ADA_RC = 256


def _ada_kernel(c_ref, w_ref, b_ref, o_ref):
    tn = w_ref.shape[1]

    def body(i, acc):
        r = pl.multiple_of(i * ADA_RC, ADA_RC)
        c = c_ref[pl.ds(r, ADA_RC), :]
        sc = c * jax.nn.sigmoid(c)
        prod = w_ref[pl.ds(r, ADA_RC), :] * sc
        return acc + jnp.sum(prod.reshape(ADA_RC // 8, 8, tn), axis=0)

    acc = lax.fori_loop(0, D_MODEL // ADA_RC, body, jnp.zeros((8, tn), F32))
    o_ref[...] = jnp.sum(acc, axis=0, keepdims=True) + b_ref[...]


def _ada(c, w_ada, b_ada):
    n = w_ada.shape[1]
    return pl.pallas_call(
        _ada_kernel,
        grid=(n // ADA_TN,),
        in_specs=[
            pl.BlockSpec((D_MODEL, 1), lambda j: (0, 0)),
            pl.BlockSpec((D_MODEL, ADA_TN), lambda j: (0, j)),
            pl.BlockSpec((1, ADA_TN), lambda j: (0, j)),
        ],
        out_specs=pl.BlockSpec((1, ADA_TN), lambda j: (0, j)),
        out_shape=jax.ShapeDtypeStruct((1, n), F32),
        compiler_params=_params(("parallel",)),
        name="ada",
    )(c.reshape(D_MODEL, 1), w_ada, b_ada.reshape(1, n))


IN_TM = 512
IN_NC = 256


def _modulated_rms(x, g, scale, shift):
    ms = jnp.mean(x * x, axis=-1, keepdims=True)
    return (x * lax.rsqrt(ms + EPS) * g) * (1.0 + scale) + shift


def _in_kernel(x_ref, g_ref, sc_ref, sh_ref, w_ref, b_ref, q_ref, kv_ref, uc_ref):
    h = _modulated_rms(x_ref[...], g_ref[...], sc_ref[...], sh_ref[...]).astype(BF16)

    def proj(lo, n, o_ref):
        for c in range(0, n, IN_NC):
            u = jnp.dot(h, w_ref[:, lo + c:lo + c + IN_NC], preferred_element_type=F32)
            o_ref[:, c:c + IN_NC] = u + b_ref[:, lo + c:lo + c + IN_NC]

    proj(0, D_ATTN, q_ref)
    proj(D_ATTN, 2 * D_KV, kv_ref)
    proj(D_ATTN + 2 * D_KV, 2 * D_CONV, uc_ref)


def _in_proj(x2, g1, scale1, shift1, w_in_bf, b_in):
    t = x2.shape[0]
    vec = lambda n: pl.BlockSpec((1, n), lambda i: (0, 0))
    return pl.pallas_call(
        _in_kernel,
        grid=(t // IN_TM,),
        in_specs=[
            pl.BlockSpec((IN_TM, D_MODEL), lambda i: (i, 0)),
            vec(D_MODEL), vec(D_MODEL), vec(D_MODEL),
            pl.BlockSpec((D_MODEL, D_IN), lambda i: (0, 0)),
            vec(D_IN),
        ],
        out_specs=[
            pl.BlockSpec((IN_TM, D_ATTN), lambda i: (i, 0)),
            pl.BlockSpec((IN_TM, 2 * D_KV), lambda i: (i, 0)),
            pl.BlockSpec((IN_TM, 2 * D_CONV), lambda i: (i, 0)),
        ],
        out_shape=[
            jax.ShapeDtypeStruct((t, D_ATTN), F32),
            jax.ShapeDtypeStruct((t, 2 * D_KV), F32),
            jax.ShapeDtypeStruct((t, 2 * D_CONV), F32),
        ],
        compiler_params=_params(("parallel",)),
        name="in_proj",
    )(x2, g1, scale1, shift1, w_in_bf, b_in)


ATT_R = 4


def _split_dot(a, b_bf):
    hi = a.astype(BF16)
    lo = (a - hi.astype(F32)).astype(BF16)
    return (jnp.dot(hi, b_bf, preferred_element_type=F32)
            + jnp.dot(lo, b_bf, preferred_element_type=F32))


def _attn_kernel(sinks_ref, q_ref, kvp_ref, kvc_ref, bias_ref, gq_ref, gk_ref, go_ref,
                 hq_ref, hqt_ref, hk_ref, o_ref, y_ref):
    step = pl.program_id(0)
    lane = lax.broadcasted_iota(jnp.int32, (2 * BLOCK, LANES), 1)
    low = lane < HEAD_DIM
    col = lax.broadcasted_iota(jnp.int32, (BLOCK, 2 * BLOCK), 1)
    first_lo = jnp.where(step == 0, BLOCK, 0)

    for r in range(ATT_R):
        rows = slice(r * BLOCK, (r + 1) * BLOCK)
        q = q_ref[rows, :]
        ssq = _split_dot(_split_dot(q * q, hq_ref[...]), hqt_ref[...])
        qn = (q * lax.rsqrt(ssq * (1.0 / HEAD_DIM) + EPS) * gq_ref[...]).astype(BF16)

        if r == 0:
            kv_prev = kvp_ref[...]
        else:
            kv_prev = kvc_ref[(r - 1) * BLOCK:r * BLOCK, :]
        kv = jnp.concatenate([kv_prev, kvc_ref[rows, :]], axis=0)
        k = kv[:, :LANES]
        v = kv[:, LANES:]
        kss = _split_dot(k * k, hk_ref[...])
        kn = k * lax.rsqrt(kss * (1.0 / HEAD_DIM) + EPS) * gk_ref[...]
        kn_sw = pltpu.roll(kn, HEAD_DIM, axis=1)
        v_sw = pltpu.roll(v, HEAD_DIM, axis=1)
        zero = jnp.zeros_like(kn)
        k_lo = [jnp.where(low, kn, zero).astype(BF16), jnp.where(low, kn_sw, zero).astype(BF16)]
        k_hi = [jnp.where(low, zero, kn_sw).astype(BF16), jnp.where(low, zero, kn).astype(BF16)]
        v_lo = [jnp.where(low, v, zero).astype(BF16), jnp.where(low, v_sw, zero).astype(BF16)]
        v_hi = [jnp.where(low, zero, v_sw).astype(BF16), jnp.where(low, zero, v).astype(BF16)]

        for p in range(N_Q_HEADS // 2):
            g = (2 * p) // (N_Q_HEADS // N_KV_HEADS)
            qp = qn[:, p * LANES:(p + 1) * LANES]
            acc = None
            for half, (kz, vz) in enumerate(((k_lo[g], v_lo[g]), (k_hi[g], v_hi[g]))):
                h = 2 * p + half
                s = lax.dot_general(qp, kz, (((1,), (1,)), ((), ())), preferred_element_type=F32)
                s = s + bias_ref[h]
                if r == 0:
                    s = jnp.where(col >= first_lo, s, NEG_INF)
                sink = sinks_ref[h]
                m = jnp.maximum(jnp.max(s, axis=-1, keepdims=True), sink)
                e = jnp.exp(s - m)
                denom = jnp.sum(e, axis=-1, keepdims=True) + jnp.exp(sink - m)
                pv = jnp.dot(e.astype(BF16), vz, preferred_element_type=F32)
                pv = pv * (1.0 / denom)
                acc = pv if acc is None else acc + pv
            y_ref[:, p * LANES:(p + 1) * LANES] = acc

        y = y_ref[...]
        ms = jnp.mean(y * y, axis=-1, keepdims=True)
        o_ref[rows, :] = (y * lax.rsqrt(ms + EPS) * go_ref[...]).astype(BF16)


def _attention(q, kv, bias, sinks, gq_t, gk_t, g_out):
    t = q.shape[0]
    tile = ATT_R * BLOCK
    head_of_lane = jnp.arange(D_ATTN) // HEAD_DIM
    hq = (head_of_lane[:, None] == jnp.arange(LANES)[None, :]).astype(BF16)
    hk = _head_indicator(LANES)
    grid_spec = pltpu.PrefetchScalarGridSpec(
        num_scalar_prefetch=0,
        grid=(t // tile,),
        in_specs=[
            pl.BlockSpec(memory_space=pltpu.SMEM),
            pl.BlockSpec((tile, D_ATTN), lambda i: (i, 0)),
            pl.BlockSpec((BLOCK, 2 * D_KV), lambda i: (jnp.maximum(i * ATT_R - 1, 0), 0)),
            pl.BlockSpec((tile, 2 * D_KV), lambda i: (i, 0)),
            pl.BlockSpec((N_Q_HEADS, BLOCK, 2 * BLOCK), lambda i: (0, 0, 0)),
            pl.BlockSpec((1, D_ATTN), lambda i: (0, 0)),
            pl.BlockSpec((1, LANES), lambda i: (0, 0)),
            pl.BlockSpec((1, D_ATTN), lambda i: (0, 0)),
            pl.BlockSpec((D_ATTN, LANES), lambda i: (0, 0)),
            pl.BlockSpec((LANES, D_ATTN), lambda i: (0, 0)),
            pl.BlockSpec((LANES, LANES), lambda i: (0, 0)),
        ],
        out_specs=pl.BlockSpec((tile, D_ATTN), lambda i: (i, 0)),
        scratch_shapes=[pltpu.VMEM((BLOCK, D_ATTN), F32)],
    )
    return pl.pallas_call(
        _attn_kernel,
        grid_spec=grid_spec,
        out_shape=jax.ShapeDtypeStruct((t, D_ATTN), BF16),
        compiler_params=_params(("parallel",)),
        name="attn",
    )(sinks, q, kv, kv, bias, gq_t, gk_t, g_out, hq, hq.T, hk)


def _t5_bucket(dist):
    max_exact = N_BUCKETS // 2
    d = jnp.maximum(dist, 0)
    log_ratio = jnp.log(jnp.maximum(d, max_exact).astype(F32) / max_exact)
    large = max_exact + (log_ratio / math.log(MAX_DISTANCE / max_exact)
                         * (N_BUCKETS - max_exact)).astype(jnp.int32)
    large = jnp.minimum(large, N_BUCKETS - 1)
    return jnp.where(d < max_exact, d, large)


def _bias_table(rel_bias):
    q_local = jnp.arange(BLOCK, dtype=jnp.int32) + BLOCK
    k_local = jnp.arange(2 * BLOCK, dtype=jnp.int32)
    dist = q_local[:, None] - k_local[None, :]
    band = (dist >= 0) & (dist < WINDOW)
    bucket = _t5_bucket(dist)
    hit = (jnp.arange(N_BUCKETS, dtype=jnp.int32)[:, None] == bucket.reshape(1, -1)).astype(F32)
    bias = jnp.dot(rel_bias.astype(F32).T, hit, precision=lax.Precision.HIGHEST)
    return jnp.where(band[None], bias.reshape(N_Q_HEADS, BLOCK, 2 * BLOCK), NEG_INF)


def _head_indicator(n):
    i = jnp.arange(n) // HEAD_DIM
    return (i[:, None] == i[None, :]).astype(BF16)


CONV_TT = 512
CONV_RC = 64
CONV_CC = 256


def _conv_kernel(u_ref, halo_ref, w_ref, b_ref, lg_ref, lb_ref, go_ref, o_ref, h_ref, y_ref):
    step = pl.program_id(0)

    def glu(u):
        return u[:, :D_CONV] * jax.nn.sigmoid(u[:, D_CONV:])

    hh = glu(halo_ref[...])
    h_ref[0, 0:HALO, :] = jnp.where(step == 0, jnp.zeros_like(hh), hh)
    h_ref[0, HALO:, :] = glu(u_ref[...])

    n_rows = CONV_TT + HALO
    for s in range(1, SUBLANES):
        for c0 in range(0, D_CONV, CONV_CC):
            cs = slice(c0, c0 + CONV_CC)
            h_ref[s, :, cs] = pltpu.roll(h_ref[0, :, cs], n_rows - s, axis=0)

    off = HALO - (CONV_WIDTH - 1)
    for r0 in range(0, CONV_TT, CONV_RC):
        for c0 in range(0, D_CONV, CONV_CC):
            cs = slice(c0, c0 + CONV_CC)
            acc = jnp.broadcast_to(b_ref[:, cs], (CONV_RC, CONV_CC))
            for j in range(CONV_WIDTH):
                s = (r0 + off + j) % SUBLANES
                a = r0 + off + j - s
                acc = acc + w_ref[j:j + 1, cs] * h_ref[s, a:a + CONV_RC, cs]
            y_ref[r0:r0 + CONV_RC, cs] = acc

    y = y_ref[...]
    mu = jnp.mean(y, axis=-1, keepdims=True)
    yc = y - mu
    var = jnp.mean(yc * yc, axis=-1, keepdims=True)
    z = yc * lax.rsqrt(var + EPS) * lg_ref[...] + lb_ref[...]
    s = z * jax.nn.sigmoid(z)
    ms = jnp.mean(s * s, axis=-1, keepdims=True)
    o_ref[...] = (s * lax.rsqrt(ms + EPS) * go_ref[...]).astype(BF16)


def _conv(uc, w_dw, b_dw, ln_g, ln_b, g_out):
    t = uc.shape[0]
    vec = lambda: pl.BlockSpec((1, D_CONV), lambda i: (0, 0))
    per = CONV_TT // HALO
    return pl.pallas_call(
        _conv_kernel,
        grid=(t // CONV_TT,),
        in_specs=[
            pl.BlockSpec((CONV_TT, 2 * D_CONV), lambda i: (i, 0)),
            pl.BlockSpec((HALO, 2 * D_CONV), lambda i: (jnp.maximum(i * per - 1, 0), 0)),
            pl.BlockSpec((HALO, D_CONV), lambda i: (0, 0)),
            vec(), vec(), vec(), vec(),
        ],
        out_specs=pl.BlockSpec((CONV_TT, D_CONV), lambda i: (i, 0)),
        out_shape=jax.ShapeDtypeStruct((t, D_CONV), BF16),
        scratch_shapes=[pltpu.VMEM((SUBLANES, CONV_TT + HALO, D_CONV), F32),
                        pltpu.VMEM((CONV_TT, D_CONV), F32)],
        compiler_params=_params(("parallel",)),
        name="conv",
    )(uc, uc, w_dw, b_dw, ln_g, ln_b, g_out)


OUT_TM = 512
D_PACK = D_MODEL // 2
HI_MASK = 0xFFFF0000


def _pack_bf16_pairs(h):
    lo = lax.bitcast_convert_type(h[:, :D_PACK].astype(BF16).astype(F32), jnp.uint32)
    hi = lax.bitcast_convert_type(h[:, D_PACK:].astype(BF16).astype(F32), jnp.uint32)
    return (lo >> 16) | (hi & jnp.uint32(HI_MASK))


def _unpack_pairs_f32(w):
    lo = lax.bitcast_convert_type(w << 16, F32)
    hi = lax.bitcast_convert_type(w & jnp.uint32(HI_MASK), F32)
    return lo, hi


def _unpack_bf16_pairs(w):
    lo, hi = _unpack_pairs_f32(w)
    return lo.astype(BF16), hi.astype(BF16)


def _out_kernel(ma_ref, mc_ref, x_ref, w_ref, bo_ref, g1_ref, g2_ref, sc_ref, sh_ref, wr_ref, br_ref,
                tri_ref, xm_ref, h2_ref, idx_ref, rank_ref, wt_ref, cnt_ref, carry_ref):
    step = pl.program_id(0)

    @pl.when(step == 0)
    def _():
        carry_ref[...] = jnp.zeros_like(carry_ref)

    y = (jnp.dot(ma_ref[...], w_ref[0:D_ATTN, :], preferred_element_type=F32)
         + jnp.dot(mc_ref[...], w_ref[D_ATTN:, :], preferred_element_type=F32) + bo_ref[...])
    xm = x_ref[...] + g1_ref[...] * y
    xm_ref[...] = xm
    h2 = _modulated_rms(xm, g2_ref[...], sc_ref[...], sh_ref[...])
    h2_ref[...] = _pack_bf16_pairs(h2)

    both = jnp.dot(h2.astype(BF16), wr_ref[...], preferred_element_type=F32)
    logits = both[:, :N_EXPERTS] + both[:, N_EXPERTS:] + br_ref[...]
    tm = logits.shape[0]
    lane = lax.broadcasted_iota(jnp.int32, (tm, N_EXPERTS), 1).astype(F32)
    vals, idxs = [], []
    l = logits
    for _ in range(TOP_K):
        m = jnp.max(l, axis=-1, keepdims=True)
        i = jnp.min(jnp.where(l == m, lane, float(N_EXPERTS)), axis=-1, keepdims=True)
        vals.append(m)
        idxs.append(i)
        l = jnp.where(lane == i, -jnp.inf, l)
    es = [jnp.exp(v - vals[0]) for v in vals]
    tot = es[0] + es[1] + es[2] + es[3]
    ws = [e / tot for e in es]

    hot = [(lane == i).astype(F32) for i in idxs]
    hot_all = hot[0] + hot[1] + hot[2] + hot[3]
    before = jnp.dot(tri_ref[...], hot_all.astype(BF16), preferred_element_type=F32) + carry_ref[...]
    ranks = [jnp.sum(h * before, axis=-1, keepdims=True) for h in hot]
    carry_ref[...] = carry_ref[...] + jnp.sum(hot_all, axis=0, keepdims=True)
    cnt_ref[...] = carry_ref[...].astype(jnp.int32)

    slot = lax.broadcasted_iota(jnp.int32, (tm, LANES), 1)

    def pack(cols):
        out = jnp.zeros((tm, LANES), F32)
        for k in range(TOP_K):
            out = jnp.where(slot == k, cols[k], out)
        return out

    idx_ref[...] = pack(idxs).astype(jnp.int32)
    rank_ref[...] = pack(ranks).astype(jnp.int32)
    wt_ref[...] = pack(ws)


def _out_proj(ma, mc, x2, w_out_bf, b_out, gate1, g2, scale2, shift2, w_router, b_router):
    t = x2.shape[0]
    vec = lambda n: pl.BlockSpec((1, n), lambda i: (0, 0))
    tri = jnp.tril(jnp.ones((OUT_TM, OUT_TM), F32), -1).astype(BF16)
    wr_hi = w_router.astype(BF16)
    wr_lo = (w_router - wr_hi.astype(F32)).astype(BF16)
    w_router = jnp.concatenate([wr_hi, wr_lo], axis=1)
    return pl.pallas_call(
        _out_kernel,
        grid=(t // OUT_TM,),
        in_specs=[
            pl.BlockSpec((OUT_TM, D_ATTN), lambda i: (i, 0)),
            pl.BlockSpec((OUT_TM, D_CONV), lambda i: (i, 0)),
            pl.BlockSpec((OUT_TM, D_MODEL), lambda i: (i, 0)),
            pl.BlockSpec((D_MODEL, D_MODEL), lambda i: (0, 0)),
            vec(D_MODEL), vec(D_MODEL), vec(D_MODEL), vec(D_MODEL), vec(D_MODEL),
            pl.BlockSpec((D_MODEL, 2 * N_EXPERTS), lambda i: (0, 0)),
            vec(N_EXPERTS),
            pl.BlockSpec((OUT_TM, OUT_TM), lambda i: (0, 0)),
        ],
        out_specs=[
            pl.BlockSpec((OUT_TM, D_MODEL), lambda i: (i, 0)),
            pl.BlockSpec((OUT_TM, D_PACK), lambda i: (i, 0)),
            pl.BlockSpec((OUT_TM, LANES), lambda i: (i, 0)),
            pl.BlockSpec((OUT_TM, LANES), lambda i: (i, 0)),
            pl.BlockSpec((OUT_TM, LANES), lambda i: (i, 0)),
            pl.BlockSpec((1, N_EXPERTS), lambda i: (0, 0)),
        ],
        out_shape=[
            jax.ShapeDtypeStruct((t, D_MODEL), F32),
            jax.ShapeDtypeStruct((t, D_PACK), jnp.uint32),
            jax.ShapeDtypeStruct((t, LANES), jnp.int32),
            jax.ShapeDtypeStruct((t, LANES), jnp.int32),
            jax.ShapeDtypeStruct((t, LANES), F32),
            jax.ShapeDtypeStruct((1, N_EXPERTS), jnp.int32),
        ],
        scratch_shapes=[pltpu.VMEM((1, N_EXPERTS), F32)],
        compiler_params=_params(("arbitrary",)),
        name="out_proj",
    )(ma, mc, x2, w_out_bf, b_out, gate1, g2, scale2, shift2, w_router, b_router, tri)


DISP_TD = 512
DISP_NBUF = 3


class _CopyGroup:
    def __init__(self, copies):
        self.copies = copies

    def start(self):
        for cp in self.copies:
            cp.start()

    def wait(self):
        for cp in self.copies:
            cp.wait()

ZERO_RUN = 64


def _dispatch_kernel(dest_ref, pad_lo_ref, pad_hi_ref, nu_ref, h2_ref, xs_ref,
                     stage_ref, zero_ref, in_sems, out_sems, zsem):
    step = pl.program_id(0)
    last = pl.num_programs(0) - 1
    slot = lax.rem(step, DISP_NBUF)
    rows_per_chunk = DISP_TD * TOP_K

    def load(chunk, s):
        rows = pl.ds(pl.multiple_of(chunk * DISP_TD, DISP_TD), DISP_TD)
        return _CopyGroup([pltpu.make_async_copy(h2_ref.at[rows, pl.ds(j * LANES, LANES)], stage_ref.at[s, :, j, :],
                                                 in_sems.at[s]) for j in range(stage_ref.shape[2])])

    def wait_chunk(s):
        whole = xs_ref.at[pl.ds(0, rows_per_chunk)]
        pltpu.make_async_copy(whole, whole, out_sems.at[s]).wait()

    @pl.when(step == 0)
    def _():
        load(0, 0).start()

        @pl.when(last >= 1)
        def _():
            load(1, 1).start()

    load(step, slot).wait()

    def issue(r, carry):
        for k in range(TOP_K):
            d = dest_ref[(step * DISP_TD + r) * TOP_K + k]
            pltpu.make_async_copy(stage_ref.at[slot, r], xs_ref.at[d], out_sems.at[slot]).start(priority=k % 2)
        return carry

    lax.fori_loop(0, DISP_TD, issue, 0, unroll=2)

    @pl.when(step == 0)
    def _():
        zero_ref[...] = jnp.zeros_like(zero_ref)

        def zero_rows(d, n):
            d = d if n == 1 else pl.multiple_of(d, n)
            return pltpu.make_async_copy(zero_ref.at[pl.ds(0, n)], xs_ref.at[pl.ds(d, n)], zsem)

        def ranges(e):
            lo, hi = pad_lo_ref[e], pad_hi_ref[e]
            a = jnp.minimum((lo + SUBLANES - 1) // SUBLANES * SUBLANES, hi)
            b = jnp.minimum((lo + ZERO_RUN - 1) // ZERO_RUN * ZERO_RUN, hi)
            return ((lo, a - lo, 1), (a, (b - a) // SUBLANES, SUBLANES), (b, (hi - b) // ZERO_RUN, ZERO_RUN))

        def for_all_runs(act):
            def per_expert(e, carry):
                for start, count, n in ranges(e):
                    lax.fori_loop(0, count, lambda i, c, s=start, n=n: (act(zero_rows(s + i * n, n)), c)[1], 0)
                return carry

            lax.fori_loop(0, N_EXPERTS, per_expert, 0)
            n_blocks = xs_ref.shape[0] // EXPERT_TM
            lax.fori_loop(nu_ref[0], n_blocks,
                          lambda blk, c: (act(zero_rows(blk * EXPERT_TM, EXPERT_TM)), c)[1], 0)

        for_all_runs(lambda copy: copy.start())
        for_all_runs(lambda copy: copy.wait())

    @pl.when(step > 0)
    def _():
        wait_chunk(lax.rem(step + DISP_NBUF - 1, DISP_NBUF))

    @pl.when(step + 2 <= last)
    def _():
        load(step + 2, lax.rem(step + 2, DISP_NBUF)).start()

    @pl.when(step == last)
    def _():
        wait_chunk(slot)


def _dispatch(h2p, dest_flat, pad_lo, pad_hi, n_used, n_rows):
    t = h2p.shape[0]
    grid_spec = pltpu.PrefetchScalarGridSpec(
        num_scalar_prefetch=4,
        grid=(t // DISP_TD,),
        in_specs=[pl.BlockSpec(memory_space=pl.ANY)],
        out_specs=pl.BlockSpec(memory_space=pl.ANY),
        scratch_shapes=[pltpu.VMEM((DISP_NBUF, DISP_TD, D_PACK // LANES, LANES), jnp.uint32),
                        pltpu.VMEM((EXPERT_TM, D_PACK // LANES, LANES), jnp.uint32),
                        pltpu.SemaphoreType.DMA((DISP_NBUF,)), pltpu.SemaphoreType.DMA((DISP_NBUF,)),
                        pltpu.SemaphoreType.DMA(())],
    )
    return pl.pallas_call(
        _dispatch_kernel,
        grid_spec=grid_spec,
        out_shape=jax.ShapeDtypeStruct((n_rows, D_PACK // LANES, LANES), jnp.uint32),
        compiler_params=_params(("arbitrary",)),
        name="dispatch",
    )(dest_flat, pad_lo, pad_hi, n_used, h2p)


UP_TF = 1024
DOWN_TN = 2048
UP_AHEAD = 2


def _weight_index_map(n_chunks):
    def index_map(c, b, be, nu, first, nxt, full):
        del nu, full
        in_last = nxt[b] < 0
        wrap = jnp.logical_and(in_last, c + 1 < n_chunks)
        e_next = jnp.where(in_last, jnp.where(wrap, be[0], be[b]), nxt[b])
        c_next = jnp.where(wrap, c + 1, c)
        is_first = first[b] == 1
        return jnp.where(is_first, be[b], e_next), 0, jnp.where(is_first, c, c_next)

    return index_map


def _row_cases(b, nu_ref, full_ref, o_ref, compute):
    half = EXPERT_TM // 2
    used = b < nu_ref[0]

    @pl.when(jnp.logical_and(used, full_ref[b] == 1))
    def _():
        compute(EXPERT_TM)

    @pl.when(jnp.logical_and(used, full_ref[b] == 0))
    def _():
        compute(half)
        o_ref[half:, :] = jnp.zeros((half, o_ref.shape[1]), o_ref.dtype)

    @pl.when(jnp.logical_not(used))
    def _():
        o_ref[...] = jnp.zeros(o_ref.shape, o_ref.dtype)


def _up_kernel(be_ref, nu_ref, first_ref, nxt_ref, full_ref, xs_ref, wg_ref, wu_ref, bg_ref, bu_ref, h_ref,
               wg_bf, wu_bf, xbuf_ref, xsems):
    f = pl.program_id(0)
    b = pl.program_id(1)
    nb = pl.num_programs(1)
    step = f * nb + b
    n_steps = pl.num_programs(0) * nb
    n_buf = UP_AHEAD + 1
    slot = lax.rem(step, n_buf)
    n_sub = xs_ref.shape[1]

    def fetch(ahead):
        blk = lax.rem(b + ahead, nb)
        rows = pl.ds(pl.multiple_of(jnp.minimum(blk, nu_ref[0] - 1) * EXPERT_TM, EXPERT_TM), EXPERT_TM)
        s = lax.rem(step + ahead, n_buf)
        return _CopyGroup([pltpu.make_async_copy(xs_ref.at[rows, j, :], xbuf_ref.at[s, :, pl.ds(j * LANES, LANES)],
                                                 xsems.at[s]) for j in range(n_sub)])

    @pl.when(step == 0)
    def _():
        for ahead in range(UP_AHEAD):
            @pl.when(ahead < n_steps)
            def _():
                fetch(ahead).start()

    @pl.when(step + UP_AHEAD < n_steps)
    def _():
        fetch(UP_AHEAD).start()

    fetch(0).wait()
    x_ref = xbuf_ref.at[slot]

    @pl.when(first_ref[b] == 1)
    def _():
        wg_bf[...] = wg_ref[0].astype(BF16)
        wu_bf[...] = wu_ref[0].astype(BF16)

    def compute(m):
        x_lo, x_hi = _unpack_bf16_pairs(x_ref[0:m, :])

        def proj(w_bf, bias_ref):
            return (jnp.dot(x_lo, w_bf[0:D_PACK, :], preferred_element_type=F32)
                    + jnp.dot(x_hi, w_bf[D_PACK:, :], preferred_element_type=F32) + bias_ref[0])

        g = proj(wg_bf, bg_ref)
        lin = proj(wu_bf, bu_ref)
        g = jnp.minimum(g, SWIGLU_LIMIT)
        lin = jnp.clip(lin, -SWIGLU_LIMIT, SWIGLU_LIMIT)
        act = g * jax.nn.sigmoid(SWIGLU_ALPHA * g) * (lin + 1.0)
        h_ref[0:m, :] = act.astype(BF16)

    _row_cases(b, nu_ref, full_ref, h_ref, compute)


def _up(xs, tables, w_gate, b_gate, w_up, b_up):
    n_rows = xs.shape[0]
    nb = n_rows // EXPERT_TM
    n_chunks = D_FF // UP_TF
    wsel = _weight_index_map(n_chunks)
    bsel = lambda f, b, be, nu, first, nxt, full: (be[b], 0, f)
    grid_spec = pltpu.PrefetchScalarGridSpec(
        num_scalar_prefetch=5,
        grid=(n_chunks, nb),
        in_specs=[
            pl.BlockSpec(memory_space=pl.ANY),
            pl.BlockSpec((1, D_MODEL, UP_TF), wsel),
            pl.BlockSpec((1, D_MODEL, UP_TF), wsel),
            pl.BlockSpec((1, 1, UP_TF), bsel),
            pl.BlockSpec((1, 1, UP_TF), bsel),
        ],
        out_specs=pl.BlockSpec((EXPERT_TM, UP_TF), lambda f, b, be, nu, first, nxt, full: (b, f)),
        scratch_shapes=[pltpu.VMEM((D_MODEL, UP_TF), BF16), pltpu.VMEM((D_MODEL, UP_TF), BF16),
                        pltpu.VMEM((UP_AHEAD + 1, EXPERT_TM, D_PACK), jnp.uint32),
                        pltpu.SemaphoreType.DMA((UP_AHEAD + 1,))],
    )
    return pl.pallas_call(
        _up_kernel,
        grid_spec=grid_spec,
        out_shape=jax.ShapeDtypeStruct((n_rows, D_FF), BF16),
        compiler_params=_params(("arbitrary", "arbitrary"), EXPERT_VMEM_LIMIT),
        name="expert_up",
    )(*tables, xs, w_gate, w_up,
      b_gate.reshape(N_EXPERTS, 1, D_FF), b_up.reshape(N_EXPERTS, 1, D_FF))


def _down_kernel(be_ref, nu_ref, first_ref, nxt_ref, full_ref, h_ref, wd_ref, bd_ref, ys_ref, wd_bf,
                 ybuf_ref, ysems):
    b = pl.program_id(1)
    last = pl.num_programs(1) - 1
    slot = b % 2

    def write_back(blk, s):
        rows = pl.ds(pl.multiple_of(blk * EXPERT_TM, EXPERT_TM), EXPERT_TM)
        return _CopyGroup([pltpu.make_async_copy(ybuf_ref.at[s, :, pl.ds(j * LANES, LANES)], ys_ref.at[rows, j, :],
                                                 ysems.at[s]) for j in range(ys_ref.shape[1])])

    @pl.when(b >= 2)
    def _():
        write_back(b - 2, slot).wait()

    @pl.when(first_ref[b] == 1)
    def _():
        wd_bf[...] = wd_ref[0].astype(BF16)

    y_ref = ybuf_ref.at[slot]

    def compute(m):
        y = jnp.dot(h_ref[0:m, :], wd_bf[...], preferred_element_type=F32) + bd_ref[0]
        y_ref[0:m, :] = _pack_bf16_pairs(y)

    _row_cases(b, nu_ref, full_ref, y_ref, compute)
    write_back(b, slot).start()

    @pl.when(b == last)
    def _():
        @pl.when(b >= 1)
        def _():
            write_back(b - 1, 1 - slot).wait()

        write_back(b, slot).wait()


def _down(hs, tables, w_down, b_down):
    n_rows = hs.shape[0]
    nb = n_rows // EXPERT_TM
    n_chunks = D_MODEL // DOWN_TN
    assert n_chunks == 1
    row = lambda n, b, be, nu, first, nxt, full: (jnp.minimum(b, nu[0] - 1), 0)
    grid_spec = pltpu.PrefetchScalarGridSpec(
        num_scalar_prefetch=5,
        grid=(n_chunks, nb),
        in_specs=[
            pl.BlockSpec((EXPERT_TM, D_FF), row),
            pl.BlockSpec((1, D_FF, DOWN_TN), _weight_index_map(n_chunks)),
            pl.BlockSpec((1, 1, DOWN_TN), lambda n, b, be, nu, first, nxt, full: (be[b], 0, n)),
        ],
        out_specs=pl.BlockSpec(memory_space=pl.ANY),
        scratch_shapes=[pltpu.VMEM((D_FF, DOWN_TN), BF16),
                        pltpu.VMEM((2, EXPERT_TM, D_PACK), jnp.uint32), pltpu.SemaphoreType.DMA((2,))],
    )
    return pl.pallas_call(
        _down_kernel,
        grid_spec=grid_spec,
        out_shape=jax.ShapeDtypeStruct((n_rows, D_PACK // LANES, LANES), jnp.uint32),
        compiler_params=_params(("arbitrary", "arbitrary"), EXPERT_VMEM_LIMIT),
        name="expert_down",
    )(*tables, hs, w_down, b_down.reshape(N_EXPERTS, 1, D_MODEL))


COMB_TC = 256


def _combine_kernel(dest_ref, wsm_ref, xm_ref, g2_ref, ys_ref, o_ref, buf_ref, acc_ref, sems):
    step = pl.program_id(0)
    last = pl.num_programs(0) - 1
    slot = step % 2

    def gather_tile(tile, s):
        def issue(r, carry):
            for k in range(TOP_K):
                d = dest_ref[(tile * COMB_TC + r) * TOP_K + k]
                pltpu.make_async_copy(ys_ref.at[d], buf_ref.at[s, k * COMB_TC + r],
                                      sems.at[s]).start(priority=k % 2)
            return carry

        lax.fori_loop(0, COMB_TC, issue, 0, unroll=2)

    @pl.when(step == 0)
    def _():
        gather_tile(0, 0)

    @pl.when(step < last)
    def _():
        gather_tile(step + 1, 1 - slot)

    pltpu.make_async_copy(ys_ref.at[pl.ds(0, TOP_K * COMB_TC)], buf_ref.at[slot], sems.at[slot]).wait()

    def reduce_token(r, carry):
        acc_lo = acc_hi = None
        for k in range(TOP_K):
            w = wsm_ref[(step * COMB_TC + r) * TOP_K + k]
            lo, hi = _unpack_pairs_f32(buf_ref[slot, k * COMB_TC + r])
            acc_lo = w * lo if acc_lo is None else acc_lo + w * lo
            acc_hi = w * hi if acc_hi is None else acc_hi + w * hi
        acc_ref[0, r] = acc_lo
        acc_ref[1, r] = acc_hi
        return carry

    lax.fori_loop(0, COMB_TC, reduce_token, 0, unroll=8)

    for half in range(2):
        for j in range(D_PACK // LANES):
            cols = slice(half * D_PACK + j * LANES, half * D_PACK + (j + 1) * LANES)
            o_ref[:, cols] = xm_ref[:, cols] + g2_ref[:, cols] * acc_ref[half, :, j, :]


def _combine(ys, dest_flat, xm, wts_flat, gate2):
    t = xm.shape[0]
    n_sub = D_PACK // LANES
    grid_spec = pltpu.PrefetchScalarGridSpec(
        num_scalar_prefetch=2,
        grid=(t // COMB_TC,),
        in_specs=[
            pl.BlockSpec((COMB_TC, D_MODEL), lambda i, d, w: (i, 0)),
            pl.BlockSpec((1, D_MODEL), lambda i, d, w: (0, 0)),
            pl.BlockSpec(memory_space=pl.ANY),
        ],
        out_specs=pl.BlockSpec((COMB_TC, D_MODEL), lambda i, d, w: (i, 0)),
        scratch_shapes=[pltpu.VMEM((2, TOP_K * COMB_TC, n_sub, LANES), jnp.uint32),
                        pltpu.VMEM((2, COMB_TC, n_sub, LANES), F32),
                        pltpu.SemaphoreType.DMA((2,))],
    )
    return pl.pallas_call(
        _combine_kernel,
        grid_spec=grid_spec,
        out_shape=jax.ShapeDtypeStruct((t, D_MODEL), F32),
        compiler_params=_params(("arbitrary",)),
        name="combine",
    )(dest_flat, wts_flat, xm, gate2, ys)


def _routing_tables(idx, rank, counts, n_tok):
    padded = ((counts + EXPERT_TM - 1) // EXPERT_TM) * EXPERT_TM
    pend = jnp.cumsum(padded)
    pstart = pend - padded
    experts = jnp.arange(N_EXPERTS, dtype=jnp.int32)
    start_of = jnp.sum(jnp.where(idx[..., None] == experts, pstart, 0), axis=-1)
    dest = (start_of + rank).astype(jnp.int32).reshape(-1)
    n_rows = n_tok * TOP_K + N_EXPERTS * EXPERT_TM
    nb = n_rows // EXPERT_TM
    block_start = jnp.arange(nb, dtype=jnp.int32) * EXPERT_TM
    block_expert = jnp.minimum(jnp.sum(pend[None, :] <= block_start[:, None], axis=1),
                               N_EXPERTS - 1).astype(jnp.int32)
    n_used = (pend[-1:] // EXPERT_TM).astype(jnp.int32)
    pad_lo = (pstart + counts).astype(jnp.int32)
    pad_hi = pend.astype(jnp.int32)

    blocks = jnp.arange(nb, dtype=jnp.int32)
    block_expert = jnp.where(blocks < n_used[0], block_expert, block_expert[jnp.maximum(n_used[0] - 1, 0)])
    prev = jnp.concatenate([block_expert[:1], block_expert[:-1]])
    first = jnp.logical_or(blocks == 0, block_expert != prev)
    first_pos = jnp.where(first, blocks, nb)
    next_first = jnp.flip(lax.cummin(jnp.flip(jnp.concatenate([first_pos[1:], jnp.full((1,), nb, jnp.int32)]))))
    nxt = jnp.where(next_first < nb, block_expert[jnp.minimum(next_first, nb - 1)], -1).astype(jnp.int32)
    rows_end = jnp.sum(jnp.where(block_expert[:, None] == experts, pad_lo, 0), axis=-1)
    full = (rows_end > block_start + EXPERT_TM // 2).astype(jnp.int32)
    tables = (block_expert, n_used, first.astype(jnp.int32), nxt, full)
    return dest, tables, pad_lo, pad_hi, n_rows


def _layer(x2, c, w_ada, b_ada, g_norm1, w_in, b_in, g_q, g_k, sinks, rel_bias, w_dw, b_dw, ln_g, ln_b,
           g_out_attn, g_out_conv, w_out, b_out, g_norm2, w_router, b_router,
           w_gate, b_gate, w_up, b_up, w_down, b_down):
    t = x2.shape[0]
    row = lambda v: v.reshape(1, -1)
    mod = _ada(c, w_ada, b_ada)
    shift1, scale1, gate1, shift2, scale2, gate2 = [mod[:, i * D_MODEL:(i + 1) * D_MODEL] for i in range(6)]

    q, kv, uc = _in_proj(x2, row(g_norm1), scale1, shift1, w_in.astype(BF16), row(b_in))

    gq_t = row(jnp.tile(g_q, N_Q_HEADS)) * (HEAD_DIM ** -0.5)
    gk_t = row(jnp.tile(g_k, N_KV_HEADS))
    ma = _attention(q, kv, _bias_table(rel_bias), sinks, gq_t, gk_t, row(g_out_attn))
    w_dw_p = jnp.concatenate([w_dw, jnp.zeros((HALO - CONV_WIDTH, D_CONV), w_dw.dtype)], axis=0)
    mc = _conv(uc, w_dw_p, row(b_dw), row(ln_g), row(ln_b), row(g_out_conv))

    xm, h2, idx, rank, wts, counts = _out_proj(
        ma, mc, x2, w_out.astype(BF16), row(b_out), gate1, row(g_norm2), scale2, shift2,
        w_router, row(b_router))

    dest, tables, pad_lo, pad_hi, n_rows = _routing_tables(idx[:, :TOP_K], rank[:, :TOP_K], counts[0], t)
    xs = _dispatch(h2, dest, pad_lo, pad_hi, tables[1], n_rows)
    hs = _up(xs, tables, w_gate, b_gate, w_up, b_up)
    ys = _down(hs, tables, w_down, b_down)
    return _combine(ys, dest, xm, wts[:, :TOP_K].reshape(-1), gate2)


def kernel(x, c, w_ada, b_ada, g_norm1, w_in, b_in, g_q, g_k, sinks, rel_bias, w_dw, b_dw, ln_g, ln_b,
           g_out_attn, g_out_conv, w_out, b_out, g_norm2, w_router, b_router,
           w_gate, b_gate, w_up, b_up, w_down, b_down):
    b, t, d = x.shape
    assert b == 1 and d == D_MODEL and w_ada.shape[0] == 1
    out = _layer(x.reshape(t, d), c, w_ada[0], b_ada[0], g_norm1[0], w_in[0], b_in[0], g_q[0], g_k[0],
                 sinks[0], rel_bias, w_dw[0], b_dw[0], ln_g[0], ln_b[0], g_out_attn[0], g_out_conv[0],
                 w_out[0], b_out[0], g_norm2[0], w_router[0], b_router[0],
                 w_gate[0], b_gate[0], w_up[0], b_up[0], w_down[0], b_down[0])
    return out.reshape(b, t, d)
```
